```python
import jax
import jax.numpy as jnp
from jax import lax
import numpy as np

D_MODEL = 2048
BATCH = 4
SEQ = 2048
DEPTH = 2
DEC_BATCH = 8
DEC_SEQ = 1
PAST_LEN = 16384
PAGE_SIZE = 128

D_A = D_MODEL // 2
HEAD_DIM_A = 128
N_HEADS_A = D_A // HEAD_DIM_A
N_KV = 2
GQA_R = N_HEADS_A // N_KV
BLK = 64
N_SEL = 16
WINDOW = 512
CMP_HID = 2 * HEAD_DIM_A
WIN_QBLK = 128
SEL_QBLK = 32
SCALE_A = HEAD_DIM_A ** -0.5
D_B = D_MODEL - D_A
HEAD_DIM_B = 64
N_HEADS_B = D_B // HEAD_DIM_B
LORA_W = 64
LORA_A = 64
LORA_G = 32
GN_EPS = HEAD_DIM_B * 1e-5
N_GROUPS = 4
E_PER_GROUP = 8
N_EXPERTS = N_GROUPS * E_PER_GROUP
TOP_K_EXP = 2
D_EXPERT = 512
MOE_BLOCK = 128
RMS_EPS = 1e-6
Q_W = N_HEADS_A * HEAD_DIM_A
KV_W = N_KV * HEAD_DIM_A
OFF_KV = Q_W
OFF_GATE_A = OFF_KV + 6 * KV_W
OFF_RWKV = OFF_GATE_A + 3 * N_HEADS_A
SHIFT_W = 3 * D_B + LORA_W + LORA_A + LORA_G
OFF_MERGE = OFF_RWKV + SHIFT_W
IN_W = OFF_MERGE + 2 * D_MODEL

kernel_name = 'nsa_rwkv7_hier_moe_step'


def _rmsnorm(x, g):
    xf = x.astype(jnp.float32)
    y = xf * lax.rsqrt(jnp.mean(xf * xf, axis=-1, keepdims=True) + RMS_EPS)
    return (y * g.astype(jnp.float32)).astype(x.dtype)


def _modulate(x, g, shift, scale):
    return _rmsnorm(x, g) * (1 + scale[:, None, :]) + shift[:, None, :]


def _alibi_slopes():
    h = jnp.arange(N_HEADS_A, dtype=jnp.float32)
    return jnp.exp2(-8.0 * (h + 1.0) / N_HEADS_A).reshape(N_KV, GQA_R)


def _masked_softmax(s, mask, axis):
    s = jnp.where(mask, s.astype(jnp.float32), -jnp.inf)
    m = jnp.max(s, axis=axis, keepdims=True)
    m = jnp.where(jnp.isfinite(m), m, 0.0)
    p = jnp.where(mask, jnp.exp(s - m), 0.0)
    den = jnp.sum(p, axis=axis, keepdims=True)
    return p / jnp.where(den > 0, den, 1.0)


def _compress(rows, w1, b1, w2):
    B, L = rows.shape[:2]
    nb = L // BLK
    flat = rows.reshape(B, nb, BLK, N_KV, HEAD_DIM_A).transpose(0, 1, 3, 2, 4).reshape(B, nb, N_KV, BLK * HEAD_DIM_A)
    return jax.nn.gelu(flat @ w1 + b1) @ w2


def _cmp_attend(q, pos_q, k_cmp, v_cmp, slopes):
    nb = k_cmp.shape[1]
    blk_end = (jnp.arange(nb) + 1) * BLK - 1
    dist = pos_q[:, None] - blk_end[None, :]
    valid = dist >= 0
    s = jnp.einsum('bqgrd,bngd->bqgrn', q, k_cmp).astype(jnp.float32) * SCALE_A
    s = s - slopes[None, None, :, :, None] * dist.astype(jnp.float32)[None, :, None, None, :]
    p = _masked_softmax(s, valid[None, :, None, None, :], -1)
    o = jnp.einsum('bqgrn,bngd->bqgrd', p.astype(v_cmp.dtype), v_cmp)
    return o, p


def _select_blocks(p_cmp, pos_q):
    nb = p_cmp.shape[-1]
    imp = jnp.sum(p_cmp, axis=3)
    cur = (pos_q // BLK)[None, :, None, None]
    j = jnp.arange(nb)[None, None, None, :]
    score = jnp.where(j < cur, imp, jnp.where(j == cur, GQA_R + 1.0, -1.0))
    _, idx = lax.top_k(score, min(N_SEL, nb))
    return idx, idx <= cur


def _sel_attend(q, pos_q, kv_sel, idx, ok, slopes):
    k = kv_sel[..., 0, :]
    v = kv_sel[..., 1, :]
    s_pos = idx[..., None] * BLK + jnp.arange(BLK)
    dist = pos_q[None, :, None, None, None] - s_pos
    valid = ok[..., None] & (dist >= 0)
    s = jnp.einsum('bqgrd,bqgnkd->bqgrnk', q, k).astype(jnp.float32) * SCALE_A
    s = s - slopes[None, None, :, :, None, None] * dist.astype(jnp.float32)[:, :, :, None]
    p = _masked_softmax(s, valid[:, :, :, None], (-2, -1))
    return jnp.einsum('bqgrnk,bqgnkd->bqgrd', p.astype(v.dtype), v)


def _sel_prompt(q, pos_q, kv_slc, idx, ok, slopes):
    B, T = q.shape[:2]
    nb = T // BLK
    kvb = kv_slc.reshape(B, nb, BLK, 2, N_KV, HEAD_DIM_A).transpose(0, 4, 1, 2, 3, 5)
    qb = min(SEL_QBLK, T)
    nc = T // qb
    bi = jnp.arange(B)[:, None, None, None]
    gi = jnp.arange(N_KV)[None, None, :, None]

    def chunk(args):
        qc, pc, ic, oc = args
        kv_c = kvb[bi, gi, ic]
        return _sel_attend(qc, pc, kv_c, ic, oc, slopes)

    def split(a):
        return a.reshape(B, nc, qb, *a.shape[2:]).swapaxes(0, 1)

    out = lax.map(chunk, (split(q), pos_q.reshape(nc, qb), split(idx), split(ok)))
    return out.swapaxes(0, 1).reshape(q.shape)


def _win_prompt(q, kv_win, slopes):
    B, T = q.shape[:2]
    qw = min(WIN_QBLK, T)
    nq = T // qw
    span = WINDOW + qw
    kv_pad = jnp.pad(kv_win, ((0, 0), (WINDOW, 0), (0, 0), (0, 0), (0, 0)))
    key_idx = jnp.arange(nq)[:, None] * qw + jnp.arange(span)[None, :]
    kvb = kv_pad[:, key_idx]
    pos_k = key_idx - WINDOW
    pos_q = jnp.arange(T).reshape(nq, qw)
    dist = pos_q[:, :, None] - pos_k[:, None, :]
    valid = (dist >= 0) & (dist < WINDOW) & (pos_k >= 0)[:, None, :]
    qb = q.reshape(B, nq, qw, N_KV, GQA_R, HEAD_DIM_A)
    s = jnp.einsum('bnqgrd,bnkgd->bnqgrk', qb, kvb[..., 0, :, :]).astype(jnp.float32) * SCALE_A
    s = s - slopes[None, None, None, :, :, None] * dist.astype(jnp.float32)[None, :, :, None, None, :]
    p = _masked_softmax(s, valid[None, :, :, None, None, :], -1)
    o = jnp.einsum('bnqgrk,bnkgd->bnqgrd', p.astype(q.dtype), kvb[..., 1, :, :])
    return o.reshape(q.shape)


def _win_direct(q, pos_q, keys, pos_k, slopes):
    dist = pos_q[:, None] - pos_k[None, :]
    valid = (dist >= 0) & (dist < WINDOW)
    s = jnp.einsum('bqgrd,bkgd->bqgrk', q, keys[:, :, 0]).astype(jnp.float32) * SCALE_A
    s = s - slopes[None, None, :, :, None] * dist.astype(jnp.float32)[None, :, None, None, :]
    p = _masked_softmax(s, valid[None, :, None, None, :], -1)
    return jnp.einsum('bqgrk,bkgd->bqgrd', p.astype(q.dtype), keys[:, :, 1])


def _nsa_split(p):
    B, T = p.shape[:2]
    q = p[..., :Q_W].reshape(B, T, N_KV, GQA_R, HEAD_DIM_A)
    kv = p[..., OFF_KV:OFF_GATE_A].reshape(B, T, 3, 2, N_KV, HEAD_DIM_A)
    gates = jax.nn.sigmoid(p[..., OFF_GATE_A:OFF_RWKV].astype(jnp.float32)).reshape(B, T, N_KV, GQA_R, 3)
    return q, kv, gates


def _nsa_combine(o_c, o_s, o_w, gates):
    o = gates[..., 0:1] * o_c + gates[..., 1:2] * o_s + gates[..., 2:3] * o_w
    return o.astype(o_c.dtype).reshape(o_c.shape[0], o_c.shape[1], D_A)


def _nsa_prompt(p, cmp_w1, cmp_b1, cmp_w2, slopes):
    q, kv, gates = _nsa_split(p)
    T = p.shape[1]
    pos = jnp.arange(T)
    k_c = _compress(kv[:, :, 0, 0], cmp_w1[0], cmp_b1[0], cmp_w2[0])
    v_c = _compress(kv[:, :, 0, 1], cmp_w1[1], cmp_b1[1], cmp_w2[1])
    o_c, p_c = _cmp_attend(q, pos, k_c, v_c, slopes)
    idx, ok = _select_blocks(p_c, pos)
    o_s = _sel_prompt(q, pos, kv[:, :, 1], idx, ok, slopes)
    o_w = _win_prompt(q, kv[:, :, 2], slopes)
    o = _nsa_combine(o_c, o_s, o_w, gates)
    return o, kv[:, :, 0], kv[:, :, 1], kv[:, T - min(WINDOW, T):, 2]


def _nsa_sample(p, pool_cmp, pool_slc, win_buf, page_table, cmp_w1, cmp_b1, cmp_w2, slopes):
    q, kv, gates = _nsa_split(p)
    Bd, T = p.shape[:2]
    past = page_table.shape[1] * PAGE_SIZE
    pos = past + jnp.arange(T)
    past_cmp = pool_cmp[page_table].reshape(Bd, past, 2, N_KV, HEAD_DIM_A)
    rows = jnp.concatenate([past_cmp, kv[:, :, 0]], axis=1)
    rows = jnp.pad(rows, ((0, 0), (0, (-rows.shape[1]) % BLK), (0, 0), (0, 0), (0, 0)))
    k_c = _compress(rows[:, :, 0], cmp_w1[0], cmp_b1[0], cmp_w2[0])
    v_c = _compress(rows[:, :, 1], cmp_w1[1], cmp_b1[1], cmp_w2[1])
    o_c, p_c = _cmp_attend(q, pos, k_c, v_c, slopes)
    idx, ok = _select_blocks(p_c, pos)
    nb_past = past // BLK
    bpp = PAGE_SIZE // BLK
    bi = jnp.arange(Bd)[:, None, None, None]
    gi = jnp.arange(N_KV)[None, None, :, None]
    ip = jnp.clip(idx, 0, nb_past - 1)
    phys = page_table[bi, ip // bpp]
    pool_blk = pool_slc.reshape(pool_slc.shape[0], bpp, BLK, 2, N_KV, HEAD_DIM_A)
    kv_past = pool_blk[phys, ip % bpp, :, :, gi]
    new = jnp.pad(kv[:, :, 1], ((0, 0), (0, (-T) % BLK), (0, 0), (0, 0), (0, 0)))
    new_blk = new.reshape(Bd, -1, BLK, 2, N_KV, HEAD_DIM_A)
    inew = jnp.clip(idx - nb_past, 0, new_blk.shape[1] - 1)
    kv_new = new_blk[bi, inew, :, :, gi]
    kv_sel = jnp.where((idx < nb_past)[..., None, None, None], kv_past, kv_new)
    o_s = _sel_attend(q, pos, kv_sel, idx, ok, slopes)
    w_buf = win_buf.shape[1]
    keys = jnp.concatenate([win_buf, kv[:, :, 2]], axis=1)
    pos_k = past - w_buf + jnp.arange(w_buf + T)
    o_w = _win_direct(q, pos, keys, pos_k, slopes)
    o = _nsa_combine(o_c, o_s, o_w, gates)
    return o, kv[:, :, 0], kv[:, :, 1], keys[:, T:]


def _rwkv7(p_rw, prev_row, wkv0, mu, w0, w2, a0, a2, g2, k_k, k_a, r_k, gn_g, gn_b):
    B, T, _ = p_rw.shape
    f32 = jnp.float32
    pf = p_rw.astype(f32)
    shifted = jnp.concatenate([prev_row.astype(f32)[:, None], pf[:, :-1]], axis=1)
    xm = pf + mu * (shifted - pf)
    o = 3 * D_B
    r = xm[..., :D_B]
    k = xm[..., D_B:2 * D_B]
    v = xm[..., 2 * D_B:o]
    dw = xm[..., o:o + LORA_W]
    da = xm[..., o + LORA_W:o + LORA_W + LORA_A]
    dg = xm[..., o + LORA_W + LORA_A:]
    w = -jax.nn.softplus(-(w0 + jnp.tanh(dw) @ w2)) - 0.5
    decay = jnp.exp(-jnp.exp(w))
    a = jax.nn.sigmoid(a0 + da @ a2)
    g = jax.nn.sigmoid(dg) @ g2

    def heads(t):
        return t.reshape(B, T, N_HEADS_B, HEAD_DIM_B)

    kk = heads(k * k_k)
    kk = kk / jnp.maximum(jnp.sqrt(jnp.sum(kk * kk, axis=-1, keepdims=True)), 1e-12)
    k = heads(k * (1.0 + (a - 1.0) * k_a))
    r, v, decay, a = heads(r), heads(v), heads(decay), heads(a)
    b = kk * a

    def step(S, inp):
        r_t, w_t, k_t, v_t, kk_t, b_t = inp
        sa = jnp.einsum('bhvk,bhk->bhv', S, -kk_t)
        S = S * w_t[:, :, None, :] + sa[..., None] * b_t[:, :, None, :] + v_t[..., None] * k_t[:, :, None, :]
        return S, jnp.einsum('bhvk,bhk->bhv', S, r_t)

    seq = tuple(t.swapaxes(0, 1) for t in (r, decay, k, v, kk, b))
    S_fin, y = lax.scan(step, wkv0.astype(f32), seq)
    y = y.swapaxes(0, 1)
    mean = jnp.mean(y, axis=-1, keepdims=True)
    var = jnp.mean(jnp.square(y - mean), axis=-1, keepdims=True)
    yn = ((y - mean) * lax.rsqrt(var + GN_EPS)).reshape(B, T, D_B) * gn_g + gn_b
    bonus = (jnp.sum(r * k * r_k, axis=-1, keepdims=True) * v).reshape(B, T, D_B)
    out = ((yn + bonus) * g).astype(p_rw.dtype)
    return out, p_rw[:, -1], S_fin.astype(p_rw.dtype)


def _merge(o_a, o_b, p, w_branch, w_out):
    B, T = p.shape[:2]
    gates = jax.nn.sigmoid(p[..., OFF_MERGE:].astype(jnp.float32)).reshape(B, T, 2, D_MODEL)
    y_a = o_a @ w_branch[:D_A]
    y_b = o_b @ w_branch[D_A:]
    merged = (gates[:, :, 0] * y_a + gates[:, :, 1] * y_b).astype(o_a.dtype)
    return merged @ w_out


def _grouped_experts(xt, expert, wts, w1, w3, w2):
    n, K = expert.shape
    D = xt.shape[1]
    nk = n * K
    bm = min(MOE_BLOCK, nk)
    n_blocks = (nk + N_EXPERTS * (bm - 1) + bm - 1) // bm
    n_rows = n_blocks * bm
    flat_e = expert.reshape(-1)
    order = jnp.argsort(flat_e)
    sorted_e = flat_e[order]
    counts = jnp.zeros((N_EXPERTS,), jnp.int32).at[flat_e].add(1)
    padded = (counts + bm - 1) // bm * bm
    pad_end = jnp.cumsum(padded)
    pad_start = pad_end - padded
    start = jnp.cumsum(counts) - counts
    dest = pad_start[sorted_e] + jnp.arange(nk, dtype=jnp.int32) - start[sorted_e]
    row_tok = jnp.zeros((n_rows,), jnp.int32).at[dest].set((order // K).astype(jnp.int32))
    row_ok = jnp.zeros((n_rows,), jnp.bool_).at[dest].set(True)
    blk_e = jnp.minimum(jnp.searchsorted(pad_end, jnp.arange(n_blocks) * bm, side='right'), N_EXPERTS - 1)
    xs = jnp.where(row_ok[:, None], xt[row_tok], 0).reshape(n_blocks, bm, D)

    def block(args):
        xb, e = args
        return (jax.nn.silu(xb @ w1[e]) * (xb @ w3[e])) @ w2[e]

    ys = lax.map(block, (xs, blk_e)).reshape(n_rows, D)
    y_pair = jnp.zeros((nk, D), ys.dtype).at[order].set(ys[dest]).reshape(n, K, D)
    return jnp.sum(y_pair * wts[..., None].astype(ys.dtype), axis=1)


def _hier_moe(h, rg_w, rg_b, re_w, re_b, w1, w3, w2):
    B, T, D = h.shape
    xt = h.reshape(B * T, D)
    g_prob = jax.nn.softmax((xt @ rg_w + rg_b).astype(jnp.float32), axis=-1)
    grp = jnp.argmax(g_prob, axis=-1)
    g_w = jnp.take_along_axis(g_prob, grp[:, None], axis=-1)
    e_logit = (xt @ re_w + re_b).astype(jnp.float32).reshape(-1, N_GROUPS, E_PER_GROUP)
    e_logit = jnp.take_along_axis(e_logit, grp[:, None, None], axis=1)[:, 0]
    top_p, top_i = lax.top_k(jax.nn.softmax(e_logit, axis=-1), TOP_K_EXP)
    wts = g_w * top_p / jnp.sum(top_p, axis=-1, keepdims=True)
    expert = (grp[:, None] * E_PER_GROUP + top_i).astype(jnp.int32)
    return _grouped_experts(xt, expert, wts, w1, w3, w2).reshape(B, T, D)


def setup_inputs(seed: int = 0) -> dict:
    key = jax.random.key(seed)
    keys = iter(jax.random.split(key, 64))

    def nrm(shape, scale=1.0):
        return jax.random.normal(next(keys), shape, jnp.float32) * scale

    n_pages = PAST_LEN // PAGE_SIZE
    n_phys = (DEC_BATCH * n_pages * 5) // 4
    w_buf = min(WINDOW, PAST_LEN)
    perm = jax.random.permutation(next(keys), n_phys)
    page_table = perm[:DEC_BATCH * n_pages].reshape(DEC_BATCH, n_pages).astype(jnp.int32)
    kv_pool = (DEPTH, n_phys, PAGE_SIZE, 2, N_KV, HEAD_DIM_A)
    return {
        'x_prompt': nrm((BATCH, SEQ, D_MODEL)),
        'x_sample': nrm((DEC_BATCH, DEC_SEQ, D_MODEL)),
        'cache_cmp': nrm(kv_pool),
        'cache_slc': nrm(kv_pool),
        'cache_win': nrm((DEPTH, DEC_BATCH, w_buf, 2, N_KV, HEAD_DIM_A)),
        'state_shift': nrm((DEPTH, DEC_BATCH, SHIFT_W)),
        'state_wkv': nrm((DEPTH, DEC_BATCH, N_HEADS_B, HEAD_DIM_B, HEAD_DIM_B), 0.5),
        'page_table': page_table,
        'c_prompt': nrm((BATCH, D_MODEL)),
        'c_sample': nrm((DEC_BATCH, D_MODEL)),
        'ln1_g': 1.0 + nrm((DEPTH, D_MODEL), 0.05),
        'ln2_g': 1.0 + nrm((DEPTH, D_MODEL), 0.05),
        'ada_w': nrm((DEPTH, D_MODEL, 6 * D_MODEL), 0.5 * D_MODEL ** -0.5),
        'ada_b': nrm((DEPTH, 6 * D_MODEL), 0.01),
        'w_in': nrm((DEPTH, D_MODEL, IN_W), D_MODEL ** -0.5),
        'cmp_w1': nrm((DEPTH, 2, BLK * HEAD_DIM_A, CMP_HID), (BLK * HEAD_DIM_A) ** -0.5),
        'cmp_b1': nrm((DEPTH, 2, CMP_HID), 0.01),
        'cmp_w2': nrm((DEPTH, 2, CMP_HID, HEAD_DIM_A), CMP_HID ** -0.5),
        'rwkv_mu': jax.random.uniform(next(keys), (DEPTH, SHIFT_W), jnp.float32),
        'rwkv_w0': nrm((DEPTH, D_B), 0.5),
        'rwkv_w2': nrm((DEPTH, LORA_W, D_B), 0.5 * LORA_W ** -0.5),
        'rwkv_a0': nrm((DEPTH, D_B), 0.5),
        'rwkv_a2': nrm((DEPTH, LORA_A, D_B), LORA_A ** -0.5),
        'rwkv_g2': nrm((DEPTH, LORA_G, D_B), LORA_G ** -0.5),
        'rwkv_kk': 0.85 + nrm((DEPTH, D_B), 0.05),
        'rwkv_ka': 1.0 + nrm((DEPTH, D_B), 0.05),
        'rwkv_rk': nrm((DEPTH, N_HEADS_B, HEAD_DIM_B), 0.1),
        'rwkv_gn_g': 1.0 + nrm((DEPTH, D_B), 0.05),
        'rwkv_gn_b': nrm((DEPTH, D_B), 0.01),
        'w_branch': nrm((DEPTH, D_A + D_B, D_MODEL), D_A ** -0.5),
        'w_out': nrm((DEPTH, D_MODEL, D_MODEL), D_MODEL ** -0.5),
        'router_g_w': nrm((DEPTH, D_MODEL, N_GROUPS), D_MODEL ** -0.5),
        'router_g_b': nrm((DEPTH, N_GROUPS), 0.01),
        'router_e_w': nrm((DEPTH, D_MODEL, N_EXPERTS), D_MODEL ** -0.5),
        'router_e_b': nrm((DEPTH, N_EXPERTS), 0.01),
        'exp_w1': nrm((DEPTH, N_EXPERTS, D_MODEL, D_EXPERT), D_MODEL ** -0.5),
        'exp_w3': nrm((DEPTH, N_EXPERTS, D_MODEL, D_EXPERT), D_MODEL ** -0.5),
        'exp_w2': nrm((DEPTH, N_EXPERTS, D_EXPERT, D_MODEL), D_EXPERT ** -0.5),
        'final_g': 1.0 + nrm((D_MODEL,), 0.05),
    }


def reference(x_prompt, x_sample, cache_cmp, cache_slc, cache_win, state_shift, state_wkv, page_table,
              c_prompt, c_sample, ln1_g, ln2_g, ada_w, ada_b, w_in, cmp_w1, cmp_b1, cmp_w2,
              rwkv_mu, rwkv_w0, rwkv_w2, rwkv_a0, rwkv_a2, rwkv_g2, rwkv_kk, rwkv_ka, rwkv_rk,
              rwkv_gn_g, rwkv_gn_b, w_branch, w_out, router_g_w, router_g_b, router_e_w, router_e_b,
              exp_w1, exp_w3, exp_w2, final_g):
    slopes = _alibi_slopes()
    xp, xs = x_prompt, x_sample
    Bp = xp.shape[0]
    cmp_p, slc_p, win_p, shf_p, wkv_p = [], [], [], [], []
    cmp_s, slc_s, win_s, shf_s, wkv_s = [], [], [], [], []
    for l in range(DEPTH):
        mod_p = jnp.split(jax.nn.silu(c_prompt) @ ada_w[l] + ada_b[l], 6, axis=-1)
        mod_s = jnp.split(jax.nn.silu(c_sample) @ ada_w[l] + ada_b[l], 6, axis=-1)
        cw = (cmp_w1[l], cmp_b1[l], cmp_w2[l])
        rw = (rwkv_mu[l], rwkv_w0[l], rwkv_w2[l], rwkv_a0[l], rwkv_a2[l], rwkv_g2[l],
              rwkv_kk[l], rwkv_ka[l], rwkv_rk[l], rwkv_gn_g[l], rwkv_gn_b[l])
        moe = (router_g_w[l], router_g_b[l], router_e_w[l], router_e_b[l], exp_w1[l], exp_w3[l], exp_w2[l])
        pp = _modulate(xp, ln1_g[l], mod_p[0], mod_p[1]) @ w_in[l]
        o_a, r_cmp, r_slc, r_win = _nsa_prompt(pp, *cw, slopes)
        o_b, r_shf, r_wkv = _rwkv7(pp[..., OFF_RWKV:OFF_MERGE], jnp.zeros((Bp, SHIFT_W), xp.dtype),
                                   jnp.zeros((Bp, N_HEADS_B, HEAD_DIM_B, HEAD_DIM_B), xp.dtype), *rw)
        xp = xp + mod_p[2][:, None, :] * _merge(o_a, o_b, pp, w_branch[l], w_out[l])
        cmp_p.append(r_cmp)
        slc_p.append(r_slc)
        win_p.append(r_win)
        shf_p.append(r_shf)
        wkv_p.append(r_wkv)
        ps = _modulate(xs, ln1_g[l], mod_s[0], mod_s[1]) @ w_in[l]
        o_a, r_cmp, r_slc, r_win = _nsa_sample(ps, cache_cmp[l], cache_slc[l], cache_win[l], page_table, *cw, slopes)
        o_b, r_shf, r_wkv = _rwkv7(ps[..., OFF_RWKV:OFF_MERGE], state_shift[l], state_wkv[l], *rw)
        xs = xs + mod_s[2][:, None, :] * _merge(o_a, o_b, ps, w_branch[l], w_out[l])
        cmp_s.append(r_cmp)
        slc_s.append(r_slc)
        win_s.append(r_win)
        shf_s.append(r_shf)
        wkv_s.append(r_wkv)
        xp = xp + mod_p[5][:, None, :] * _hier_moe(_modulate(xp, ln2_g[l], mod_p[3], mod_p[4]), *moe)
        xs = xs + mod_s[5][:, None, :] * _hier_moe(_modulate(xs, ln2_g[l], mod_s[3], mod_s[4]), *moe)
    y_prompt = _rmsnorm(xp, final_g)
    y_sample = _rmsnorm(xs, final_g)
    new_cmp_prompt = jnp.stack(cmp_p)
    new_slc_prompt = jnp.stack(slc_p)
    new_win_prompt = jnp.stack(win_p)
    new_shift_prompt = jnp.stack(shf_p)
    new_wkv_prompt = jnp.stack(wkv_p)
    new_cmp_sample = jnp.stack(cmp_s)
    new_slc_sample = jnp.stack(slc_s)
    new_win_sample = jnp.stack(win_s)
    new_shift_sample = jnp.stack(shf_s)
    new_wkv_sample = jnp.stack(wkv_s)
    return (y_prompt, y_sample, new_cmp_prompt, new_slc_prompt, new_win_prompt, new_shift_prompt, new_wkv_prompt,
            new_cmp_sample, new_slc_sample, new_win_sample, new_shift_sample, new_wkv_sample)
```

```python
import functools

import jax
import jax.numpy as jnp
from jax import lax
from jax.experimental import pallas as pl
from jax.experimental.pallas import tpu as pltpu

F32 = jnp.float32
BF16 = jnp.bfloat16
I32 = jnp.int32

LANE = 128
SUBLANE = 8
VMEM_LIMIT = 56 * 1024 * 1024

D_MODEL = 2048
HD_A = 128
N_HEADS_A = 8
N_KV = 2
GQA_R = 4
BLK = 64
N_SEL = 16
WINDOW = 512
CMP_HID = 256
SCALE_A = HD_A ** -0.5
D_A = 1024
D_B = 1024
HD_B = 64
N_HEADS_B = 16
LORA_W, LORA_A, LORA_G = 64, 64, 32
GN_EPS = HD_B * 1e-5
N_GROUPS = 4
E_PER_GROUP = 8
N_EXPERTS = 32
D_EXPERT = 512
MOE_BLOCK = 128
RMS_EPS = 1e-6
PAGE = 128

Q_W = N_HEADS_A * HD_A
KV_W = N_KV * HD_A
OFF_KV = Q_W
OFF_GATE_A = OFF_KV + 6 * KV_W
OFF_RWKV = OFF_GATE_A + 3 * N_HEADS_A
SHIFT_W = 3 * D_B + LORA_W + LORA_A + LORA_G
OFF_MERGE = OFF_RWKV + SHIFT_W

A_KV = Q_W
A_GATE = A_KV + 6 * KV_W
A_RWKV = A_GATE + N_KV * LANE
RW_W = 3 * D_B + 2 * LANE
A_MERGE = A_RWKV + RW_W
IN_AL = A_MERGE + 2 * D_MODEL
NEG = -1e30


def _cparams(sem):
    return pltpu.CompilerParams(dimension_semantics=sem, vmem_limit_bytes=VMEM_LIMIT)


def _dot(a, b, dims=(((1,), (0,)), ((), ()))):
    return lax.dot_general(a.astype(BF16), b.astype(BF16), dims, preferred_element_type=F32)


def _dot_nt(a, b):
    return _dot(a, b, (((1,), (1,)), ((), ())))


def _dot_tn(a, b):
    return _dot(a, b, (((0,), (0,)), ((), ())))


def _split3(x):
    h = x.astype(BF16)
    r1 = x - h.astype(F32)
    m = r1.astype(BF16)
    lo = (r1 - m.astype(F32)).astype(BF16)
    return h, m, lo


def _dot3(a, b, dims=(((1,), (0,)), ((), ()))):
    ah, am, al = _split3(a)
    bh, bm, bl = _split3(b)
    d = lambda x, y: lax.dot_general(x, y, dims, preferred_element_type=F32)
    return (d(ah, bh) + (d(ah, bm) + d(am, bh))) + ((d(am, bm) + d(ah, bl)) + d(al, bh))


def _sigmoid(x):
    return 1.0 / (1.0 + jnp.exp(-x))


def _softplus(x):
    return jnp.maximum(x, 0.0) + jnp.log(1.0 + jnp.exp(-jnp.abs(x)))


def _adaln_kernel(c_ref, w_ref, b_ref, o_ref):
    c = c_ref[...]
    h = c * _sigmoid(c)
    o_ref[...] = _dot(h, w_ref[...]) + b_ref[...]


def adaln(c16, ada_w, ada_b):
    depth, d, n = ada_w.shape
    tn = 1024
    return pl.pallas_call(
        _adaln_kernel,
        out_shape=jax.ShapeDtypeStruct((depth, c16.shape[0], n), F32),
        grid=(depth, n // tn),
        in_specs=[pl.BlockSpec(c16.shape, lambda l, j: (0, 0)),
                  pl.BlockSpec((None, d, tn), lambda l, j: (l, 0, j)),
                  pl.BlockSpec((None, 1, tn), lambda l, j: (l, 0, j))],
        out_specs=pl.BlockSpec((None, c16.shape[0], tn), lambda l, j: (l, 0, j)),
        compiler_params=_cparams(("parallel", "parallel")),
        name="adaln",
    )(c16, ada_w, ada_b.reshape(depth, 1, n))


def _norm_kernel(x_ref, g_ref, sh_ref, sc_ref, o_ref):
    x = x_ref[...]
    y = x * lax.rsqrt(jnp.mean(x * x, axis=-1, keepdims=True) + RMS_EPS)
    y = y * g_ref[...]
    o_ref[...] = (y * (1.0 + sc_ref[...]) + sh_ref[...]).astype(o_ref.dtype)


def norm_mod(x, g, shift, scale, out_dtype):
    b, t, d = x.shape
    tr = min(t, 256)
    return pl.pallas_call(
        _norm_kernel,
        out_shape=jax.ShapeDtypeStruct((b, t, d), out_dtype),
        grid=(b, t // tr),
        in_specs=[pl.BlockSpec((None, tr, d), lambda i, j: (i, j, 0)),
                  pl.BlockSpec((1, d), lambda i, j: (0, 0)),
                  pl.BlockSpec((None, 1, d), lambda i, j: (i, 0, 0)),
                  pl.BlockSpec((None, 1, d), lambda i, j: (i, 0, 0))],
        out_specs=pl.BlockSpec((None, tr, d), lambda i, j: (i, j, 0)),
        compiler_params=_cparams(("parallel", "parallel")),
        name="norm_mod",
    )(x, g.reshape(1, d), shift.reshape(b, 1, d), scale.reshape(b, 1, d))


def _mm_kernel(a_ref, w_ref, o_ref, *maybe_bf16):
    acc = _dot(a_ref[...], w_ref[...])
    o_ref[...] = acc
    if maybe_bf16:
        maybe_bf16[0][...] = acc.astype(BF16)


def _row_tile(m, cap):
    return m if m <= cap else cap


def matmul(a, w, col0, ncols, also_bf16=False):
    m, k = a.shape
    tm = _row_tile(m, 1024)
    tn = 512
    assert m % tm == 0 and ncols % tn == 0 and col0 % tn == 0
    c0 = col0 // tn
    out_shape = [jax.ShapeDtypeStruct((m, ncols), F32)]
    out_specs = [pl.BlockSpec((tm, tn), lambda i, j: (i, j))]
    if also_bf16:
        out_shape.append(jax.ShapeDtypeStruct((m, ncols), BF16))
        out_specs.append(pl.BlockSpec((tm, tn), lambda i, j: (i, j)))
    res = pl.pallas_call(
        _mm_kernel,
        out_shape=out_shape,
        grid=(m // tm, ncols // tn),
        in_specs=[pl.BlockSpec((tm, k), lambda i, j: (i, 0)),
                  pl.BlockSpec((k, tn), lambda i, j: (0, j + c0))],
        out_specs=out_specs,
        compiler_params=_cparams(("parallel", "parallel")),
        name="proj_in",
    )(a, w)
    return res if also_bf16 else res[0]


def _branch_kernel(oa_ref, ob_ref, wa_ref, wb_ref, ga_ref, gb_ref, o_ref):
    ya = _dot(oa_ref[...], wa_ref[...])
    yb = _dot(ob_ref[...], wb_ref[...])
    o_ref[...] = (_sigmoid(ga_ref[...]) * ya + _sigmoid(gb_ref[...]) * yb).astype(o_ref.dtype)


def branch_merge(o_a, o_b, w_branch_l, pp):
    m = o_a.shape[0]
    tm = _row_tile(m, 1024)
    tn = 512
    gcol = A_MERGE // tn
    return pl.pallas_call(
        _branch_kernel,
        out_shape=jax.ShapeDtypeStruct((m, D_MODEL), BF16),
        grid=(m // tm, D_MODEL // tn),
        in_specs=[pl.BlockSpec((tm, D_A), lambda i, j: (i, 0)),
                  pl.BlockSpec((tm, D_B), lambda i, j: (i, 0)),
                  pl.BlockSpec((D_A, tn), lambda i, j: (0, j)),
                  pl.BlockSpec((D_B, tn), lambda i, j: (D_A // D_B, j)),
                  pl.BlockSpec((tm, tn), lambda i, j: (i, gcol + j)),
                  pl.BlockSpec((tm, tn), lambda i, j: (i, gcol + D_MODEL // tn + j))],
        out_specs=pl.BlockSpec((tm, tn), lambda i, j: (i, j)),
        compiler_params=_cparams(("parallel", "parallel")),
        name="branch_merge",
    )(o_a, o_b, w_branch_l, w_branch_l, pp, pp)


def _resid_kernel(m_ref, w_ref, x_ref, g_ref, o_ref):
    o_ref[...] = x_ref[...] + g_ref[...] * _dot(m_ref[...], w_ref[...])


def out_proj_residual(merged, w_out_l, x, gate):
    b, t, d = x.shape
    tm = _row_tile(t, 1024)
    tn = 512
    nt = t // tm
    return pl.pallas_call(
        _resid_kernel,
        out_shape=jax.ShapeDtypeStruct((b, t, d), F32),
        grid=(b, nt, d // tn),
        in_specs=[pl.BlockSpec((None, tm, d), lambda i, r, j: (i, r, 0)),
                  pl.BlockSpec((d, tn), lambda i, r, j: (0, j)),
                  pl.BlockSpec((None, tm, tn), lambda i, r, j: (i, r, j)),
                  pl.BlockSpec((None, 1, tn), lambda i, r, j: (i, 0, j))],
        out_specs=pl.BlockSpec((None, tm, tn), lambda i, r, j: (i, r, j)),
        compiler_params=_cparams(("parallel", "parallel", "parallel")),
        name="out_proj",
    )(merged, w_out_l, x, gate.reshape(b, 1, d))


T_PER_STEP = 8


def _gelu_tanh(x):
    return 0.5 * x * (1.0 + jnp.tanh(0.7978845608028654 * (x + 0.044715 * x * x * x)))


def _compress_kernel(x_ref, w1_ref, b1_ref, w2_ref, o_ref, acc_ref):
    tc = pl.program_id(3)
    nblk = o_ref.shape[0]

    @pl.when(tc == 0)
    def _():
        acc_ref[...] = jnp.zeros_like(acc_ref)

    acc = acc_ref[...]
    for tl in range(T_PER_STEP):
        t = tc * T_PER_STEP + tl
        xt = x_ref[pl.ds(t, nblk, stride=BLK), :]
        acc = acc + _dot(xt, w1_ref[tl * HD_A:(tl + 1) * HD_A, :])
    acc_ref[...] = acc

    @pl.when(tc == pl.num_programs(3) - 1)
    def _():
        h = _gelu_tanh(acc + b1_ref[...])
        o_ref[...] = _dot(h, w2_ref[...])


def compress(rows, col0, n_slabs, w1b, b1, w2):
    r = rows.shape[0] // n_slabs
    nblk = r // BLK
    cb = col0 // HD_A
    return pl.pallas_call(
        _compress_kernel,
        out_shape=jax.ShapeDtypeStruct((2, N_KV, n_slabs * nblk, HD_A), F32),
        grid=(2, N_KV, n_slabs, BLK // T_PER_STEP),
        in_specs=[pl.BlockSpec((r, HD_A), lambda kv, g, s, tc: (s, cb + kv * N_KV + g)),
                  pl.BlockSpec((None, T_PER_STEP * HD_A, CMP_HID), lambda kv, g, s, tc: (kv, tc, 0)),
                  pl.BlockSpec((None, 1, CMP_HID), lambda kv, g, s, tc: (kv, 0, 0)),
                  pl.BlockSpec((None, CMP_HID, HD_A), lambda kv, g, s, tc: (kv, 0, 0))],
        out_specs=pl.BlockSpec((None, None, nblk, HD_A), lambda kv, g, s, tc: (kv, g, s, 0)),
        scratch_shapes=[pltpu.VMEM((nblk, CMP_HID), F32)],
        compiler_params=_cparams(("parallel", "parallel", "parallel", "arbitrary")),
        name="compress",
    )(rows, w1b, b1.reshape(2, 1, CMP_HID), w2)


QB = 128
KC = 512


def _softmax_cols(s, valid):
    sm = jnp.where(valid, s, NEG)
    m = jnp.max(sm, axis=0, keepdims=True)
    e = jnp.where(valid, jnp.exp(sm - m), 0.0)
    return e, jnp.sum(e, axis=0, keepdims=True)


def _safe(den):
    return jnp.where(den > 0, den, 1.0)


def _select_mask_t(imp, jblk, cur):
    nb = imp.shape[0]
    score = jnp.where(jblk < cur, imp, jnp.where(jblk == cur, GQA_R + 1.0, -1.0))
    rank = jnp.zeros(imp.shape, I32)
    for i in range(nb):
        row = score[i:i + 1, :]
        beats = (row > score) | ((row == score) & (jblk > i))
        rank = rank + beats.astype(I32)
    return (rank < N_SEL) & (jblk <= cur)


def _nsa_prompt_kernel(slopes_ref, q_ref, kc_ref, vc_ref, ks_ref, vs_ref, kw_ref, vw_ref, gate_ref,
                       o_ref, mask_ref):
    g = pl.program_id(1)
    i = pl.program_id(2)
    t_len = ks_ref.shape[0]
    nb = t_len // BLK
    q0 = i * QB
    span = WINDOW + QB

    jblk = lax.broadcasted_iota(I32, (nb, QB), 0)
    qpos_l = q0 + lax.broadcasted_iota(I32, (nb, QB), 1)
    dist_c = qpos_l - ((jblk + 1) * BLK - 1)
    valid_c = dist_c >= 0
    dist_cf = dist_c.astype(F32)
    kc = kc_ref[...]
    vc = vc_ref[...]

    qs = [q_ref[:, r * HD_A:(r + 1) * HD_A] for r in range(GQA_R)]
    slopes = [slopes_ref[g * GQA_R + r] for r in range(GQA_R)]

    imp = jnp.zeros((nb, QB), F32)
    o_cmp = []
    for r in range(GQA_R):
        s = _dot_nt(kc, qs[r]) * SCALE_A - slopes[r] * dist_cf
        e, den = _softmax_cols(s, valid_c)
        p = e / _safe(den)
        imp = imp + p
        o_cmp.append(_dot_tn(p, vc))

    sel = _select_mask_t(imp, jblk, qpos_l // BLK)
    expand = (lax.broadcasted_iota(I32, (nb, t_len), 1) // BLK
              == lax.broadcasted_iota(I32, (nb, t_len), 0))
    key_sel = _dot_tn(sel.astype(F32), expand.astype(F32))
    for cc in range(t_len // KC):
        mask_ref[cc] = key_sel[:, cc * KC:(cc + 1) * KC]

    n_chunks = (q0 + QB + KC - 1) // KC
    row_q = q0 + lax.broadcasted_iota(I32, (QB, KC), 0)
    col_k = lax.broadcasted_iota(I32, (QB, KC), 1)
    o_sel = []
    for r in range(GQA_R):
        def body(c, carry, r=r):
            m, l, acc = carry
            k0 = pl.multiple_of(c * KC, KC)
            kk = ks_ref[pl.ds(k0, KC), :]
            vv = vs_ref[pl.ds(k0, KC), :]
            dist = row_q - (k0 + col_k)
            valid = (mask_ref[c] > 0.5) & (dist >= 0)
            s = _dot_nt(qs[r], kk) * SCALE_A - slopes[r] * dist.astype(F32)
            sm = jnp.where(valid, s, NEG)
            m_new = jnp.maximum(m, jnp.max(sm, axis=-1, keepdims=True))
            alpha = jnp.exp(m - m_new)
            e = jnp.where(valid, jnp.exp(sm - m_new), 0.0)
            l = alpha * l + jnp.sum(e, axis=-1, keepdims=True)
            acc = alpha * acc + _dot(e, vv)
            return m_new, l, acc

        m, l, acc = lax.fori_loop(
            0, n_chunks, body,
            (jnp.full((QB, 1), NEG, F32), jnp.zeros((QB, 1), F32), jnp.zeros((QB, HD_A), F32)))
        o_sel.append(acc / _safe(l))

    start = pl.multiple_of(jnp.maximum(i - WINDOW // QB, 0) * QB, QB)
    kw = kw_ref[pl.ds(start, span), :]
    vw = vw_ref[pl.ds(start, span), :]
    dist_w = (q0 + lax.broadcasted_iota(I32, (QB, span), 0)) - (start + lax.broadcasted_iota(I32, (QB, span), 1))
    valid_w = (dist_w >= 0) & (dist_w < WINDOW)
    dist_wf = dist_w.astype(F32)
    gates = _sigmoid(gate_ref[...])
    for r in range(GQA_R):
        s = _dot_nt(qs[r], kw) * SCALE_A - slopes[r] * dist_wf
        sm = jnp.where(valid_w, s, NEG)
        m = jnp.max(sm, axis=-1, keepdims=True)
        e = jnp.where(valid_w, jnp.exp(sm - m), 0.0)
        den = jnp.sum(e, axis=-1, keepdims=True)
        o_win = _dot(e, vw) / _safe(den)
        o = (gates[:, 3 * r:3 * r + 1] * o_cmp[r] + gates[:, 3 * r + 1:3 * r + 2] * o_sel[r]
             + gates[:, 3 * r + 2:3 * r + 3] * o_win)
        o_ref[:, r * HD_A:(r + 1) * HD_A] = o.astype(o_ref.dtype)


def nsa_prompt(slopes, ppb, pp, cmp_kv, b, t):
    nq = t // QB
    nb = t // BLK
    kvb = A_KV // HD_A

    def kv_spec(branch, kv):
        return pl.BlockSpec((t, HD_A), lambda bi, g, i, s: (bi, kvb + branch * 4 + kv * 2 + g))

    grid_spec = pltpu.PrefetchScalarGridSpec(
        num_scalar_prefetch=1,
        grid=(b, N_KV, nq),
        in_specs=[pl.BlockSpec((QB, GQA_R * HD_A), lambda bi, g, i, s: (bi * nq + i, g)),
                  pl.BlockSpec((None, None, nb, HD_A), lambda bi, g, i, s: (0, g, bi, 0)),
                  pl.BlockSpec((None, None, nb, HD_A), lambda bi, g, i, s: (1, g, bi, 0)),
                  kv_spec(1, 0), kv_spec(1, 1), kv_spec(2, 0), kv_spec(2, 1),
                  pl.BlockSpec((QB, LANE), lambda bi, g, i, s: (bi * nq + i, A_GATE // LANE + g))],
        out_specs=pl.BlockSpec((QB, GQA_R * HD_A), lambda bi, g, i, s: (bi * nq + i, g)),
        scratch_shapes=[pltpu.VMEM((t // KC, QB, KC), F32)],
    )
    return pl.pallas_call(
        _nsa_prompt_kernel,
        out_shape=jax.ShapeDtypeStruct((b * t, D_A), BF16),
        grid_spec=grid_spec,
        compiler_params=_cparams(("parallel", "parallel", "arbitrary")),
        name="nsa_prompt",
    )(slopes, ppb, cmp_kv, cmp_kv, ppb, ppb, ppb, ppb, pp)


def _page_gather_kernel(pt_ref, src_ref, dst_ref):
    dst_ref[...] = src_ref[...]


def page_gather(page_table, pool):
    bd, n_pages = page_table.shape
    w = pool.shape[-1]
    grid_spec = pltpu.PrefetchScalarGridSpec(
        num_scalar_prefetch=1,
        grid=(bd, n_pages),
        in_specs=[pl.BlockSpec((None, PAGE, w), lambda bi, p, pt: (pt[bi * n_pages + p], 0, 0))],
        out_specs=pl.BlockSpec((PAGE, w), lambda bi, p, pt: (bi * n_pages + p, 0)),
    )
    return pl.pallas_call(
        _page_gather_kernel,
        out_shape=jax.ShapeDtypeStruct((bd * n_pages * PAGE, w), pool.dtype),
        grid_spec=grid_spec,
        compiler_params=_cparams(("parallel", "arbitrary")),
        name="page_gather",
    )(page_table.reshape(-1), pool)


SEL_PAD = 128


def _heads_on_sublanes(q_ref):
    q = q_ref[...].astype(F32)
    row = lax.broadcasted_iota(I32, (SUBLANE, HD_A), 0)
    q8 = jnp.zeros((SUBLANE, HD_A), F32)
    for r in range(GQA_R):
        q8 = jnp.where(row == r, jnp.broadcast_to(q[:, r * HD_A:(r + 1) * HD_A], (SUBLANE, HD_A)), q8)
    return q8


def _slopes_on_sublanes(slopes_ref, g):
    row = lax.broadcasted_iota(I32, (SUBLANE, 1), 0)
    slope = jnp.zeros((SUBLANE, 1), F32)
    for r in range(GQA_R):
        slope = jnp.where(row == r, slopes_ref[g * GQA_R + r], slope)
    return slope


def _softmax_rows(s, valid):
    sm = jnp.where(valid, s, NEG)
    m = jnp.max(sm, axis=-1, keepdims=True)
    e = jnp.where(valid, jnp.exp(sm - m), 0.0)
    return e, jnp.sum(e, axis=-1, keepdims=True)


def _nsa_sample_cmp_kernel(slopes_ref, q_ref, kc_ref, vc_ref, o_ref, idx_ref, *, past, nb_real):
    g = pl.program_id(1)
    nbp = kc_ref.shape[0]
    kc = kc_ref[...]
    vc = vc_ref[...]
    q8 = _heads_on_sublanes(q_ref)
    slope = _slopes_on_sublanes(slopes_ref, g)
    head = lax.broadcasted_iota(I32, (SUBLANE, 1), 0) < GQA_R
    jrow = lax.broadcasted_iota(I32, (1, nbp), 1)
    dist = past - ((jrow + 1) * BLK - 1)
    valid = (dist >= 0) & (jrow < nb_real)
    s = _dot_nt(q8, kc) * SCALE_A - slope * dist.astype(F32)
    e, den = _softmax_rows(s, valid)
    p = e / _safe(den)
    o = _dot(p, vc)
    for r in range(GQA_R):
        o_ref[:, r * HD_A:(r + 1) * HD_A] = o[r:r + 1, :]

    imp = jnp.sum(jnp.where(head, p, 0.0), axis=0, keepdims=True)
    cur = past // BLK
    score_r = jnp.where(jrow < cur, imp, jnp.where(jrow == cur, GQA_R + 1.0, -1.0))
    score_r = jnp.where(jrow < nb_real, score_r, -2.0)
    ii = lax.broadcasted_iota(I32, (nbp, nbp), 0)
    jj = lax.broadcasted_iota(I32, (nbp, nbp), 1)
    sc_b = jnp.broadcast_to(score_r, (nbp, nbp))
    score_c = jnp.sum(jnp.where(ii == jj, sc_b, 0.0), axis=1, keepdims=True)
    beats = (score_c > sc_b) | ((score_c == sc_b) & (ii < jj))
    rank_r = jnp.sum(beats.astype(F32), axis=0, keepdims=True)
    nn = lax.broadcasted_iota(I32, (SEL_PAD, nbp), 0).astype(F32)
    jn = lax.broadcasted_iota(I32, (SEL_PAD, nbp), 1).astype(F32)
    hit = jnp.broadcast_to(rank_r, (SEL_PAD, nbp)) == nn
    idx_c = jnp.sum(jnp.where(hit, jn, 0.0), axis=1, keepdims=True)
    idx_ref[...] = jnp.broadcast_to(idx_c, (SEL_PAD, LANE)).astype(I32)


def nsa_sample_cmp(slopes, ppb3, cmp_kv_s, past, nb_real):
    bd = ppb3.shape[0]
    nbp = cmp_kv_s.shape[3]
    grid_spec = pltpu.PrefetchScalarGridSpec(
        num_scalar_prefetch=1,
        grid=(bd, N_KV),
        in_specs=[pl.BlockSpec((None, 1, GQA_R * HD_A), lambda bi, g, s: (bi, 0, g)),
                  pl.BlockSpec((None, None, None, nbp, HD_A), lambda bi, g, s: (0, g, bi, 0, 0)),
                  pl.BlockSpec((None, None, None, nbp, HD_A), lambda bi, g, s: (1, g, bi, 0, 0))],
        out_specs=[pl.BlockSpec((None, 1, GQA_R * HD_A), lambda bi, g, s: (bi, 0, g)),
                   pl.BlockSpec((None, None, SEL_PAD, LANE), lambda bi, g, s: (bi, g, 0, 0))],
    )
    return pl.pallas_call(
        functools.partial(_nsa_sample_cmp_kernel, past=past, nb_real=nb_real),
        out_shape=[jax.ShapeDtypeStruct((bd, 1, D_A), F32),
                   jax.ShapeDtypeStruct((bd, N_KV, SEL_PAD, LANE), I32)],
        grid_spec=grid_spec,
        compiler_params=_cparams(("parallel", "parallel")),
        name="nsa_sample_cmp",
    )(slopes, ppb3, cmp_kv_s, cmp_kv_s)


W_PAD = LANE


def _nsa_sample_sel_kernel(idx_ref, pt_ref, slopes_ref, q_ref, kp_ref, vp_ref, kn_ref, vn_ref,
                           kwb_ref, vwb_ref, kwn_ref, vwn_ref, gate_ref, oc_ref, o_ref,
                           ksel_ref, vsel_ref, kwin_ref, vwin_ref, *, past):
    bi = pl.program_id(0)
    g = pl.program_id(1)
    n = pl.program_id(2)
    nb_past = past // BLK
    base = (bi * N_KV + g) * N_SEL
    idx_n = idx_ref[base + n]
    r0 = pl.multiple_of(n * BLK, BLK)

    @pl.when(idx_n < nb_past)
    def _():
        ksel_ref[pl.ds(r0, BLK), :] = kp_ref[...]
        vsel_ref[pl.ds(r0, BLK), :] = vp_ref[...]

    @pl.when(idx_n >= nb_past)
    def _():
        first = lax.broadcasted_iota(I32, (BLK, HD_A), 0) == 0
        ksel_ref[pl.ds(r0, BLK), :] = jnp.where(first, kn_ref[...], 0.0)
        vsel_ref[pl.ds(r0, BLK), :] = jnp.where(first, vn_ref[...], 0.0)

    @pl.when(n == N_SEL - 1)
    def _():
        q8 = _heads_on_sublanes(q_ref)
        slope = _slopes_on_sublanes(slopes_ref, g)
        cur = past // BLK
        lane = lax.broadcasted_iota(I32, (1, N_SEL * BLK), 1)
        slot = lane // BLK
        idx_row = jnp.zeros((1, N_SEL * BLK), I32)
        for m in range(N_SEL):
            idx_row = jnp.where(slot == m, idx_ref[base + m], idx_row)
        dist = past - (idx_row * BLK + lane % BLK)
        valid = (idx_row <= cur) & (dist >= 0)
        s = _dot_nt(q8, ksel_ref[...]) * SCALE_A - slope * dist.astype(F32)
        e, den = _softmax_rows(s, valid)
        o_sel = _dot(e / _safe(den), vsel_ref[...])
        wb = kwb_ref.shape[0]
        kwin_ref[0:wb, :] = kwb_ref[...]
        vwin_ref[0:wb, :] = vwb_ref[...]
        first = lax.broadcasted_iota(I32, (W_PAD, HD_A), 0) == 0
        kwin_ref[wb:wb + W_PAD, :] = jnp.where(first, kwn_ref[...], 0.0)
        vwin_ref[wb:wb + W_PAD, :] = jnp.where(first, vwn_ref[...], 0.0)
        lane_w = lax.broadcasted_iota(I32, (1, wb + W_PAD), 1)
        dist_w = wb - lane_w
        valid_w = (dist_w >= 0) & (dist_w < WINDOW)
        s = _dot_nt(q8, kwin_ref[...]) * SCALE_A - slope * dist_w.astype(F32)
        e, den = _softmax_rows(s, valid_w)
        o_win = _dot(e / _safe(den), vwin_ref[...])
        gates = _sigmoid(gate_ref[...])
        for r in range(GQA_R):
            o = (gates[:, 3 * r:3 * r + 1] * oc_ref[:, r * HD_A:(r + 1) * HD_A]
                 + gates[:, 3 * r + 1:3 * r + 2] * o_sel[r:r + 1, :]
                 + gates[:, 3 * r + 2:3 * r + 3] * o_win[r:r + 1, :])
            o_ref[:, r * HD_A:(r + 1) * HD_A] = o.astype(o_ref.dtype)


def nsa_sample_sel(idx_flat, page_table, slopes, ppb3, pp3, pool_slc, win_buf, o_cmp, past):
    bd = ppb3.shape[0]
    n_pages = page_table.shape[1]
    wb = win_buf.shape[1]
    bpp = PAGE // BLK
    nb_past = past // BLK
    kvb = A_KV // HD_A

    def pool_spec(kv):
        def imap(bi, g, n, idx, pt, s):
            ip = jnp.clip(idx[(bi * N_KV + g) * N_SEL + n], 0, nb_past - 1)
            phys = pt[bi * n_pages + ip // bpp]
            return (phys * bpp + ip % bpp, 0, kv * N_KV + g)
        return pl.BlockSpec((None, BLK, HD_A), imap)

    def new_spec(branch, kv):
        return pl.BlockSpec((None, 1, HD_A), lambda bi, g, n, idx, pt, s: (bi, 0, kvb + branch * 4 + kv * 2 + g))

    def win_spec(kv):
        return pl.BlockSpec((None, wb, HD_A), lambda bi, g, n, idx, pt, s: (bi, 0, kv * N_KV + g))

    grid_spec = pltpu.PrefetchScalarGridSpec(
        num_scalar_prefetch=3,
        grid=(bd, N_KV, N_SEL),
        in_specs=[pl.BlockSpec((None, 1, GQA_R * HD_A), lambda bi, g, n, idx, pt, s: (bi, 0, g)),
                  pool_spec(0), pool_spec(1), new_spec(1, 0), new_spec(1, 1),
                  win_spec(0), win_spec(1), new_spec(2, 0), new_spec(2, 1),
                  pl.BlockSpec((None, 1, LANE), lambda bi, g, n, idx, pt, s: (bi, 0, A_GATE // LANE + g)),
                  pl.BlockSpec((None, 1, GQA_R * HD_A), lambda bi, g, n, idx, pt, s: (bi, 0, g))],
        out_specs=pl.BlockSpec((None, 1, GQA_R * HD_A), lambda bi, g, n, idx, pt, s: (bi, 0, g)),
        scratch_shapes=[pltpu.VMEM((N_SEL * BLK, HD_A), F32), pltpu.VMEM((N_SEL * BLK, HD_A), F32),
                        pltpu.VMEM((wb + W_PAD, HD_A), F32), pltpu.VMEM((wb + W_PAD, HD_A), F32)],
    )
    return pl.pallas_call(
        functools.partial(_nsa_sample_sel_kernel, past=past),
        out_shape=jax.ShapeDtypeStruct((bd, 1, D_A), BF16),
        grid_spec=grid_spec,
        compiler_params=_cparams(("parallel", "parallel", "arbitrary")),
        name="nsa_sample_sel",
    )(idx_flat, page_table.reshape(-1), slopes, ppb3, pool_slc, pool_slc, pp3, pp3,
      win_buf, win_buf, pp3, pp3, pp3, o_cmp)


HP = LANE // HD_B


def _cumsum_rows(x):
    c = x.shape[0]
    row = lax.broadcasted_iota(I32, x.shape, 0)
    sh = 1
    while sh < c:
        x = x + jnp.where(row >= sh, pltpu.roll(x, sh, 0), 0.0)
        sh *= 2
    return x


def _rwkv_kernel(pr_ref, pk_ref, pv_ref, pl_ref, sr_ref, sk_ref, sv_ref, sl_ref, s0_ref,
                 mur_ref, muk_ref, muv_ref, mul_ref, w0_ref, w2_ref, a0_ref, a2_ref, g2_ref,
                 kkp_ref, kap_ref, rkp_ref, gng_ref, gnb_ref,
                 o_ref, sT_ref, prev_ref, prevl_ref, st_ref, *, n_valid):
    ci = pl.program_id(2)
    c = pr_ref.shape[0]

    @pl.when(ci == 0)
    def _():
        prev_ref[0:1, :] = sr_ref[...]
        prev_ref[1:2, :] = sk_ref[...]
        prev_ref[2:3, :] = sv_ref[...]
        prevl_ref[...] = sl_ref[...]
        st_ref[...] = s0_ref[...]

    row = lax.broadcasted_iota(I32, (c, 1), 0)

    def mix(p, prev, mu):
        if c == 1:
            shifted = prev
        else:
            shifted = jnp.where(row == 0, prev, pltpu.roll(p, 1, 0))
        return p + mu * (shifted - p)

    pr, pk, pv, plo = pr_ref[...], pk_ref[...], pv_ref[...], pl_ref[...]
    xr = mix(pr, prev_ref[0:1, :], mur_ref[...])
    xk = mix(pk, prev_ref[1:2, :], muk_ref[...])
    xv = mix(pv, prev_ref[2:3, :], muv_ref[...])
    xl = mix(plo, prevl_ref[...], mul_ref[...])
    last = n_valid - 1
    prev_ref[0:1, :] = pr[last:last + 1, :]
    prev_ref[1:2, :] = pk[last:last + 1, :]
    prev_ref[2:3, :] = pv[last:last + 1, :]
    prevl_ref[...] = plo[last:last + 1, :]

    dw = xl[:, 0:LORA_W]
    da = xl[:, LORA_W:LORA_W + LORA_A]
    dg = xl[:, LORA_W + LORA_A:LORA_W + LORA_A + LORA_G]
    wlog = -_softplus(-(w0_ref[...] + _dot(jnp.tanh(dw), w2_ref[...]))) - 0.5
    logdec = -jnp.exp(wlog)
    a = _sigmoid(a0_ref[...] + _dot(da, a2_ref[...]))
    gate = _dot(_sigmoid(dg), g2_ref[...])
    kkv = xk * kkp_ref[...]
    kmod = xk * (1.0 + (a - 1.0) * kap_ref[...])
    if n_valid < c:
        live = row < n_valid
        logdec = jnp.where(live, logdec, 0.0)
        kmod = jnp.where(live, kmod, 0.0)
        a = jnp.where(live, a, 0.0)
        xv = jnp.where(live, xv, 0.0)
    cum = _cumsum_rows(logdec)
    cum_ex = cum - logdec
    cum_end = cum[c - 1:c, :]

    ti = lax.broadcasted_iota(I32, (c, c), 0)
    si = lax.broadcasted_iota(I32, (c, c), 1)
    lower_strict = ti > si
    lower_incl = ti >= si
    eye = (ti == si).astype(F32)
    rk_all = rkp_ref[...]
    gng, gnb = gng_ref[...], gnb_ref[...]

    for h in range(HP):
        sl = slice(h * HD_B, (h + 1) * HD_B)
        r_h, k_h, v_h = xr[:, sl], kmod[:, sl], xv[:, sl]
        kk_h = kkv[:, sl]
        kk_h = kk_h / jnp.maximum(jnp.sqrt(jnp.sum(kk_h * kk_h, axis=-1, keepdims=True)), 1e-12)
        b_h = kk_h * a[:, sl]
        cu, cx, ce = cum[:, sl], cum_ex[:, sl], cum_end[:, sl]
        s0 = st_ref[h]

        kap_t = kk_h * jnp.exp(cx)
        r_t = r_h * jnp.exp(cu)
        inv = jnp.exp(-cu)
        k_t = k_h * inv
        b_t = b_h * inv
        lhs = jnp.concatenate([kap_t, r_t], axis=0)
        rhs = jnp.concatenate([k_t, b_t], axis=0)
        amat = _dot3(lhs, rhs, (((1,), (1,)), ((), ())))
        a_kk = jnp.where(lower_strict, amat[0:c, 0:c], 0.0)
        a_kb = jnp.where(lower_strict, amat[0:c, c:2 * c], 0.0)
        a_rk = jnp.where(lower_incl, amat[c:2 * c, 0:c], 0.0)
        a_rb = jnp.where(lower_incl, amat[c:2 * c, c:2 * c], 0.0)

        tinv = eye - a_kb
        if c > 2:
            pw = _dot3(a_kb, a_kb)
            span_done = 2
            while True:
                tinv = tinv + _dot3(tinv, pw)
                span_done *= 2
                if span_done >= c:
                    break
                pw = _dot3(pw, pw)

        from_state = _dot3(jnp.concatenate([kap_t, r_t], axis=0), s0, (((1,), (1,)), ((), ())))
        w_in = _dot3(a_kk, v_h) + from_state[0:c]
        u = _dot3(tinv, w_in)
        y = _dot3(a_rk, v_h) - _dot3(a_rb, u) + from_state[c:2 * c]
        k_end = k_h * jnp.exp(ce - cu)
        b_end = b_h * jnp.exp(ce - cu)
        s_new = s0 * jnp.exp(ce) + _dot3(v_h, k_end, (((0,), (0,)), ((), ()))) \
            - _dot3(u, b_end, (((0,), (0,)), ((), ())))
        st_ref[h] = s_new
        sT_ref[h] = s_new

        mean = jnp.mean(y, axis=-1, keepdims=True)
        var = jnp.mean(jnp.square(y - mean), axis=-1, keepdims=True)
        yn = (y - mean) * lax.rsqrt(var + GN_EPS) * gng[:, sl] + gnb[:, sl]
        bonus = jnp.sum(r_h * k_h * rk_all[:, sl], axis=-1, keepdims=True) * v_h
        o_ref[:, sl] = ((yn + bonus) * gate[:, sl]).astype(o_ref.dtype)


def rwkv7(pp, shift0, wkv0, prm, bsz, t, chunk, n_valid):
    nc = t // chunk
    nhp = N_HEADS_B // HP
    rb = A_RWKV // LANE
    lo_blk = (A_RWKV + 3 * D_B) // (2 * LANE)

    def rows(off):
        return pl.BlockSpec((chunk, LANE), lambda b, hp, ci: (b * nc + ci, rb + off * (D_B // LANE) + hp))

    def srow(off):
        return pl.BlockSpec((None, 1, LANE), lambda b, hp, ci: (b, 0, off * (D_B // LANE) + hp))

    def prow(off):
        return pl.BlockSpec((1, LANE), lambda b, hp, ci: (0, off * (D_B // LANE) + hp))

    def per_head(shape0):
        return pl.BlockSpec((shape0, LANE), lambda b, hp, ci: (0, hp))

    lo3 = 3 * D_B // (2 * LANE)
    in_specs = [rows(0), rows(1), rows(2),
                pl.BlockSpec((chunk, 2 * LANE), lambda b, hp, ci: (b * nc + ci, lo_blk)),
                srow(0), srow(1), srow(2),
                pl.BlockSpec((None, 1, 2 * LANE), lambda b, hp, ci: (b, 0, lo3)),
                pl.BlockSpec((None, HP, HD_B, HD_B), lambda b, hp, ci: (b, hp, 0, 0)),
                prow(0), prow(1), prow(2),
                pl.BlockSpec((1, 2 * LANE), lambda b, hp, ci: (0, lo3)),
                per_head(1), per_head(LORA_W), per_head(1), per_head(LORA_A), per_head(LORA_G),
                per_head(1), per_head(1), per_head(1), per_head(1), per_head(1)]
    o_b, s_fin = pl.pallas_call(
        functools.partial(_rwkv_kernel, n_valid=n_valid),
        out_shape=[jax.ShapeDtypeStruct((bsz * t, D_B), BF16),
                   jax.ShapeDtypeStruct((bsz, N_HEADS_B, HD_B, HD_B), F32)],
        grid=(bsz, nhp, nc),
        in_specs=in_specs,
        out_specs=[pl.BlockSpec((chunk, LANE), lambda b, hp, ci: (b * nc + ci, hp)),
                   pl.BlockSpec((None, HP, HD_B, HD_B), lambda b, hp, ci: (b, hp, 0, 0))],
        scratch_shapes=[pltpu.VMEM((SUBLANE, LANE), F32), pltpu.VMEM((1, 2 * LANE), F32),
                        pltpu.VMEM((HP, HD_B, HD_B), F32)],
        compiler_params=_cparams(("parallel", "parallel", "arbitrary")),
        name="rwkv7",
    )(pp, pp, pp, pp, shift0, shift0, shift0, shift0, wkv0,
      prm["mu"], prm["mu"], prm["mu"], prm["mu"], prm["w0"], prm["w2"], prm["a0"], prm["a2"], prm["g2"],
      prm["kk"], prm["ka"], prm["rk"], prm["gn_g"], prm["gn_b"])
    return o_b, s_fin


def _first_lane(cond, lane):
    return jnp.min(jnp.where(cond, lane, 4 * LANE), axis=-1, keepdims=True)


def _router_kernel(h_ref, w_ref, b_ref, eid_ref, wt_ref):
    logits = _dot3(h_ref[...], w_ref[...]) + b_ref[...]
    lane = lax.broadcasted_iota(I32, logits.shape, 1)
    gmask = lane < N_GROUPS
    lg = jnp.where(gmask, logits, NEG)
    eg = jnp.where(gmask, jnp.exp(lg - jnp.max(lg, axis=-1, keepdims=True)), 0.0)
    gp = eg / jnp.sum(eg, axis=-1, keepdims=True)
    g_w = jnp.max(gp, axis=-1, keepdims=True)
    grp = _first_lane(gmask & (gp == g_w), lane)
    lo = N_GROUPS + grp * E_PER_GROUP
    emask = (lane >= lo) & (lane < lo + E_PER_GROUP)
    le = jnp.where(emask, logits, NEG)
    ee = jnp.where(emask, jnp.exp(le - jnp.max(le, axis=-1, keepdims=True)), 0.0)
    ep = ee / jnp.sum(ee, axis=-1, keepdims=True)
    p1 = jnp.max(jnp.where(emask, ep, -1.0), axis=-1, keepdims=True)
    i1 = _first_lane(emask & (ep == p1), lane)
    rest = emask & (lane != i1)
    p2 = jnp.max(jnp.where(rest, ep, -1.0), axis=-1, keepdims=True)
    i2 = _first_lane(rest & (ep == p2), lane)
    tot = p1 + p2
    eid_ref[...] = jnp.where(lane == 0, i1 - N_GROUPS, jnp.where(lane == 1, i2 - N_GROUPS, 0))
    wt_ref[...] = jnp.where(lane == 0, g_w * p1 / tot, jnp.where(lane == 1, g_w * p2 / tot, 0.0))


def router(h2, wr, br):
    n, d = h2.shape
    tm = _row_tile(n, 512)
    return pl.pallas_call(
        _router_kernel,
        out_shape=[jax.ShapeDtypeStruct((n, LANE), I32), jax.ShapeDtypeStruct((n, LANE), F32)],
        grid=(n // tm,),
        in_specs=[pl.BlockSpec((tm, d), lambda i: (i, 0)),
                  pl.BlockSpec((d, LANE), lambda i: (0, 0)),
                  pl.BlockSpec((1, LANE), lambda i: (0, 0))],
        out_specs=[pl.BlockSpec((tm, LANE), lambda i: (i, 0)), pl.BlockSpec((tm, LANE), lambda i: (i, 0))],
        compiler_params=_cparams(("parallel",)),
        name="router",
    )(h2, wr, br)


def _row_copy(src_hbm, dst_vmem, sem, src_row, dst_row):
    return pltpu.make_async_copy(src_hbm.at[pl.ds(src_row, 1), :], dst_vmem.at[pl.ds(dst_row, 1), :], sem)


def _experts_kernel(blk_e_ref, n_used_ref, tok_ref, h_hbm, w1_ref, w3_ref, w2_ref, y_ref, x_buf, sem):
    i = pl.program_id(0)
    bm = x_buf.shape[0]

    @pl.when(i < n_used_ref[0])
    def _():
        def start(r, carry):
            _row_copy(h_hbm, x_buf, sem, tok_ref[0, r], r).start()
            return carry

        def wait(r, carry):
            _row_copy(h_hbm, x_buf, sem, tok_ref[0, r], r).wait()
            return carry

        lax.fori_loop(0, bm, start, 0)
        lax.fori_loop(0, bm, wait, 0)
        x = x_buf[...]
        h1 = _dot(x, w1_ref[...])
        h3 = _dot(x, w3_ref[...])
        act = h1 * _sigmoid(h1) * h3
        y_ref[...] = _dot(act, w2_ref[...])

    @pl.when(i >= n_used_ref[0])
    def _():
        y_ref[...] = jnp.zeros_like(y_ref)


def experts(h_all, blk_e, n_used, row_tok, w1, w3, w2, bm):
    n_blocks = blk_e.shape[0]
    d = h_all.shape[1]
    grid_spec = pltpu.PrefetchScalarGridSpec(
        num_scalar_prefetch=2,
        grid=(n_blocks,),
        in_specs=[pl.BlockSpec((None, 1, bm), lambda i, be, nu: (i, 0, 0), memory_space=pltpu.SMEM),
                  pl.BlockSpec(memory_space=pl.ANY),
                  pl.BlockSpec((None, d, D_EXPERT), lambda i, be, nu: (be[i], 0, 0)),
                  pl.BlockSpec((None, d, D_EXPERT), lambda i, be, nu: (be[i], 0, 0)),
                  pl.BlockSpec((None, D_EXPERT, d), lambda i, be, nu: (be[i], 0, 0))],
        out_specs=pl.BlockSpec((bm, d), lambda i, be, nu: (i, 0)),
        scratch_shapes=[pltpu.VMEM((bm, d), F32), pltpu.SemaphoreType.DMA(())],
    )
    return pl.pallas_call(
        _experts_kernel,
        out_shape=jax.ShapeDtypeStruct((n_blocks * bm, d), F32),
        grid_spec=grid_spec,
        compiler_params=_cparams(("arbitrary",)),
        name="experts",
    )(blk_e, n_used, row_tok, h_all, w1, w3, w2)


def _combine_kernel(dest_ref, ys_hbm, wt_ref, x_ref, g_ref, o_ref, buf, sem):
    tm = x_ref.shape[0]

    def start(r, carry):
        _row_copy(ys_hbm, buf.at[0], sem, dest_ref[0, 2 * r], r).start()
        _row_copy(ys_hbm, buf.at[1], sem, dest_ref[0, 2 * r + 1], r).start()
        return carry

    def wait(r, carry):
        _row_copy(ys_hbm, buf.at[0], sem, dest_ref[0, 2 * r], r).wait()
        _row_copy(ys_hbm, buf.at[1], sem, dest_ref[0, 2 * r + 1], r).wait()
        return carry

    lax.fori_loop(0, tm, start, 0)
    lax.fori_loop(0, tm, wait, 0)
    wt = wt_ref[...]
    moe = wt[:, 0:1] * buf[0] + wt[:, 1:2] * buf[1]
    o_ref[...] = x_ref[...] + g_ref[...] * moe


def combine(ys, dest, wts, x, gate):
    b, t, d = x.shape
    tm = _row_tile(t, 128)
    nt = t // tm
    dest3 = dest.reshape(b * nt, 1, 2 * tm)
    grid_spec = pltpu.PrefetchScalarGridSpec(
        num_scalar_prefetch=0,
        grid=(b, nt),
        in_specs=[pl.BlockSpec((None, 1, 2 * tm), lambda i, j: (i * nt + j, 0, 0), memory_space=pltpu.SMEM),
                  pl.BlockSpec(memory_space=pl.ANY),
                  pl.BlockSpec((None, tm, LANE), lambda i, j: (i, j, 0)),
                  pl.BlockSpec((None, tm, d), lambda i, j: (i, j, 0)),
                  pl.BlockSpec((None, 1, d), lambda i, j: (i, 0, 0))],
        out_specs=pl.BlockSpec((None, tm, d), lambda i, j: (i, j, 0)),
        scratch_shapes=[pltpu.VMEM((2, tm, d), F32), pltpu.SemaphoreType.DMA(())],
    )
    return pl.pallas_call(
        _combine_kernel,
        out_shape=jax.ShapeDtypeStruct((b, t, d), F32),
        grid_spec=grid_spec,
        compiler_params=_cparams(("arbitrary", "arbitrary")),
        name="moe_combine",
    )(dest3, ys, wts.reshape(b, t, LANE), x, gate.reshape(b, 1, d))


def _dispatch_tables(eid, bm):
    n = eid.shape[0]
    nk = n * 2
    n_blocks = (nk + N_EXPERTS * (bm - 1) + bm - 1) // bm
    flat_e = eid.reshape(-1)
    onehot = (flat_e[:, None] == jnp.arange(N_EXPERTS, dtype=I32)[None, :]).astype(I32)
    csum = jnp.cumsum(onehot, axis=0)
    counts = csum[-1]
    rank = jnp.sum(onehot * (csum - 1), axis=1)
    padded = (counts + bm - 1) // bm * bm
    pad_end = jnp.cumsum(padded)
    pad_start = pad_end - padded
    dest = (pad_start[flat_e] + rank).astype(I32)
    row_tok = jnp.zeros((n_blocks * bm,), I32).at[dest].set(jnp.arange(nk, dtype=I32) // 2)
    blk_e = jnp.minimum(jnp.searchsorted(pad_end, jnp.arange(n_blocks, dtype=I32) * bm, side="right"),
                        N_EXPERTS - 1).astype(I32)
    n_used = (pad_end[-1] // bm).astype(I32).reshape(1)
    return blk_e, n_used, row_tok, dest.reshape(n, 2)


def _final_norm_kernel(x_ref, g_ref, o_ref):
    x = x_ref[...]
    o_ref[...] = x * lax.rsqrt(jnp.mean(x * x, axis=-1, keepdims=True) + RMS_EPS) * g_ref[...]


def final_norm(x, g):
    b, t, d = x.shape
    tr = min(t, 256)
    return pl.pallas_call(
        _final_norm_kernel,
        out_shape=jax.ShapeDtypeStruct((b, t, d), F32),
        grid=(b, t // tr),
        in_specs=[pl.BlockSpec((None, tr, d), lambda i, j: (i, j, 0)), pl.BlockSpec((1, d), lambda i, j: (0, 0))],
        out_specs=pl.BlockSpec((None, tr, d), lambda i, j: (i, j, 0)),
        compiler_params=_cparams(("parallel", "parallel")),
        name="final_norm",
    )(x, g.reshape(1, d))


def _align_in_cols(w, axis):
    def take(a, b_):
        return lax.slice_in_dim(w, a, b_, axis=axis)

    def zeros(nz):
        shp = list(w.shape)
        shp[axis] = nz
        return jnp.zeros(shp, w.dtype)

    gate_parts = []
    for g in range(N_KV):
        gate_parts += [take(OFF_GATE_A + g * 3 * GQA_R, OFF_GATE_A + (g + 1) * 3 * GQA_R), zeros(LANE - 3 * GQA_R)]
    parts = [take(0, OFF_GATE_A)] + gate_parts + [take(OFF_RWKV, OFF_MERGE), zeros(RW_W - SHIFT_W),
                                                   take(OFF_MERGE, OFF_MERGE + 2 * D_MODEL)]
    return jnp.concatenate(parts, axis=axis)


def _pad_lanes(v, width):
    return jnp.pad(v, [(0, 0)] * (v.ndim - 1) + [(0, width - v.shape[-1])])


def kernel(x_prompt, x_sample, cache_cmp, cache_slc, cache_win, state_shift, state_wkv, page_table, c_prompt, c_sample, ln1_g, ln2_g, ada_w, ada_b, w_in, cmp_w1, cmp_b1, cmp_w2, rwkv_mu, rwkv_w0, rwkv_w2, rwkv_a0, rwkv_a2, rwkv_g2, rwkv_kk, rwkv_ka, rwkv_rk, rwkv_gn_g, rwkv_gn_b, w_branch, w_out, router_g_w, router_g_b, router_e_w, router_e_b, exp_w1, exp_w3, exp_w2, final_g):
    depth = w_in.shape[0]
    bp, t, d = x_prompt.shape
    bd, ts, _ = x_sample.shape
    assert ts == 1 and t % KC == 0 and t >= WINDOW + QB
    n_pages = page_table.shape[1]
    past = n_pages * PAGE
    wb = cache_win.shape[2]
    n_phys = cache_cmp.shape[1]
    n_p = bp * t

    slopes = jnp.exp2(-8.0 * (jnp.arange(N_HEADS_A, dtype=F32) + 1.0) / N_HEADS_A)
    c_rows = bp + bd
    c16 = jnp.zeros(((c_rows + SUBLANE - 1) // SUBLANE * SUBLANE, d), F32).at[:bp].set(c_prompt).at[bp:c_rows].set(c_sample)
    mod = adaln(c16, ada_w, ada_b).reshape(depth, c16.shape[0], 6, d)

    xp, xs = x_prompt, x_sample
    outs = {k: [] for k in ("cmp_p", "slc_p", "win_p", "shf_p", "wkv_p", "cmp_s", "slc_s", "win_s", "shf_s", "wkv_s")}
    rs_tile = 8 * BLK

    for l in range(depth):
        mp, ms = mod[l, :bp], mod[l, bp:c_rows]
        w_in_al = _align_in_cols(w_in[l], 1).astype(BF16)
        w1b = cmp_w1[l].astype(BF16)
        w_out_b = w_out[l].astype(BF16)
        prm = {
            "mu": _pad_lanes(rwkv_mu[l][None, :], RW_W),
            "w0": rwkv_w0[l][None, :], "w2": rwkv_w2[l], "a0": rwkv_a0[l][None, :], "a2": rwkv_a2[l],
            "g2": rwkv_g2[l], "kk": rwkv_kk[l][None, :], "ka": rwkv_ka[l][None, :],
            "rk": rwkv_rk[l].reshape(1, D_B), "gn_g": rwkv_gn_g[l][None, :], "gn_b": rwkv_gn_b[l][None, :],
        }
        wr = _pad_lanes(jnp.concatenate([router_g_w[l], router_e_w[l]], axis=1), LANE)
        br = _pad_lanes(jnp.concatenate([router_g_b[l], router_e_b[l]])[None, :], LANE)

        hp_ = norm_mod(xp, ln1_g[l], mp[:, 0], mp[:, 1], BF16).reshape(n_p, d)
        pp_a, ppb = matmul(hp_, w_in_al, 0, A_GATE, also_bf16=True)
        pp_b = matmul(hp_, w_in_al, A_GATE, IN_AL - A_GATE)
        pp = jnp.concatenate([pp_a, pp_b], axis=1)
        cmp_kv = compress(pp, A_KV, 1, w1b, cmp_b1[l], cmp_w2[l])
        o_a = nsa_prompt(slopes, ppb, pp, cmp_kv, bp, t)
        o_b, s_fin = rwkv7(pp, jnp.zeros((bp, 1, RW_W), F32), jnp.zeros((bp, N_HEADS_B, HD_B, HD_B), F32),
                           prm, bp, t, 64, 64)
        merged = branch_merge(o_a, o_b, w_branch[l], pp)
        xp = out_proj_residual(merged.reshape(bp, t, d), w_out_b, xp, mp[:, 2])
        kv6 = pp[:, A_KV:A_GATE].reshape(bp, t, 3, 2, N_KV, HD_A)
        outs["cmp_p"].append(kv6[:, :, 0])
        outs["slc_p"].append(kv6[:, :, 1])
        outs["win_p"].append(kv6[:, t - min(WINDOW, t):, 2])
        outs["shf_p"].append(pp.reshape(bp, t, IN_AL)[:, -1, A_RWKV:A_RWKV + SHIFT_W])
        outs["wkv_p"].append(s_fin)

        hs_ = norm_mod(xs, ln1_g[l], ms[:, 0], ms[:, 1], BF16).reshape(bd, d)
        ps_a, psb = matmul(hs_, w_in_al, 0, A_GATE, also_bf16=True)
        ps_b = matmul(hs_, w_in_al, A_GATE, IN_AL - A_GATE)
        ps = jnp.concatenate([ps_a, ps_b], axis=1)
        past_rows = page_gather(page_table, cache_cmp[l].reshape(n_phys, PAGE, 2 * KV_W))
        cmp_past = compress(past_rows, 0, bd, w1b, cmp_b1[l], cmp_w2[l])
        new_rows = jnp.zeros((bd, BLK, 2 * KV_W), F32).at[:, 0].set(ps[:, A_KV:A_KV + 2 * KV_W])
        cmp_new = compress(new_rows.reshape(bd * BLK, 2 * KV_W), 0, 1, w1b, cmp_b1[l], cmp_w2[l])
        nb_past = past // BLK
        nb_real = nb_past + 1
        nbp = (nb_real + LANE - 1) // LANE * LANE
        cmp_s = jnp.concatenate([cmp_past.reshape(2, N_KV, bd, nb_past, HD_A), cmp_new[:, :, :, None, :],
                                 jnp.zeros((2, N_KV, bd, nbp - nb_real, HD_A), F32)], axis=3)
        ps3 = ps.reshape(bd, 1, IN_AL)
        psb3 = psb.reshape(bd, 1, A_GATE)
        o_cmp, idx_full = nsa_sample_cmp(slopes, psb3, cmp_s, past, nb_real)
        idx_flat = idx_full[:, :, :N_SEL, 0].reshape(-1)
        o_a_s = nsa_sample_sel(idx_flat, page_table, slopes, psb3, ps3,
                               cache_slc[l].reshape(n_phys * (PAGE // BLK), BLK, 2 * KV_W),
                               cache_win[l].reshape(bd, wb, 2 * KV_W), o_cmp, past).reshape(bd, D_A)
        ps_pad = jnp.zeros((bd, SUBLANE, IN_AL), F32).at[:, 0].set(ps).reshape(bd * SUBLANE, IN_AL)
        o_b_s, s_fin_s = rwkv7(ps_pad, _pad_lanes(state_shift[l], RW_W)[:, None, :], state_wkv[l],
                               prm, bd, SUBLANE, SUBLANE, 1)
        o_b_s = o_b_s.reshape(bd, SUBLANE, D_B)[:, 0]
        merged_s = branch_merge(o_a_s, o_b_s, w_branch[l], ps)
        xs = out_proj_residual(merged_s.reshape(bd, 1, d), w_out_b, xs, ms[:, 2])
        kv6s = ps[:, A_KV:A_GATE].reshape(bd, 1, 3, 2, N_KV, HD_A)
        outs["cmp_s"].append(kv6s[:, :, 0])
        outs["slc_s"].append(kv6s[:, :, 1])
        outs["win_s"].append(jnp.concatenate([cache_win[l], kv6s[:, :, 2]], axis=1)[:, 1:])
        outs["shf_s"].append(ps[:, A_RWKV:A_RWKV + SHIFT_W])
        outs["wkv_s"].append(s_fin_s)

        h2p = norm_mod(xp, ln2_g[l], mp[:, 3], mp[:, 4], F32).reshape(n_p, d)
        h2s = norm_mod(xs, ln2_g[l], ms[:, 3], ms[:, 4], F32).reshape(bd, d)
        eid_p, wt_p = router(h2p, wr, br)
        eid_s, wt_s = router(h2s, wr, br)
        h_all = jnp.concatenate([h2p, h2s], axis=0)
        eid = jnp.concatenate([eid_p[:, :2], eid_s[:, :2]], axis=0)
        blk_e, n_used, row_tok, dest = _dispatch_tables(eid, MOE_BLOCK)
        ys = experts(h_all, blk_e, n_used, row_tok.reshape(-1, 1, MOE_BLOCK), exp_w1[l], exp_w3[l], exp_w2[l],
                     MOE_BLOCK)
        xp = combine(ys, dest[:n_p], wt_p, xp, mp[:, 5])
        xs = combine(ys, dest[n_p:], wt_s, xs, ms[:, 5])

    y_prompt = final_norm(xp, final_g)
    y_sample = final_norm(xs, final_g)
    st = lambda k: jnp.stack(outs[k])
    return (y_prompt, y_sample, st("cmp_p"), st("slc_p"), st("win_p"), st("shf_p"), st("wkv_p"),
            st("cmp_s"), st("slc_s"), st("win_s"), st("shf_s"), st("wkv_s"))
```

```python
import functools

import jax
import jax.numpy as jnp
from jax import lax
from jax.experimental import pallas as pl
from jax.experimental.pallas import tpu as pltpu

F32 = jnp.float32
BF16 = jnp.bfloat16
I32 = jnp.int32

LANE = 128
SUBLANE = 8
VMEM_LIMIT = 56 * 1024 * 1024

D_MODEL = 2048
HD_A = 128
N_HEADS_A = 8
N_KV = 2
GQA_R = 4
BLK = 64
N_SEL = 16
WINDOW = 512
CMP_HID = 256
SCALE_A = HD_A ** -0.5
D_A = 1024
D_B = 1024
HD_B = 64
N_HEADS_B = 16
LORA_W, LORA_A, LORA_G = 64, 64, 32
GN_EPS = HD_B * 1e-5
N_GROUPS = 4
E_PER_GROUP = 8
N_EXPERTS = 32
D_EXPERT = 512
MOE_BLOCK = 128
RMS_EPS = 1e-6
PAGE = 128

Q_W = N_HEADS_A * HD_A
KV_W = N_KV * HD_A
OFF_KV = Q_W
OFF_GATE_A = OFF_KV + 6 * KV_W
OFF_RWKV = OFF_GATE_A + 3 * N_HEADS_A
SHIFT_W = 3 * D_B + LORA_W + LORA_A + LORA_G
OFF_MERGE = OFF_RWKV + SHIFT_W

A_KV = Q_W
A_GATE = A_KV + 6 * KV_W
A_LO = A_GATE + N_KV * LANE
A_RKV = A_LO + 2 * LANE
RW_W = 3 * D_B + 2 * LANE
A_MERGE = A_RKV + 3 * D_B
IN_AL = A_MERGE + 2 * D_MODEL
NEG = -1e30


def _cparams(sem):
    return pltpu.CompilerParams(dimension_semantics=sem, vmem_limit_bytes=VMEM_LIMIT)


def _dot(a, b, dims=(((1,), (0,)), ((), ()))):
    return lax.dot_general(a.astype(BF16), b.astype(BF16), dims, preferred_element_type=F32)


def _dot_nt(a, b):
    return _dot(a, b, (((1,), (1,)), ((), ())))


def _dot_tn(a, b):
    return _dot(a, b, (((0,), (0,)), ((), ())))


def _split3(x):
    h = x.astype(BF16)
    r1 = x - h.astype(F32)
    m = r1.astype(BF16)
    lo = (r1 - m.astype(F32)).astype(BF16)
    return h, m, lo


def _dot3(a, b, dims=(((1,), (0,)), ((), ()))):
    ah, am, al = _split3(a)
    bh, bm, bl = _split3(b)
    d = lambda x, y: lax.dot_general(x, y, dims, preferred_element_type=F32)
    return (d(ah, bh) + (d(ah, bm) + d(am, bh))) + ((d(am, bm) + d(ah, bl)) + d(al, bh))


def _sigmoid(x):
    return 1.0 / (1.0 + jnp.exp(-x))


def _softplus(x):
    return jnp.maximum(x, 0.0) + jnp.log(1.0 + jnp.exp(-jnp.abs(x)))


def _adaln_kernel(c_ref, w_ref, b_ref, o_ref):
    c = c_ref[...]
    h = c * _sigmoid(c)
    o_ref[...] = _dot(h, w_ref[...]) + b_ref[...]


def adaln(c16, ada_w, ada_b):
    depth, d, n = ada_w.shape
    tn = 1024
    return pl.pallas_call(
        _adaln_kernel,
        out_shape=jax.ShapeDtypeStruct((depth, c16.shape[0], n), F32),
        grid=(depth, n // tn),
        in_specs=[pl.BlockSpec(c16.shape, lambda l, j: (0, 0)),
                  pl.BlockSpec((None, d, tn), lambda l, j: (l, 0, j)),
                  pl.BlockSpec((None, 1, tn), lambda l, j: (l, 0, j))],
        out_specs=pl.BlockSpec((None, c16.shape[0], tn), lambda l, j: (l, 0, j)),
        compiler_params=_cparams(("parallel", "parallel")),
        name="adaln",
    )(c16, ada_w, ada_b.reshape(depth, 1, n))


def _norm_kernel(x_ref, g_ref, sh_ref, sc_ref, o_ref):
    x = x_ref[...]
    y = x * lax.rsqrt(jnp.mean(x * x, axis=-1, keepdims=True) + RMS_EPS)
    y = y * g_ref[...]
    o_ref[...] = (y * (1.0 + sc_ref[...]) + sh_ref[...]).astype(o_ref.dtype)


def norm_mod(x, g, shift, scale, out_dtype):
    b, t, d = x.shape
    tr = min(t, 256)
    return pl.pallas_call(
        _norm_kernel,
        out_shape=jax.ShapeDtypeStruct((b, t, d), out_dtype),
        grid=(b, t // tr),
        in_specs=[pl.BlockSpec((None, tr, d), lambda i, j: (i, j, 0)),
                  pl.BlockSpec((1, d), lambda i, j: (0, 0)),
                  pl.BlockSpec((None, 1, d), lambda i, j: (i, 0, 0)),
                  pl.BlockSpec((None, 1, d), lambda i, j: (i, 0, 0))],
        out_specs=pl.BlockSpec((None, tr, d), lambda i, j: (i, j, 0)),
        compiler_params=_cparams(("parallel", "parallel")),
        name="norm_mod",
    )(x, g.reshape(1, d), shift.reshape(b, 1, d), scale.reshape(b, 1, d))


IN_TN = 2 * KV_W
N_BF16_TILES = A_GATE // IN_TN
KV_TILE0 = A_KV // IN_TN
KV_ROWS = 2 * N_KV


def _in_proj_kernel(a_ref, w_ref, o_ref, ob_ref, kv_ref):
    j = pl.program_id(1)
    acc = _dot(a_ref[...], w_ref[...])
    o_ref[...] = acc

    @pl.when(j < N_BF16_TILES)
    def _():
        ob_ref[...] = acc.astype(BF16)

    @pl.when((j >= KV_TILE0) & (j < N_BF16_TILES))
    def _():
        tm = acc.shape[0]
        for c4 in range(KV_ROWS):
            kv_ref[pl.ds(c4, tm, stride=KV_ROWS), :] = acc[:, c4 * HD_A:(c4 + 1) * HD_A]


def _row_tile(m, cap):
    return m if m <= cap else cap


def in_proj(a, w):
    m, k = a.shape
    tm = _row_tile(m, 1024)
    tn = IN_TN
    assert m % tm == 0
    return pl.pallas_call(
        _in_proj_kernel,
        out_shape=[jax.ShapeDtypeStruct((m, IN_AL), F32), jax.ShapeDtypeStruct((m, A_GATE), BF16),
                   jax.ShapeDtypeStruct((3, m * KV_ROWS, HD_A), F32)],
        grid=(m // tm, IN_AL // tn),
        in_specs=[pl.BlockSpec((tm, k), lambda i, j: (i, 0)),
                  pl.BlockSpec((k, tn), lambda i, j: (0, j))],
        out_specs=[pl.BlockSpec((tm, tn), lambda i, j: (i, j)),
                   pl.BlockSpec((tm, tn), lambda i, j: (i, jnp.minimum(j, N_BF16_TILES - 1))),
                   pl.BlockSpec((None, tm * KV_ROWS, HD_A),
                                lambda i, j: (jnp.clip(j - KV_TILE0, 0, 2), i, 0))],
        compiler_params=_cparams(("parallel", "arbitrary")),
        name="proj_in",
    )(a, w)


def _branch_kernel(oa_ref, ob_ref, wa_ref, wb_ref, ga_ref, gb_ref, o_ref):
    ya = _dot(oa_ref[...], wa_ref[...])
    yb = _dot(ob_ref[...], wb_ref[...])
    o_ref[...] = (_sigmoid(ga_ref[...]) * ya + _sigmoid(gb_ref[...]) * yb).astype(o_ref.dtype)


def branch_merge(o_a, o_b, w_branch, l, pp):
    m = o_a.shape[0]
    tm = _row_tile(m, 1024)
    tn = 512
    gcol = A_MERGE // tn
    return pl.pallas_call(
        _branch_kernel,
        out_shape=jax.ShapeDtypeStruct((m, D_MODEL), BF16),
        grid=(m // tm, D_MODEL // tn),
        in_specs=[pl.BlockSpec((tm, D_A), lambda i, j: (i, 0)),
                  pl.BlockSpec((tm, D_B), lambda i, j: (i, 0)),
                  pl.BlockSpec((None, D_A, tn), lambda i, j: (l, 0, j)),
                  pl.BlockSpec((None, D_B, tn), lambda i, j: (l, D_A // D_B, j)),
                  pl.BlockSpec((tm, tn), lambda i, j: (i, gcol + j)),
                  pl.BlockSpec((tm, tn), lambda i, j: (i, gcol + D_MODEL // tn + j))],
        out_specs=pl.BlockSpec((tm, tn), lambda i, j: (i, j)),
        compiler_params=_cparams(("parallel", "parallel")),
        name="branch_merge",
    )(o_a, o_b, w_branch, w_branch, pp, pp)


def _resid_kernel(m_ref, w_ref, x_ref, g_ref, o_ref):
    o_ref[...] = x_ref[...] + g_ref[...] * _dot(m_ref[...], w_ref[...])


def out_proj_residual(merged, w_out_l, x, gate):
    b, t, d = x.shape
    tm = _row_tile(t, 1024)
    tn = 512
    nt = t // tm
    return pl.pallas_call(
        _resid_kernel,
        out_shape=jax.ShapeDtypeStruct((b, t, d), F32),
        grid=(b, nt, d // tn),
        in_specs=[pl.BlockSpec((None, tm, d), lambda i, r, j: (i, r, 0)),
                  pl.BlockSpec((d, tn), lambda i, r, j: (0, j)),
                  pl.BlockSpec((None, tm, tn), lambda i, r, j: (i, r, j)),
                  pl.BlockSpec((None, 1, tn), lambda i, r, j: (i, 0, j))],
        out_specs=pl.BlockSpec((None, tm, tn), lambda i, r, j: (i, r, j)),
        compiler_params=_cparams(("parallel", "parallel", "parallel")),
        name="out_proj",
    )(merged, w_out_l, x, gate.reshape(b, 1, d))


T_PER_STEP = 8


def _gelu_tanh(x):
    return 0.5 * x * (1.0 + jnp.tanh(0.7978845608028654 * (x + 0.044715 * x * x * x)))


def _compress_kernel(x_ref, w1_ref, b1_ref, w2_ref, o_ref, acc_ref):
    tc = pl.program_id(3)
    nblk = o_ref.shape[0]

    @pl.when(tc == 0)
    def _():
        acc_ref[...] = jnp.zeros_like(acc_ref)

    acc = acc_ref[...]
    for tl in range(T_PER_STEP):
        t = tc * T_PER_STEP + tl
        xt = x_ref[pl.ds(t, nblk, stride=BLK), :]
        acc = acc + _dot(xt, w1_ref[tl * HD_A:(tl + 1) * HD_A, :])
    acc_ref[...] = acc

    @pl.when(tc == pl.num_programs(3) - 1)
    def _():
        h = _gelu_tanh(acc + b1_ref[...])
        o_ref[...] = _dot(h, w2_ref[...])


def compress(rows, col0, n_slabs, w1b, b1, w2):
    r = rows.shape[0] // n_slabs
    nblk = r // BLK
    cb = col0 // HD_A
    return pl.pallas_call(
        _compress_kernel,
        out_shape=jax.ShapeDtypeStruct((2, N_KV, n_slabs * nblk, HD_A), F32),
        grid=(2, N_KV, n_slabs, BLK // T_PER_STEP),
        in_specs=[pl.BlockSpec((r, HD_A), lambda kv, g, s, tc: (s, cb + kv * N_KV + g)),
                  pl.BlockSpec((None, T_PER_STEP * HD_A, CMP_HID), lambda kv, g, s, tc: (kv, tc, 0)),
                  pl.BlockSpec((None, 1, CMP_HID), lambda kv, g, s, tc: (kv, 0, 0)),
                  pl.BlockSpec((None, CMP_HID, HD_A), lambda kv, g, s, tc: (kv, 0, 0))],
        out_specs=pl.BlockSpec((None, None, nblk, HD_A), lambda kv, g, s, tc: (kv, g, s, 0)),
        scratch_shapes=[pltpu.VMEM((nblk, CMP_HID), F32)],
        compiler_params=_cparams(("parallel", "parallel", "parallel", "arbitrary")),
        name="compress",
    )(rows, w1b, b1.reshape(2, 1, CMP_HID), w2)


QB = 128
KC = 512


def _softmax_cols(s, valid):
    sm = jnp.where(valid, s, NEG)
    m = jnp.max(sm, axis=0, keepdims=True)
    e = jnp.where(valid, jnp.exp(sm - m), 0.0)
    return e, jnp.sum(e, axis=0, keepdims=True)


def _safe(den):
    return jnp.where(den > 0, den, 1.0)


def _select_mask_t(imp, jblk, cur):
    nb = imp.shape[0]
    score = jnp.where(jblk < cur, imp, jnp.where(jblk == cur, GQA_R + 1.0, -1.0))
    rank = jnp.zeros(imp.shape, I32)
    for i in range(nb):
        row = score[i:i + 1, :]
        beats = (row > score) | ((row == score) & (jblk > i))
        rank = rank + beats.astype(I32)
    return (rank < N_SEL) & (jblk <= cur)


def _nsa_prompt_kernel(slopes_ref, q_ref, kc_ref, vc_ref, ks_ref, vs_ref, kw_ref, vw_ref, gate_ref,
                       o_ref, mask_ref):
    g = pl.program_id(1)
    i = pl.program_id(2)
    t_len = ks_ref.shape[0]
    nb = t_len // BLK
    q0 = i * QB
    span = WINDOW + QB

    jblk = lax.broadcasted_iota(I32, (nb, QB), 0)
    qpos_l = q0 + lax.broadcasted_iota(I32, (nb, QB), 1)
    dist_c = qpos_l - ((jblk + 1) * BLK - 1)
    valid_c = dist_c >= 0
    dist_cf = dist_c.astype(F32)
    kc = kc_ref[...]
    vc = vc_ref[...]

    qs = [q_ref[:, r * HD_A:(r + 1) * HD_A] for r in range(GQA_R)]
    slopes = [slopes_ref[g * GQA_R + r] for r in range(GQA_R)]

    imp = jnp.zeros((nb, QB), F32)
    o_cmp = []
    for r in range(GQA_R):
        s = _dot_nt(kc, qs[r]) * SCALE_A - slopes[r] * dist_cf
        e, den = _softmax_cols(s, valid_c)
        p = e / _safe(den)
        imp = imp + p
        o_cmp.append(_dot_tn(p, vc))

    sel = _select_mask_t(imp, jblk, qpos_l // BLK)
    expand = (lax.broadcasted_iota(I32, (nb, t_len), 1) // BLK
              == lax.broadcasted_iota(I32, (nb, t_len), 0))
    key_sel = _dot_tn(sel.astype(F32), expand.astype(F32))
    for cc in range(t_len // KC):
        mask_ref[cc] = key_sel[:, cc * KC:(cc + 1) * KC]

    n_chunks = (q0 + QB + KC - 1) // KC
    row_q = q0 + lax.broadcasted_iota(I32, (QB, KC), 0)
    col_k = lax.broadcasted_iota(I32, (QB, KC), 1)
    o_sel = []
    for r in range(GQA_R):
        def body(c, carry, r=r):
            m, l, acc = carry
            k0 = pl.multiple_of(c * KC, KC)
            kk = ks_ref[pl.ds(k0, KC), :]
            vv = vs_ref[pl.ds(k0, KC), :]
            dist = row_q - (k0 + col_k)
            valid = (mask_ref[c] > 0.5) & (dist >= 0)
            s = _dot_nt(qs[r], kk) * SCALE_A - slopes[r] * dist.astype(F32)
            sm = jnp.where(valid, s, NEG)
            m_new = jnp.maximum(m, jnp.max(sm, axis=-1, keepdims=True))
            alpha = jnp.exp(m - m_new)
            e = jnp.where(valid, jnp.exp(sm - m_new), 0.0)
            l = alpha * l + jnp.sum(e, axis=-1, keepdims=True)
            acc = alpha * acc + _dot(e, vv)
            return m_new, l, acc

        m, l, acc = lax.fori_loop(
            0, n_chunks, body,
            (jnp.full((QB, 1), NEG, F32), jnp.zeros((QB, 1), F32), jnp.zeros((QB, HD_A), F32)))
        o_sel.append(acc / _safe(l))

    start = pl.multiple_of(jnp.maximum(i - WINDOW // QB, 0) * QB, QB)
    kw = kw_ref[pl.ds(start, span), :]
    vw = vw_ref[pl.ds(start, span), :]
    dist_w = (q0 + lax.broadcasted_iota(I32, (QB, span), 0)) - (start + lax.broadcasted_iota(I32, (QB, span), 1))
    valid_w = (dist_w >= 0) & (dist_w < WINDOW)
    dist_wf = dist_w.astype(F32)
    gates = _sigmoid(gate_ref[...])
    for r in range(GQA_R):
        s = _dot_nt(qs[r], kw) * SCALE_A - slopes[r] * dist_wf
        sm = jnp.where(valid_w, s, NEG)
        m = jnp.max(sm, axis=-1, keepdims=True)
        e = jnp.where(valid_w, jnp.exp(sm - m), 0.0)
        den = jnp.sum(e, axis=-1, keepdims=True)
        o_win = _dot(e, vw) / _safe(den)
        o = (gates[:, 3 * r:3 * r + 1] * o_cmp[r] + gates[:, 3 * r + 1:3 * r + 2] * o_sel[r]
             + gates[:, 3 * r + 2:3 * r + 3] * o_win)
        o_ref[:, r * HD_A:(r + 1) * HD_A] = o.astype(o_ref.dtype)


def nsa_prompt(slopes, ppb, pp, cmp_kv, b, t):
    nq = t // QB
    nb = t // BLK
    kvb = A_KV // HD_A

    def kv_spec(branch, kv):
        return pl.BlockSpec((t, HD_A), lambda bi, g, i, s: (bi, kvb + branch * 4 + kv * 2 + g))

    grid_spec = pltpu.PrefetchScalarGridSpec(
        num_scalar_prefetch=1,
        grid=(b, N_KV, nq),
        in_specs=[pl.BlockSpec((QB, GQA_R * HD_A), lambda bi, g, i, s: (bi * nq + i, g)),
                  pl.BlockSpec((None, None, nb, HD_A), lambda bi, g, i, s: (0, g, bi, 0)),
                  pl.BlockSpec((None, None, nb, HD_A), lambda bi, g, i, s: (1, g, bi, 0)),
                  kv_spec(1, 0), kv_spec(1, 1), kv_spec(2, 0), kv_spec(2, 1),
                  pl.BlockSpec((QB, LANE), lambda bi, g, i, s: (bi * nq + i, A_GATE // LANE + g))],
        out_specs=pl.BlockSpec((QB, GQA_R * HD_A), lambda bi, g, i, s: (bi * nq + i, g)),
        scratch_shapes=[pltpu.VMEM((t // KC, QB, KC), F32)],
    )
    return pl.pallas_call(
        _nsa_prompt_kernel,
        out_shape=jax.ShapeDtypeStruct((b * t, D_A), BF16),
        grid_spec=grid_spec,
        compiler_params=_cparams(("parallel", "parallel", "arbitrary")),
        name="nsa_prompt",
    )(slopes, ppb, cmp_kv, cmp_kv, ppb, ppb, ppb, ppb, pp)


PAGE_ROWS = PAGE * KV_ROWS
SLAB_PAGES = 64


def _compress_paged_kernel(pt_ref, pool_hbm, w1_ref, b1_ref, w2_ref, o_ref, slab_ref, acc_ref, sem, *, page0):
    s = pl.program_id(0)
    tc = pl.program_id(1)
    n_slabs = pl.num_programs(0)
    nblk = SLAB_PAGES * (PAGE // BLK)

    def page_copy(slab, p, slot):
        phys = pt_ref[slab * SLAB_PAGES + p]
        return pltpu.make_async_copy(pool_hbm.at[pl.ds((page0 + phys) * PAGE_ROWS, PAGE_ROWS), :],
                                     slab_ref.at[slot, pl.ds(p * PAGE_ROWS, PAGE_ROWS), :], sem.at[slot])

    def start_slab(slab, slot):
        lax.fori_loop(0, SLAB_PAGES, lambda p, c: (page_copy(slab, p, slot).start(), c)[1], 0)

    def wait_slab(slab, slot):
        lax.fori_loop(0, SLAB_PAGES, lambda p, c: (page_copy(slab, p, slot).wait(), c)[1], 0)

    slot = s % 2

    @pl.when(tc == 0)
    def _():
        @pl.when(s == 0)
        def _():
            start_slab(0, 0)

        wait_slab(s, slot)

        @pl.when(s + 1 < n_slabs)
        def _():
            start_slab(s + 1, 1 - slot)

        acc_ref[...] = jnp.zeros_like(acc_ref)

    for tl in range(T_PER_STEP):
        t = tc * T_PER_STEP + tl
        for kv in range(2):
            w1t = w1_ref[kv, tl * HD_A:(tl + 1) * HD_A, :]
            for g in range(N_KV):
                c4 = kv * N_KV + g
                xt = slab_ref[slot, pl.ds(t * KV_ROWS + c4, nblk, stride=BLK * KV_ROWS), :]
                acc_ref[c4] += _dot(xt, w1t)

    @pl.when(tc == pl.num_programs(1) - 1)
    def _():
        for kv in range(2):
            for g in range(N_KV):
                h = _gelu_tanh(acc_ref[kv * N_KV + g] + b1_ref[kv])
                o_ref[kv, g] = _dot(h, w2_ref[kv])


def compress_paged(page_table, pool2d, page0, w1b, b1, w2):
    bd, n_pages = page_table.shape
    assert n_pages % SLAB_PAGES == 0
    n_slabs = bd * n_pages // SLAB_PAGES
    nblk = SLAB_PAGES * (PAGE // BLK)
    grid_spec = pltpu.PrefetchScalarGridSpec(
        num_scalar_prefetch=1,
        grid=(n_slabs, BLK // T_PER_STEP),
        in_specs=[pl.BlockSpec(memory_space=pl.ANY),
                  pl.BlockSpec((2, T_PER_STEP * HD_A, CMP_HID), lambda s, tc, pt: (0, tc, 0)),
                  pl.BlockSpec((2, 1, CMP_HID), lambda s, tc, pt: (0, 0, 0)),
                  pl.BlockSpec((2, CMP_HID, HD_A), lambda s, tc, pt: (0, 0, 0))],
        out_specs=pl.BlockSpec((2, N_KV, nblk, HD_A), lambda s, tc, pt: (0, 0, s, 0)),
        scratch_shapes=[pltpu.VMEM((2, SLAB_PAGES * PAGE_ROWS, HD_A), F32),
                        pltpu.VMEM((2 * N_KV, nblk, CMP_HID), F32),
                        pltpu.SemaphoreType.DMA((2,))],
    )
    return pl.pallas_call(
        functools.partial(_compress_paged_kernel, page0=page0),
        out_shape=jax.ShapeDtypeStruct((2, N_KV, n_slabs * nblk, HD_A), F32),
        grid_spec=grid_spec,
        compiler_params=_cparams(("arbitrary", "arbitrary")),
        name="compress_paged",
    )(page_table.reshape(-1), pool2d, w1b, b1.reshape(2, 1, CMP_HID), w2)


SEL_PAD = 128


def _heads_on_sublanes(q_ref):
    q = q_ref[...].astype(F32)
    row = lax.broadcasted_iota(I32, (SUBLANE, HD_A), 0)
    q8 = jnp.zeros((SUBLANE, HD_A), F32)
    for r in range(GQA_R):
        q8 = jnp.where(row == r, jnp.broadcast_to(q[:, r * HD_A:(r + 1) * HD_A], (SUBLANE, HD_A)), q8)
    return q8


def _slopes_on_sublanes(slopes_ref, g):
    row = lax.broadcasted_iota(I32, (SUBLANE, 1), 0)
    slope = jnp.zeros((SUBLANE, 1), F32)
    for r in range(GQA_R):
        slope = jnp.where(row == r, slopes_ref[g * GQA_R + r], slope)
    return slope


def _softmax_rows(s, valid):
    sm = jnp.where(valid, s, NEG)
    m = jnp.max(sm, axis=-1, keepdims=True)
    e = jnp.where(valid, jnp.exp(sm - m), 0.0)
    return e, jnp.sum(e, axis=-1, keepdims=True)


def _nsa_sample_cmp_kernel(slopes_ref, q_ref, kc_ref, vc_ref, o_ref, idx_ref, *, past, nb_real):
    g = pl.program_id(1)
    nbp = kc_ref.shape[0]
    kc = kc_ref[...]
    vc = vc_ref[...]
    q8 = _heads_on_sublanes(q_ref)
    slope = _slopes_on_sublanes(slopes_ref, g)
    head = lax.broadcasted_iota(I32, (SUBLANE, 1), 0) < GQA_R
    jrow = lax.broadcasted_iota(I32, (1, nbp), 1)
    dist = past - ((jrow + 1) * BLK - 1)
    valid = (dist >= 0) & (jrow < nb_real)
    s = _dot_nt(q8, kc) * SCALE_A - slope * dist.astype(F32)
    e, den = _softmax_rows(s, valid)
    p = e / _safe(den)
    o = _dot(p, vc)
    for r in range(GQA_R):
        o_ref[:, r * HD_A:(r + 1) * HD_A] = o[r:r + 1, :]

    imp = jnp.sum(jnp.where(head, p, 0.0), axis=0, keepdims=True)
    cur = past // BLK
    score_r = jnp.where(jrow < cur, imp, jnp.where(jrow == cur, GQA_R + 1.0, -1.0))
    score_r = jnp.where(jrow < nb_real, score_r, -2.0)
    ii = lax.broadcasted_iota(I32, (nbp, nbp), 0)
    jj = lax.broadcasted_iota(I32, (nbp, nbp), 1)
    sc_b = jnp.broadcast_to(score_r, (nbp, nbp))
    score_c = jnp.sum(jnp.where(ii == jj, sc_b, 0.0), axis=1, keepdims=True)
    beats = (score_c > sc_b) | ((score_c == sc_b) & (ii < jj))
    rank_r = jnp.sum(beats.astype(F32), axis=0, keepdims=True)
    nn = lax.broadcasted_iota(I32, (SEL_PAD, nbp), 0).astype(F32)
    jn = lax.broadcasted_iota(I32, (SEL_PAD, nbp), 1).astype(F32)
    hit = jnp.broadcast_to(rank_r, (SEL_PAD, nbp)) == nn
    idx_c = jnp.sum(jnp.where(hit, jn, 0.0), axis=1, keepdims=True)
    idx_ref[...] = jnp.broadcast_to(idx_c, (SEL_PAD, LANE)).astype(I32)


def nsa_sample_cmp(slopes, ppb3, cmp_kv_s, past, nb_real):
    bd = ppb3.shape[0]
    nbp = cmp_kv_s.shape[3]
    grid_spec = pltpu.PrefetchScalarGridSpec(
        num_scalar_prefetch=1,
        grid=(bd, N_KV),
        in_specs=[pl.BlockSpec((None, 1, GQA_R * HD_A), lambda bi, g, s: (bi, 0, g)),
                  pl.BlockSpec((None, None, None, nbp, HD_A), lambda bi, g, s: (0, g, bi, 0, 0)),
                  pl.BlockSpec((None, None, None, nbp, HD_A), lambda bi, g, s: (1, g, bi, 0, 0))],
        out_specs=[pl.BlockSpec((None, 1, GQA_R * HD_A), lambda bi, g, s: (bi, 0, g)),
                   pl.BlockSpec((None, None, SEL_PAD, LANE), lambda bi, g, s: (bi, g, 0, 0))],
    )
    return pl.pallas_call(
        functools.partial(_nsa_sample_cmp_kernel, past=past, nb_real=nb_real),
        out_shape=[jax.ShapeDtypeStruct((bd, 1, D_A), F32),
                   jax.ShapeDtypeStruct((bd, N_KV, SEL_PAD, LANE), I32)],
        grid_spec=grid_spec,
        compiler_params=_cparams(("parallel", "parallel")),
        name="nsa_sample_cmp",
    )(slopes, ppb3, cmp_kv_s, cmp_kv_s)


W_PAD = LANE


def _nsa_sample_sel_kernel(idx_ref, pt_ref, slopes_ref, q_ref, pool_ref, kn_ref, vn_ref,
                           wbuf_ref, kwn_ref, vwn_ref, gate_ref, oc_ref, o_ref,
                           ksel_ref, vsel_ref, kwin_ref, vwin_ref, *, past):
    bi = pl.program_id(0)
    g = pl.program_id(1)
    n = pl.program_id(2)
    nb_past = past // BLK
    base = (bi * N_KV + g) * N_SEL
    idx_n = idx_ref[base + n]
    r0 = pl.multiple_of(n * BLK, BLK)

    @pl.when(idx_n < nb_past)
    def _():
        ksel_ref[pl.ds(r0, BLK), :] = pool_ref[pl.ds(g, BLK, stride=KV_ROWS), :]
        vsel_ref[pl.ds(r0, BLK), :] = pool_ref[pl.ds(N_KV + g, BLK, stride=KV_ROWS), :]

    @pl.when(idx_n >= nb_past)
    def _():
        first = lax.broadcasted_iota(I32, (BLK, HD_A), 0) == 0
        ksel_ref[pl.ds(r0, BLK), :] = jnp.where(first, kn_ref[...], 0.0)
        vsel_ref[pl.ds(r0, BLK), :] = jnp.where(first, vn_ref[...], 0.0)

    @pl.when(n == N_SEL - 1)
    def _():
        q8 = _heads_on_sublanes(q_ref)
        slope = _slopes_on_sublanes(slopes_ref, g)
        cur = past // BLK
        lane = lax.broadcasted_iota(I32, (1, N_SEL * BLK), 1)
        slot = lane // BLK
        idx_row = jnp.zeros((1, N_SEL * BLK), I32)
        for m in range(N_SEL):
            idx_row = jnp.where(slot == m, idx_ref[base + m], idx_row)
        dist = past - (idx_row * BLK + lane % BLK)
        valid = (idx_row <= cur) & (dist >= 0)
        s = _dot_nt(q8, ksel_ref[...]) * SCALE_A - slope * dist.astype(F32)
        e, den = _softmax_rows(s, valid)
        o_sel = _dot(e / _safe(den), vsel_ref[...])
        wb = wbuf_ref.shape[0] // KV_ROWS
        kwin_ref[0:wb, :] = wbuf_ref[pl.ds(g, wb, stride=KV_ROWS), :]
        vwin_ref[0:wb, :] = wbuf_ref[pl.ds(N_KV + g, wb, stride=KV_ROWS), :]
        first = lax.broadcasted_iota(I32, (W_PAD, HD_A), 0) == 0
        kwin_ref[wb:wb + W_PAD, :] = jnp.where(first, kwn_ref[...], 0.0)
        vwin_ref[wb:wb + W_PAD, :] = jnp.where(first, vwn_ref[...], 0.0)
        lane_w = lax.broadcasted_iota(I32, (1, wb + W_PAD), 1)
        dist_w = wb - lane_w
        valid_w = (dist_w >= 0) & (dist_w < WINDOW)
        s = _dot_nt(q8, kwin_ref[...]) * SCALE_A - slope * dist_w.astype(F32)
        e, den = _softmax_rows(s, valid_w)
        o_win = _dot(e / _safe(den), vwin_ref[...])
        gates = _sigmoid(gate_ref[...])
        for r in range(GQA_R):
            o = (gates[:, 3 * r:3 * r + 1] * oc_ref[:, r * HD_A:(r + 1) * HD_A]
                 + gates[:, 3 * r + 1:3 * r + 2] * o_sel[r:r + 1, :]
                 + gates[:, 3 * r + 2:3 * r + 3] * o_win[r:r + 1, :])
            o_ref[:, r * HD_A:(r + 1) * HD_A] = o.astype(o_ref.dtype)


def nsa_sample_sel(idx_flat, page_table, slopes, ppb3, pp3, pool2d, page0, win2d, seq0, wb, o_cmp, past):
    bd = ppb3.shape[0]
    n_pages = page_table.shape[1]
    bpp = PAGE // BLK
    nb_past = past // BLK
    kvb = A_KV // HD_A

    def pool_map(bi, g, n, idx, pt, s):
        ip = jnp.clip(idx[(bi * N_KV + g) * N_SEL + n], 0, nb_past - 1)
        phys = pt[bi * n_pages + ip // bpp]
        return ((page0 + phys) * bpp + ip % bpp, 0)

    def new_spec(branch, kv):
        return pl.BlockSpec((None, 1, HD_A), lambda bi, g, n, idx, pt, s: (bi, 0, kvb + branch * 4 + kv * 2 + g))

    grid_spec = pltpu.PrefetchScalarGridSpec(
        num_scalar_prefetch=3,
        grid=(bd, N_KV, N_SEL),
        in_specs=[pl.BlockSpec((None, 1, GQA_R * HD_A), lambda bi, g, n, idx, pt, s: (bi, 0, g)),
                  pl.BlockSpec((BLK * KV_ROWS, HD_A), pool_map), new_spec(1, 0), new_spec(1, 1),
                  pl.BlockSpec((wb * KV_ROWS, HD_A), lambda bi, g, n, idx, pt, s: (seq0 + bi, 0)),
                  new_spec(2, 0), new_spec(2, 1),
                  pl.BlockSpec((None, 1, LANE), lambda bi, g, n, idx, pt, s: (bi, 0, A_GATE // LANE + g)),
                  pl.BlockSpec((None, 1, GQA_R * HD_A), lambda bi, g, n, idx, pt, s: (bi, 0, g))],
        out_specs=pl.BlockSpec((None, 1, GQA_R * HD_A), lambda bi, g, n, idx, pt, s: (bi, 0, g)),
        scratch_shapes=[pltpu.VMEM((N_SEL * BLK, HD_A), F32), pltpu.VMEM((N_SEL * BLK, HD_A), F32),
                        pltpu.VMEM((wb + W_PAD, HD_A), F32), pltpu.VMEM((wb + W_PAD, HD_A), F32)],
    )
    return pl.pallas_call(
        functools.partial(_nsa_sample_sel_kernel, past=past),
        out_shape=jax.ShapeDtypeStruct((bd, 1, D_A), BF16),
        grid_spec=grid_spec,
        compiler_params=_cparams(("parallel", "parallel", "arbitrary")),
        name="nsa_sample_sel",
    )(idx_flat, page_table.reshape(-1), slopes, ppb3, pool2d, pp3, pp3, win2d, pp3, pp3, pp3, o_cmp)


def _cumsum_rows(x):
    c = x.shape[0]
    row = lax.broadcasted_iota(I32, x.shape, 0)
    sh = 1
    while sh < c:
        x = x + jnp.where(row >= sh, pltpu.roll(x, sh, 0), 0.0)
        sh *= 2
    return x


def _split2(x):
    hi = x.astype(BF16)
    return hi, (x - hi.astype(F32)).astype(BF16)


def _dot2(a, b, dims=(((1,), (0,)), ((), ()))):
    ah, al = _split2(a)
    bh, bl = _split2(b)
    d = lambda x, y: lax.dot_general(x, y, dims, preferred_element_type=F32)
    return d(ah, bh) + (d(ah, bl) + d(al, bh))


_dot_t = _dot2
NT = (((1,), (1,)), ((), ()))
TN = (((0,), (0,)), ((), ()))
CHUNK_LB = 8
STATE_GROUP = 4


def _rwkv_chunk_kernel(pr_ref, pk_ref, pv_ref, pl_ref, qr_ref, qk_ref, qv_ref, ql_ref,
                       sr_ref, sk_ref, sv_ref, sl_ref,
                       mur_ref, muk_ref, muv_ref, mul_ref, w0_ref, w2_ref, a0_ref, a2_ref, g2_ref,
                       kkp_ref, kap_ref, rkp_ref, gnb_ref,
                       y0_ref, rt_ref, bonus_ref, gate_ref, mm_ref, g0_ref, *, n_valid):
    ci = pl.program_id(2)
    c = pr_ref.shape[0]
    row = lax.broadcasted_iota(I32, (c, 1), 0)
    first_chunk = ci == 0

    def mix(p_ref, q_ref, s_ref, mu_ref):
        p = p_ref[...]
        prev = jnp.where(first_chunk, s_ref[...], q_ref[SUBLANE - 1:SUBLANE, :])
        shifted = jnp.where(row == 0, prev, pltpu.roll(p, 1, 0))
        return p + mu_ref[...] * (shifted - p)

    xr = mix(pr_ref, qr_ref, sr_ref, mur_ref)
    xk = mix(pk_ref, qk_ref, sk_ref, muk_ref)
    xv = mix(pv_ref, qv_ref, sv_ref, muv_ref)
    xl = mix(pl_ref, ql_ref, sl_ref, mul_ref)

    dw = xl[:, 0:LORA_W]
    da = xl[:, LORA_W:LORA_W + LORA_A]
    dg = xl[:, LORA_W + LORA_A:LORA_W + LORA_A + LORA_G]
    wlog = -_softplus(-(w0_ref[...] + _dot(jnp.tanh(dw), w2_ref[...]))) - 0.5
    logdec = -jnp.exp(wlog)
    a = _sigmoid(a0_ref[...] + _dot(da, a2_ref[...]))
    gate_ref[...] = _dot(_sigmoid(dg), g2_ref[...])
    kkv = xk * kkp_ref[...]
    kmod = xk * (1.0 + (a - 1.0) * kap_ref[...])
    if n_valid < c:
        live = row < n_valid
        logdec = jnp.where(live, logdec, 0.0)
        kmod = jnp.where(live, kmod, 0.0)
        a = jnp.where(live, a, 0.0)
        xv = jnp.where(live, xv, 0.0)
    cum = _cumsum_rows(logdec)
    cum_ex = cum - logdec
    cum_end = cum[c - 1:c, :]

    ti = lax.broadcasted_iota(I32, (c, c), 0)
    si = lax.broadcasted_iota(I32, (c, c), 1)
    lower_strict = ti > si
    lower_incl = ti >= si
    eye_c = (ti == si).astype(F32)
    eye_k = (lax.broadcasted_iota(I32, (HD_B, HD_B), 0) == lax.broadcasted_iota(I32, (HD_B, HD_B), 1)).astype(F32)
    rk_all = rkp_ref[...]
    gnb = gnb_ref[...]

    heads = range(pr_ref.shape[1] // HD_B)
    sls = [slice(h * HD_B, (h + 1) * HD_B) for h in heads]
    r_ = [xr[:, s] for s in sls]
    k_ = [kmod[:, s] for s in sls]
    v_ = [xv[:, s] for s in sls]
    kk_ = [kkv[:, s] for s in sls]
    kk_ = [x / jnp.maximum(jnp.sqrt(jnp.sum(x * x, axis=-1, keepdims=True)), 1e-12) for x in kk_]
    b_ = [kk_[h] * a[:, sls[h]] for h in heads]
    kap_ = [kk_[h] * jnp.exp(cum_ex[:, sls[h]]) for h in heads]
    rt_ = [r_[h] * jnp.exp(cum[:, sls[h]]) for h in heads]
    inv_ = [jnp.exp(-cum[:, sls[h]]) for h in heads]
    amat = [_dot2(jnp.concatenate([kap_[h], rt_[h]], axis=0),
                  jnp.concatenate([k_[h] * inv_[h], b_[h] * inv_[h]], axis=0), NT) for h in heads]
    a_kk = [jnp.where(lower_strict, m[0:c, 0:c], 0.0) for m in amat]
    a_kb = [jnp.where(lower_strict, m[0:c, c:2 * c], 0.0) for m in amat]
    a_rk = [jnp.where(lower_incl, m[c:2 * c, 0:c], 0.0) for m in amat]
    a_rb = [jnp.where(lower_incl, m[c:2 * c, c:2 * c], 0.0) for m in amat]

    half = ((ti % 2) == 1) & (si == ti - 1)
    tinv = [eye_c - jnp.where(half, m, 0.0) for m in a_kb]
    av = [_dot2(jnp.concatenate([a_kk[h], a_rk[h]], axis=0), v_[h]) for h in heads]
    w = 2
    while w < c:
        off = (ti // (2 * w) == si // (2 * w)) & ((ti // w) % 2 == 1) & ((si // w) % 2 == 0)
        l_off = [jnp.where(off, m, 0.0) for m in a_kb]
        ld = [_dot_t(l_off[h], tinv[h]) for h in heads]
        tinv = [tinv[h] - _dot_t(tinv[h], ld[h]) for h in heads]
        w *= 2

    tx = [_dot2(tinv[h], jnp.concatenate([av[h][0:c], kap_[h]], axis=1)) for h in heads]
    arb_tx = [_dot2(a_rb[h], tx[h]) for h in heads]
    dec_end = [jnp.exp(cum_end[:, sls[h]] - cum[:, sls[h]]) for h in heads]
    k_end = [k_[h] * dec_end[h] for h in heads]
    b_end = [b_[h] * dec_end[h] for h in heads]
    g0 = [_dot2(jnp.concatenate([v_[h], -tx[h][:, 0:HD_B]], axis=0),
                jnp.concatenate([k_end[h], b_end[h]], axis=0), TN) for h in heads]
    ktb = [_dot2(tx[h][:, HD_B:2 * HD_B], b_end[h], TN) for h in heads]
    for h in heads:
        s = sls[h]
        y0_ref[:, s] = av[h][c:2 * c] - arb_tx[h][:, 0:HD_B]
        rt_ref[:, s] = rt_[h] - arb_tx[h][:, HD_B:2 * HD_B]
        bonus_ref[:, s] = jnp.sum(r_[h] * k_[h] * rk_all[:, s], axis=-1, keepdims=True) * v_[h] + gnb[:, s]
        mm_ref[h] = eye_k * jnp.exp(cum_end[:, s]) - ktb[h]
        g0_ref[h] = g0[h]


def _rwkv_state_kernel(y0_ref, rt_ref, bonus_ref, gate_ref, mm_ref, g0_ref, s0_ref, gng_ref,
                       o_ref, sT_ref, st_ref):
    @pl.when(pl.program_id(1) == 0)
    def _():
        st_ref[...] = s0_ref[...]

    gng = gng_ref[...]
    for h0 in range(0, N_HEADS_B, STATE_GROUP):
        heads = range(h0, h0 + STATE_GROUP)
        sls = {h: slice(h * HD_B, (h + 1) * HD_B) for h in heads}
        s0 = {h: st_ref[h] for h in heads}
        s_new = {h: _dot2(s0[h], mm_ref[h]) + g0_ref[h] for h in heads}
        y = {h: y0_ref[:, sls[h]] + _dot2(rt_ref[:, sls[h]], s0[h], NT) for h in heads}
        for h in heads:
            st_ref[h] = s_new[h]
            sT_ref[h] = s_new[h]
        mean = {h: jnp.mean(y[h], axis=-1, keepdims=True) for h in heads}
        dev = {h: y[h] - mean[h] for h in heads}
        var = {h: jnp.mean(jnp.square(dev[h]), axis=-1, keepdims=True) for h in heads}
        for h in heads:
            yn = dev[h] * lax.rsqrt(var[h] + GN_EPS) * gng[:, sls[h]]
            o_ref[:, sls[h]] = ((yn + bonus_ref[:, sls[h]]) * gate_ref[:, sls[h]]).astype(o_ref.dtype)


def rwkv7(pp, shift0, wkv0, prm, bsz, t, chunk, n_valid):
    nc = t // chunk
    lb = CHUNK_LB * LANE
    ngrp = D_B // lb
    hs = lb // HD_B
    assert A_RKV % lb == 0
    rb = A_RKV // lb
    per = D_B // lb
    lo_blk = A_LO // (2 * LANE)
    lo3 = 3 * D_B // (2 * LANE)
    sub = chunk // SUBLANE

    def rows(off):
        return pl.BlockSpec((chunk, lb), lambda b, hg, ci: (b * nc + ci, rb + off * per + hg))

    def prev_rows(off):
        return pl.BlockSpec((SUBLANE, lb),
                            lambda b, hg, ci: (jnp.maximum((b * nc + ci) * sub - 1, 0), rb + off * per + hg))

    def srow(off):
        return pl.BlockSpec((None, 1, lb), lambda b, hg, ci: (b, 0, off * per + hg))

    def prow(off):
        return pl.BlockSpec((1, lb), lambda b, hg, ci: (0, off * per + hg))

    def per_head(shape0):
        return pl.BlockSpec((shape0, lb), lambda b, hg, ci: (0, hg))

    lo_spec = lambda rws, imap: pl.BlockSpec((rws, 2 * LANE), imap)
    in_specs = [rows(0), rows(1), rows(2), lo_spec(chunk, lambda b, hg, ci: (b * nc + ci, lo_blk)),
                prev_rows(0), prev_rows(1), prev_rows(2),
                lo_spec(SUBLANE, lambda b, hg, ci: (jnp.maximum((b * nc + ci) * sub - 1, 0), lo_blk)),
                srow(0), srow(1), srow(2), pl.BlockSpec((None, 1, 2 * LANE), lambda b, hg, ci: (b, 0, lo3)),
                prow(0), prow(1), prow(2), lo_spec(1, lambda b, hg, ci: (0, lo3)),
                per_head(1), per_head(LORA_W), per_head(1), per_head(LORA_A), per_head(LORA_G),
                per_head(1), per_head(1), per_head(1), per_head(1)]
    row_out = pl.BlockSpec((chunk, lb), lambda b, hg, ci: (b * nc + ci, hg))
    mat_out = pl.BlockSpec((None, None, hs, HD_B, HD_B), lambda b, hg, ci: (b, ci, hg, 0, 0))
    n = bsz * t
    y0, rt, bonus, gate, mm, g0 = pl.pallas_call(
        functools.partial(_rwkv_chunk_kernel, n_valid=n_valid),
        out_shape=[jax.ShapeDtypeStruct((n, D_B), F32)] * 4
        + [jax.ShapeDtypeStruct((bsz, nc, N_HEADS_B, HD_B, HD_B), F32)] * 2,
        grid=(bsz, ngrp, nc),
        in_specs=in_specs,
        out_specs=[row_out] * 4 + [mat_out] * 2,
        compiler_params=_cparams(("parallel", "parallel", "parallel")),
        name="rwkv_chunk",
    )(pp, pp, pp, pp, pp, pp, pp, pp, shift0, shift0, shift0, shift0,
      prm["mu"], prm["mu"], prm["mu"], prm["mu"], prm["w0"], prm["w2"], prm["a0"], prm["a2"], prm["g2"],
      prm["kk"], prm["ka"], prm["rk"], prm["gn_b"])

    row_in = pl.BlockSpec((chunk, D_B), lambda b, ci: (b * nc + ci, 0))
    mat_in = pl.BlockSpec((None, None, N_HEADS_B, HD_B, HD_B), lambda b, ci: (b, ci, 0, 0, 0))
    state = pl.BlockSpec((None, N_HEADS_B, HD_B, HD_B), lambda b, ci: (b, 0, 0, 0))
    o_b, s_fin = pl.pallas_call(
        _rwkv_state_kernel,
        out_shape=[jax.ShapeDtypeStruct((n, D_B), BF16),
                   jax.ShapeDtypeStruct((bsz, N_HEADS_B, HD_B, HD_B), F32)],
        grid=(bsz, nc),
        in_specs=[row_in, row_in, row_in, row_in, mat_in, mat_in, state,
                  pl.BlockSpec((1, D_B), lambda b, ci: (0, 0))],
        out_specs=[row_in, state],
        scratch_shapes=[pltpu.VMEM((N_HEADS_B, HD_B, HD_B), F32)],
        compiler_params=_cparams(("parallel", "arbitrary")),
        name="rwkv_state",
    )(y0, rt, bonus, gate, mm, g0, wkv0, prm["gn_g"])
    return o_b, s_fin


def _first_lane(cond, lane):
    return jnp.min(jnp.where(cond, lane, 4 * LANE), axis=-1, keepdims=True)


def _router_kernel(h_ref, w_ref, b_ref, eid_ref, wt_ref):
    logits = _dot3(h_ref[...], w_ref[...]) + b_ref[...]
    lane = lax.broadcasted_iota(I32, logits.shape, 1)
    gmask = lane < N_GROUPS
    lg = jnp.where(gmask, logits, NEG)
    eg = jnp.where(gmask, jnp.exp(lg - jnp.max(lg, axis=-1, keepdims=True)), 0.0)
    gp = eg / jnp.sum(eg, axis=-1, keepdims=True)
    g_w = jnp.max(gp, axis=-1, keepdims=True)
    grp = _first_lane(gmask & (gp == g_w), lane)
    lo = N_GROUPS + grp * E_PER_GROUP
    emask = (lane >= lo) & (lane < lo + E_PER_GROUP)
    le = jnp.where(emask, logits, NEG)
    ee = jnp.where(emask, jnp.exp(le - jnp.max(le, axis=-1, keepdims=True)), 0.0)
    ep = ee / jnp.sum(ee, axis=-1, keepdims=True)
    p1 = jnp.max(jnp.where(emask, ep, -1.0), axis=-1, keepdims=True)
    i1 = _first_lane(emask & (ep == p1), lane)
    rest = emask & (lane != i1)
    p2 = jnp.max(jnp.where(rest, ep, -1.0), axis=-1, keepdims=True)
    i2 = _first_lane(rest & (ep == p2), lane)
    tot = p1 + p2
    eid_ref[...] = jnp.where(lane == 0, i1 - N_GROUPS, jnp.where(lane == 1, i2 - N_GROUPS, 0))
    wt_ref[...] = jnp.where(lane == 0, g_w * p1 / tot, jnp.where(lane == 1, g_w * p2 / tot, 0.0))


def router(h2, wr, br):
    n, d = h2.shape
    tm = _row_tile(n, 512)
    return pl.pallas_call(
        _router_kernel,
        out_shape=[jax.ShapeDtypeStruct((n, LANE), I32), jax.ShapeDtypeStruct((n, LANE), F32)],
        grid=(n // tm,),
        in_specs=[pl.BlockSpec((tm, d), lambda i: (i, 0)),
                  pl.BlockSpec((d, LANE), lambda i: (0, 0)),
                  pl.BlockSpec((1, LANE), lambda i: (0, 0))],
        out_specs=[pl.BlockSpec((tm, LANE), lambda i: (i, 0)), pl.BlockSpec((tm, LANE), lambda i: (i, 0))],
        compiler_params=_cparams(("parallel",)),
        name="router",
    )(h2, wr, br)


def _row_copy(src_hbm, dst_vmem, sem, src_row, dst_row):
    return pltpu.make_async_copy(src_hbm.at[pl.ds(src_row, 1), :], dst_vmem.at[pl.ds(dst_row, 1), :], sem)


def _experts_kernel(blk_e_ref, n_used_ref, tok_ref, h_hbm, w1_ref, w3_ref, w2_ref, y_ref, x_buf, sem):
    i = pl.program_id(0)
    bm = x_buf.shape[0]

    @pl.when(i < n_used_ref[0])
    def _():
        def start(r, carry):
            _row_copy(h_hbm, x_buf, sem, tok_ref[0, r], r).start()
            return carry

        def wait(r, carry):
            _row_copy(h_hbm, x_buf, sem, tok_ref[0, r], r).wait()
            return carry

        lax.fori_loop(0, bm, start, 0)
        lax.fori_loop(0, bm, wait, 0)
        x = x_buf[...]
        h1 = _dot(x, w1_ref[...])
        h3 = _dot(x, w3_ref[...])
        act = h1 * _sigmoid(h1) * h3
        y_ref[...] = _dot(act, w2_ref[...])

    @pl.when(i >= n_used_ref[0])
    def _():
        y_ref[...] = jnp.zeros_like(y_ref)


def experts(h_all, blk_e, n_used, row_tok, w1, w3, w2, l, bm):
    n_blocks = blk_e.shape[0]
    d = h_all.shape[1]
    grid_spec = pltpu.PrefetchScalarGridSpec(
        num_scalar_prefetch=2,
        grid=(n_blocks,),
        in_specs=[pl.BlockSpec((None, 1, bm), lambda i, be, nu: (i, 0, 0), memory_space=pltpu.SMEM),
                  pl.BlockSpec(memory_space=pl.ANY),
                  pl.BlockSpec((None, None, d, D_EXPERT), lambda i, be, nu: (l, be[i], 0, 0)),
                  pl.BlockSpec((None, None, d, D_EXPERT), lambda i, be, nu: (l, be[i], 0, 0)),
                  pl.BlockSpec((None, None, D_EXPERT, d), lambda i, be, nu: (l, be[i], 0, 0))],
        out_specs=pl.BlockSpec((bm, d), lambda i, be, nu: (i, 0)),
        scratch_shapes=[pltpu.VMEM((bm, d), F32), pltpu.SemaphoreType.DMA(())],
    )
    return pl.pallas_call(
        _experts_kernel,
        out_shape=jax.ShapeDtypeStruct((n_blocks * bm, d), F32),
        grid_spec=grid_spec,
        compiler_params=_cparams(("arbitrary",)),
        name="experts",
    )(blk_e, n_used, row_tok, h_all, w1, w3, w2)


def _combine_kernel(dest_ref, ys_hbm, wt_ref, x_ref, g_ref, o_ref, buf, sem):
    tm = x_ref.shape[0]

    def start(r, carry):
        _row_copy(ys_hbm, buf.at[0], sem, dest_ref[0, 2 * r], r).start()
        _row_copy(ys_hbm, buf.at[1], sem, dest_ref[0, 2 * r + 1], r).start()
        return carry

    def wait(r, carry):
        _row_copy(ys_hbm, buf.at[0], sem, dest_ref[0, 2 * r], r).wait()
        _row_copy(ys_hbm, buf.at[1], sem, dest_ref[0, 2 * r + 1], r).wait()
        return carry

    lax.fori_loop(0, tm, start, 0)
    lax.fori_loop(0, tm, wait, 0)
    wt = wt_ref[...]
    moe = wt[:, 0:1] * buf[0] + wt[:, 1:2] * buf[1]
    o_ref[...] = x_ref[...] + g_ref[...] * moe


def combine(ys, dest, wts, x, gate):
    b, t, d = x.shape
    tm = _row_tile(t, 128)
    nt = t // tm
    dest3 = dest.reshape(b * nt, 1, 2 * tm)
    grid_spec = pltpu.PrefetchScalarGridSpec(
        num_scalar_prefetch=0,
        grid=(b, nt),
        in_specs=[pl.BlockSpec((None, 1, 2 * tm), lambda i, j: (i * nt + j, 0, 0), memory_space=pltpu.SMEM),
                  pl.BlockSpec(memory_space=pl.ANY),
                  pl.BlockSpec((None, tm, LANE), lambda i, j: (i, j, 0)),
                  pl.BlockSpec((None, tm, d), lambda i, j: (i, j, 0)),
                  pl.BlockSpec((None, 1, d), lambda i, j: (i, 0, 0))],
        out_specs=pl.BlockSpec((None, tm, d), lambda i, j: (i, j, 0)),
        scratch_shapes=[pltpu.VMEM((2, tm, d), F32), pltpu.SemaphoreType.DMA(())],
    )
    return pl.pallas_call(
        _combine_kernel,
        out_shape=jax.ShapeDtypeStruct((b, t, d), F32),
        grid_spec=grid_spec,
        compiler_params=_cparams(("arbitrary", "arbitrary")),
        name="moe_combine",
    )(dest3, ys, wts.reshape(b, t, LANE), x, gate.reshape(b, 1, d))


def _dispatch_tables(eid, bm):
    n = eid.shape[0]
    nk = n * 2
    n_blocks = (nk + N_EXPERTS * (bm - 1) + bm - 1) // bm
    flat_e = eid.reshape(-1)
    onehot = (flat_e[:, None] == jnp.arange(N_EXPERTS, dtype=I32)[None, :]).astype(I32)
    csum = jnp.cumsum(onehot, axis=0)
    counts = csum[-1]
    rank = jnp.sum(onehot * (csum - 1), axis=1)
    padded = (counts + bm - 1) // bm * bm
    pad_end = jnp.cumsum(padded)
    pad_start = pad_end - padded
    dest = (pad_start[flat_e] + rank).astype(I32)
    row_tok = jnp.zeros((n_blocks * bm,), I32).at[dest].set(jnp.arange(nk, dtype=I32) // 2)
    blk_start = jnp.arange(n_blocks, dtype=I32) * bm
    blk_e = jnp.minimum(jnp.sum((pad_end[None, :] <= blk_start[:, None]).astype(I32), axis=1), N_EXPERTS - 1)
    n_used = (pad_end[-1] // bm).astype(I32).reshape(1)
    return blk_e, n_used, row_tok, dest.reshape(n, 2)


def _final_norm_kernel(x_ref, g_ref, o_ref):
    x = x_ref[...]
    o_ref[...] = x * lax.rsqrt(jnp.mean(x * x, axis=-1, keepdims=True) + RMS_EPS) * g_ref[...]


def final_norm(x, g):
    b, t, d = x.shape
    tr = min(t, 256)
    return pl.pallas_call(
        _final_norm_kernel,
        out_shape=jax.ShapeDtypeStruct((b, t, d), F32),
        grid=(b, t // tr),
        in_specs=[pl.BlockSpec((None, tr, d), lambda i, j: (i, j, 0)), pl.BlockSpec((1, d), lambda i, j: (0, 0))],
        out_specs=pl.BlockSpec((None, tr, d), lambda i, j: (i, j, 0)),
        compiler_params=_cparams(("parallel", "parallel")),
        name="final_norm",
    )(x, g.reshape(1, d))


def _align_in_cols(w, axis):
    def take(a, b_):
        return lax.slice_in_dim(w, a, b_, axis=axis)

    def zeros(nz):
        shp = list(w.shape)
        shp[axis] = nz
        return jnp.zeros(shp, w.dtype)

    gate_parts = []
    for g in range(N_KV):
        gate_parts += [take(OFF_GATE_A + g * 3 * GQA_R, OFF_GATE_A + (g + 1) * 3 * GQA_R), zeros(LANE - 3 * GQA_R)]
    parts = [take(0, OFF_GATE_A)] + gate_parts + [take(OFF_RWKV + 3 * D_B, OFF_MERGE), zeros(RW_W - SHIFT_W),
                                                   take(OFF_RWKV, OFF_RWKV + 3 * D_B),
                                                   take(OFF_MERGE, OFF_MERGE + 2 * D_MODEL)]
    return jnp.concatenate(parts, axis=axis)


def _rwkv_cols(p):
    return jnp.concatenate([p[..., A_RKV:A_RKV + 3 * D_B], p[..., A_LO:A_LO + SHIFT_W - 3 * D_B]], axis=-1)


def _pad_lanes(v, width):
    return jnp.pad(v, [(0, 0)] * (v.ndim - 1) + [(0, width - v.shape[-1])])


def kernel(x_prompt, x_sample, cache_cmp, cache_slc, cache_win, state_shift, state_wkv, page_table, c_prompt, c_sample, ln1_g, ln2_g, ada_w, ada_b, w_in, cmp_w1, cmp_b1, cmp_w2, rwkv_mu, rwkv_w0, rwkv_w2, rwkv_a0, rwkv_a2, rwkv_g2, rwkv_kk, rwkv_ka, rwkv_rk, rwkv_gn_g, rwkv_gn_b, w_branch, w_out, router_g_w, router_g_b, router_e_w, router_e_b, exp_w1, exp_w3, exp_w2, final_g):
    depth = w_in.shape[0]
    bp, t, d = x_prompt.shape
    bd, ts, _ = x_sample.shape
    assert ts == 1 and t % KC == 0 and t >= WINDOW + QB
    n_pages = page_table.shape[1]
    past = n_pages * PAGE
    wb = cache_win.shape[2]
    n_phys = cache_cmp.shape[1]
    n_p = bp * t

    slopes = jnp.exp2(-8.0 * (jnp.arange(N_HEADS_A, dtype=F32) + 1.0) / N_HEADS_A)
    c_rows = bp + bd
    c16 = jnp.zeros(((c_rows + SUBLANE - 1) // SUBLANE * SUBLANE, d), F32).at[:bp].set(c_prompt).at[bp:c_rows].set(c_sample)
    mod = adaln(c16, ada_w, ada_b).reshape(depth, c16.shape[0], 6, d)

    xp, xs = x_prompt, x_sample
    outs = {k: [] for k in ("cmp_p", "slc_p", "win_p", "shf_p", "wkv_p", "cmp_s", "slc_s", "win_s", "shf_s", "wkv_s")}
    pool_cmp2d = cache_cmp.reshape(-1, HD_A)
    pool_slc2d = cache_slc.reshape(-1, HD_A)
    win2d = cache_win.reshape(-1, HD_A)

    for l in range(depth):
        mp, ms = mod[l, :bp], mod[l, bp:c_rows]
        w_in_al = _align_in_cols(w_in[l], 1).astype(BF16)
        w1b = cmp_w1[l].astype(BF16)
        w_out_b = w_out[l].astype(BF16)
        prm = {
            "mu": _pad_lanes(rwkv_mu[l][None, :], RW_W),
            "w0": rwkv_w0[l][None, :], "w2": rwkv_w2[l], "a0": rwkv_a0[l][None, :], "a2": rwkv_a2[l],
            "g2": rwkv_g2[l], "kk": rwkv_kk[l][None, :], "ka": rwkv_ka[l][None, :],
            "rk": rwkv_rk[l].reshape(1, D_B), "gn_g": rwkv_gn_g[l][None, :], "gn_b": rwkv_gn_b[l][None, :],
        }
        wr = _pad_lanes(jnp.concatenate([router_g_w[l], router_e_w[l]], axis=1), LANE)
        br = _pad_lanes(jnp.concatenate([router_g_b[l], router_e_b[l]])[None, :], LANE)

        hp_ = norm_mod(xp, ln1_g[l], mp[:, 0], mp[:, 1], BF16).reshape(n_p, d)
        pp, ppb, kv_p = in_proj(hp_, w_in_al)
        cmp_kv = compress(pp, A_KV, 1, w1b, cmp_b1[l], cmp_w2[l])
        o_a = nsa_prompt(slopes, ppb, pp, cmp_kv, bp, t)
        o_b, s_fin = rwkv7(pp, jnp.zeros((bp, 1, RW_W), F32), jnp.zeros((bp, N_HEADS_B, HD_B, HD_B), F32),
                           prm, bp, t, 64, 64)
        merged = branch_merge(o_a, o_b, w_branch, l, pp)
        xp = out_proj_residual(merged.reshape(bp, t, d), w_out_b, xp, mp[:, 2])
        kv_p = kv_p.reshape(3, bp, t, 2, N_KV, HD_A)
        outs["cmp_p"].append(kv_p[0])
        outs["slc_p"].append(kv_p[1])
        outs["win_p"].append(kv_p[2, :, t - min(WINDOW, t):])
        outs["shf_p"].append(_rwkv_cols(pp.reshape(bp, t, IN_AL)[:, -1]))
        outs["wkv_p"].append(s_fin)

        hs_ = norm_mod(xs, ln1_g[l], ms[:, 0], ms[:, 1], BF16).reshape(bd, d)
        ps, psb, kv_s = in_proj(hs_, w_in_al)
        cmp_past = compress_paged(page_table, pool_cmp2d, l * n_phys, w1b, cmp_b1[l], cmp_w2[l])
        new_rows = jnp.zeros((bd, BLK, 2 * KV_W), F32).at[:, 0].set(ps[:, A_KV:A_KV + 2 * KV_W])
        cmp_new = compress(new_rows.reshape(bd * BLK, 2 * KV_W), 0, 1, w1b, cmp_b1[l], cmp_w2[l])
        nb_past = past // BLK
        nb_real = nb_past + 1
        nbp = (nb_real + LANE - 1) // LANE * LANE
        cmp_s = jnp.concatenate([cmp_past.reshape(2, N_KV, bd, nb_past, HD_A), cmp_new[:, :, :, None, :],
                                 jnp.zeros((2, N_KV, bd, nbp - nb_real, HD_A), F32)], axis=3)
        ps3 = ps.reshape(bd, 1, IN_AL)
        psb3 = psb.reshape(bd, 1, A_GATE)
        o_cmp, idx_full = nsa_sample_cmp(slopes, psb3, cmp_s, past, nb_real)
        idx_flat = idx_full[:, :, :N_SEL, 0].reshape(-1)
        o_a_s = nsa_sample_sel(idx_flat, page_table, slopes, psb3, ps3, pool_slc2d, l * n_phys,
                               win2d, l * bd, wb, o_cmp, past).reshape(bd, D_A)
        ps_pad = jnp.zeros((bd, SUBLANE, IN_AL), F32).at[:, 0].set(ps).reshape(bd * SUBLANE, IN_AL)
        o_b_s, s_fin_s = rwkv7(ps_pad, _pad_lanes(state_shift[l], RW_W)[:, None, :], state_wkv[l],
                               prm, bd, SUBLANE, SUBLANE, 1)
        o_b_s = o_b_s.reshape(bd, SUBLANE, D_B)[:, 0]
        merged_s = branch_merge(o_a_s, o_b_s, w_branch, l, ps)
        xs = out_proj_residual(merged_s.reshape(bd, 1, d), w_out_b, xs, ms[:, 2])
        kv_s = kv_s.reshape(3, bd, 1, 2, N_KV, HD_A)
        outs["cmp_s"].append(kv_s[0])
        outs["slc_s"].append(kv_s[1])
        outs["win_s"].append(jnp.concatenate([cache_win[l, :, 1:], kv_s[2]], axis=1))
        outs["shf_s"].append(_rwkv_cols(ps))
        outs["wkv_s"].append(s_fin_s)

        h2p = norm_mod(xp, ln2_g[l], mp[:, 3], mp[:, 4], F32).reshape(n_p, d)
        h2s = norm_mod(xs, ln2_g[l], ms[:, 3], ms[:, 4], F32).reshape(bd, d)
        eid_p, wt_p = router(h2p, wr, br)
        eid_s, wt_s = router(h2s, wr, br)
        h_all = jnp.concatenate([h2p, h2s], axis=0)
        eid = jnp.concatenate([eid_p[:, :2], eid_s[:, :2]], axis=0)
        blk_e, n_used, row_tok, dest = _dispatch_tables(eid, MOE_BLOCK)
        ys = experts(h_all, blk_e, n_used, row_tok.reshape(-1, 1, MOE_BLOCK), exp_w1, exp_w3, exp_w2, l, MOE_BLOCK)
        xp = combine(ys, dest[:n_p], wt_p, xp, mp[:, 5])
        xs = combine(ys, dest[n_p:], wt_s, xs, ms[:, 5])

    y_prompt = final_norm(xp, final_g)
    y_sample = final_norm(xs, final_g)
    st = lambda k: jnp.stack(outs[k])
    return (y_prompt, y_sample, st("cmp_p"), st("slc_p"), st("win_p"), st("shf_p"), st("wkv_p"),
            st("cmp_s"), st("slc_s"), st("win_s"), st("shf_s"), st("wkv_s"))
```

```python
import functools

import jax
import jax.numpy as jnp
from jax import lax
from jax.experimental import pallas as pl
from jax.experimental.pallas import tpu as pltpu

F32 = jnp.float32
BF16 = jnp.bfloat16
I32 = jnp.int32

LANE = 128
SUBLANE = 8
VMEM_LIMIT = 56 * 1024 * 1024

D_MODEL = 2048
HD_A = 128
N_HEADS_A = 8
N_KV = 2
GQA_R = 4
BLK = 64
N_SEL = 16
WINDOW = 512
CMP_HID = 256
SCALE_A = HD_A ** -0.5
D_A = 1024
D_B = 1024
HD_B = 64
N_HEADS_B = 16
LORA_W, LORA_A, LORA_G = 64, 64, 32
GN_EPS = HD_B * 1e-5
N_GROUPS = 4
E_PER_GROUP = 8
N_EXPERTS = 32
D_EXPERT = 512
MOE_BLOCK = 128
RMS_EPS = 1e-6
PAGE = 128

Q_W = N_HEADS_A * HD_A
KV_W = N_KV * HD_A
OFF_KV = Q_W
OFF_GATE_A = OFF_KV + 6 * KV_W
OFF_RWKV = OFF_GATE_A + 3 * N_HEADS_A
SHIFT_W = 3 * D_B + LORA_W + LORA_A + LORA_G
OFF_MERGE = OFF_RWKV + SHIFT_W

A_KV = Q_W
A_GATE = A_KV + 6 * KV_W
A_LO = A_GATE + N_KV * LANE
A_RKV = A_LO + 2 * LANE
RW_W = 3 * D_B + 2 * LANE
A_MERGE = A_RKV + 3 * D_B
IN_AL = A_MERGE + 2 * D_MODEL
NEG = -1e30


def _cparams(sem):
    return pltpu.CompilerParams(dimension_semantics=sem, vmem_limit_bytes=VMEM_LIMIT)


def _dot(a, b, dims=(((1,), (0,)), ((), ()))):
    return lax.dot_general(a.astype(BF16), b.astype(BF16), dims, preferred_element_type=F32)


def _dot_nt(a, b):
    return _dot(a, b, (((1,), (1,)), ((), ())))


def _dot_tn(a, b):
    return _dot(a, b, (((0,), (0,)), ((), ())))


def _split3(x):
    h = x.astype(BF16)
    r1 = x - h.astype(F32)
    m = r1.astype(BF16)
    lo = (r1 - m.astype(F32)).astype(BF16)
    return h, m, lo


def _dot3(a, b, dims=(((1,), (0,)), ((), ()))):
    ah, am, al = _split3(a)
    bh, bm, bl = _split3(b)
    d = lambda x, y: lax.dot_general(x, y, dims, preferred_element_type=F32)
    return (d(ah, bh) + (d(ah, bm) + d(am, bh))) + ((d(am, bm) + d(ah, bl)) + d(al, bh))


def _sigmoid(x):
    return 1.0 / (1.0 + jnp.exp(-x))


def _softplus(x):
    return jnp.maximum(x, 0.0) + jnp.log(1.0 + jnp.exp(-jnp.abs(x)))


def _adaln_kernel(c_ref, w_ref, b_ref, o_ref):
    c = c_ref[...]
    h = c * _sigmoid(c)
    o_ref[...] = _dot(h, w_ref[...]) + b_ref[...]


def adaln(c16, ada_w, ada_b):
    depth, d, n = ada_w.shape
    tn = 1024
    return pl.pallas_call(
        _adaln_kernel,
        out_shape=jax.ShapeDtypeStruct((depth, c16.shape[0], n), F32),
        grid=(depth, n // tn),
        in_specs=[pl.BlockSpec(c16.shape, lambda l, j: (0, 0)),
                  pl.BlockSpec((None, d, tn), lambda l, j: (l, 0, j)),
                  pl.BlockSpec((None, 1, tn), lambda l, j: (l, 0, j))],
        out_specs=pl.BlockSpec((None, c16.shape[0], tn), lambda l, j: (l, 0, j)),
        compiler_params=_cparams(("parallel", "parallel")),
        name="adaln",
    )(c16, ada_w, ada_b.reshape(depth, 1, n))


def _norm_kernel(x_ref, g_ref, sh_ref, sc_ref, o_ref):
    x = x_ref[...]
    y = x * lax.rsqrt(jnp.mean(x * x, axis=-1, keepdims=True) + RMS_EPS)
    y = y * g_ref[...]
    o_ref[...] = (y * (1.0 + sc_ref[...]) + sh_ref[...]).astype(o_ref.dtype)


def norm_mod(x, g, shift, scale, out_dtype):
    b, t, d = x.shape
    tr = min(t, 256)
    return pl.pallas_call(
        _norm_kernel,
        out_shape=jax.ShapeDtypeStruct((b, t, d), out_dtype),
        grid=(b, t // tr),
        in_specs=[pl.BlockSpec((None, tr, d), lambda i, j: (i, j, 0)),
                  pl.BlockSpec((1, d), lambda i, j: (0, 0)),
                  pl.BlockSpec((None, 1, d), lambda i, j: (i, 0, 0)),
                  pl.BlockSpec((None, 1, d), lambda i, j: (i, 0, 0))],
        out_specs=pl.BlockSpec((None, tr, d), lambda i, j: (i, j, 0)),
        compiler_params=_cparams(("parallel", "parallel")),
        name="norm_mod",
    )(x, g.reshape(1, d), shift.reshape(b, 1, d), scale.reshape(b, 1, d))


IN_TN = 2 * KV_W
N_BF16_TILES = A_GATE // IN_TN
KV_TILE0 = A_KV // IN_TN
KV_ROWS = 2 * N_KV


def _in_proj_kernel(a_ref, w_ref, o_ref, ob_ref, kv_ref):
    j = pl.program_id(1)
    acc = _dot(a_ref[...], w_ref[...])
    o_ref[...] = acc

    @pl.when(j < N_BF16_TILES)
    def _():
        ob_ref[...] = acc.astype(BF16)

    @pl.when((j >= KV_TILE0) & (j < N_BF16_TILES))
    def _():
        tm = acc.shape[0]
        for c4 in range(KV_ROWS):
            kv_ref[pl.ds(c4, tm, stride=KV_ROWS), :] = acc[:, c4 * HD_A:(c4 + 1) * HD_A]


def _row_tile(m, cap):
    return m if m <= cap else cap


def in_proj(a, w):
    m, k = a.shape
    tm = _row_tile(m, 1024)
    tn = IN_TN
    assert m % tm == 0
    return pl.pallas_call(
        _in_proj_kernel,
        out_shape=[jax.ShapeDtypeStruct((m, IN_AL), F32), jax.ShapeDtypeStruct((m, A_GATE), BF16),
                   jax.ShapeDtypeStruct((3, m * KV_ROWS, HD_A), F32)],
        grid=(m // tm, IN_AL // tn),
        in_specs=[pl.BlockSpec((tm, k), lambda i, j: (i, 0)),
                  pl.BlockSpec((k, tn), lambda i, j: (0, j))],
        out_specs=[pl.BlockSpec((tm, tn), lambda i, j: (i, j)),
                   pl.BlockSpec((tm, tn), lambda i, j: (i, jnp.minimum(j, N_BF16_TILES - 1))),
                   pl.BlockSpec((None, tm * KV_ROWS, HD_A),
                                lambda i, j: (jnp.clip(j - KV_TILE0, 0, 2), i, 0))],
        compiler_params=_cparams(("parallel", "arbitrary")),
        name="proj_in",
    )(a, w)


def _branch_kernel(oa_ref, ob_ref, wa_ref, wb_ref, ga_ref, gb_ref, o_ref):
    ya = _dot(oa_ref[...], wa_ref[...])
    yb = _dot(ob_ref[...], wb_ref[...])
    o_ref[...] = (_sigmoid(ga_ref[...]) * ya + _sigmoid(gb_ref[...]) * yb).astype(o_ref.dtype)


def branch_merge(o_a, o_b, w_branch, l, pp):
    m = o_a.shape[0]
    tm = _row_tile(m, 1024)
    tn = 512
    gcol = A_MERGE // tn
    return pl.pallas_call(
        _branch_kernel,
        out_shape=jax.ShapeDtypeStruct((m, D_MODEL), BF16),
        grid=(m // tm, D_MODEL // tn),
        in_specs=[pl.BlockSpec((tm, D_A), lambda i, j: (i, 0)),
                  pl.BlockSpec((tm, D_B), lambda i, j: (i, 0)),
                  pl.BlockSpec((None, D_A, tn), lambda i, j: (l, 0, j)),
                  pl.BlockSpec((None, D_B, tn), lambda i, j: (l, D_A // D_B, j)),
                  pl.BlockSpec((tm, tn), lambda i, j: (i, gcol + j)),
                  pl.BlockSpec((tm, tn), lambda i, j: (i, gcol + D_MODEL // tn + j))],
        out_specs=pl.BlockSpec((tm, tn), lambda i, j: (i, j)),
        compiler_params=_cparams(("parallel", "parallel")),
        name="branch_merge",
    )(o_a, o_b, w_branch, w_branch, pp, pp)


def _resid_kernel(m_ref, w_ref, x_ref, g_ref, o_ref):
    o_ref[...] = x_ref[...] + g_ref[...] * _dot(m_ref[...], w_ref[...])


def out_proj_residual(merged, w_out_l, x, gate):
    b, t, d = x.shape
    tm = _row_tile(t, 1024)
    tn = 512
    nt = t // tm
    return pl.pallas_call(
        _resid_kernel,
        out_shape=jax.ShapeDtypeStruct((b, t, d), F32),
        grid=(b, nt, d // tn),
        in_specs=[pl.BlockSpec((None, tm, d), lambda i, r, j: (i, r, 0)),
                  pl.BlockSpec((d, tn), lambda i, r, j: (0, j)),
                  pl.BlockSpec((None, tm, tn), lambda i, r, j: (i, r, j)),
                  pl.BlockSpec((None, 1, tn), lambda i, r, j: (i, 0, j))],
        out_specs=pl.BlockSpec((None, tm, tn), lambda i, r, j: (i, r, j)),
        compiler_params=_cparams(("parallel", "parallel", "parallel")),
        name="out_proj",
    )(merged, w_out_l, x, gate.reshape(b, 1, d))


T_PER_STEP = 8


def _gelu_tanh(x):
    return 0.5 * x * (1.0 + jnp.tanh(0.7978845608028654 * (x + 0.044715 * x * x * x)))


def _compress_kernel(x_ref, w1_ref, b1_ref, w2_ref, o_ref, acc_ref):
    tc = pl.program_id(3)
    nblk = o_ref.shape[0]

    @pl.when(tc == 0)
    def _():
        acc_ref[...] = jnp.zeros_like(acc_ref)

    acc = acc_ref[...]
    for tl in range(T_PER_STEP):
        t = tc * T_PER_STEP + tl
        xt = x_ref[pl.ds(t, nblk, stride=BLK), :]
        acc = acc + _dot(xt, w1_ref[tl * HD_A:(tl + 1) * HD_A, :])
    acc_ref[...] = acc

    @pl.when(tc == pl.num_programs(3) - 1)
    def _():
        h = _gelu_tanh(acc + b1_ref[...])
        o_ref[...] = _dot(h, w2_ref[...])


def compress(rows, col0, n_slabs, w1b, b1, w2):
    r = rows.shape[0] // n_slabs
    nblk = r // BLK
    cb = col0 // HD_A
    return pl.pallas_call(
        _compress_kernel,
        out_shape=jax.ShapeDtypeStruct((2, N_KV, n_slabs * nblk, HD_A), F32),
        grid=(2, N_KV, n_slabs, BLK // T_PER_STEP),
        in_specs=[pl.BlockSpec((r, HD_A), lambda kv, g, s, tc: (s, cb + kv * N_KV + g)),
                  pl.BlockSpec((None, T_PER_STEP * HD_A, CMP_HID), lambda kv, g, s, tc: (kv, tc, 0)),
                  pl.BlockSpec((None, 1, CMP_HID), lambda kv, g, s, tc: (kv, 0, 0)),
                  pl.BlockSpec((None, CMP_HID, HD_A), lambda kv, g, s, tc: (kv, 0, 0))],
        out_specs=pl.BlockSpec((None, None, nblk, HD_A), lambda kv, g, s, tc: (kv, g, s, 0)),
        scratch_shapes=[pltpu.VMEM((nblk, CMP_HID), F32)],
        compiler_params=_cparams(("parallel", "parallel", "parallel", "arbitrary")),
        name="compress",
    )(rows, w1b, b1.reshape(2, 1, CMP_HID), w2)


QB = 128
KC = 512


def _softmax_cols(s, valid):
    sm = jnp.where(valid, s, NEG)
    m = jnp.max(sm, axis=0, keepdims=True)
    e = jnp.where(valid, jnp.exp(sm - m), 0.0)
    return e, jnp.sum(e, axis=0, keepdims=True)


def _safe(den):
    return jnp.where(den > 0, den, 1.0)


def _select_mask_t(imp, jblk, cur):
    nb = imp.shape[0]
    score = jnp.where(jblk < cur, imp, jnp.where(jblk == cur, GQA_R + 1.0, -1.0))
    rank = jnp.zeros(imp.shape, I32)
    for i in range(nb):
        row = score[i:i + 1, :]
        beats = (row > score) | ((row == score) & (jblk > i))
        rank = rank + beats.astype(I32)
    return (rank < N_SEL) & (jblk <= cur)


def _nsa_prompt_kernel(slopes_ref, q_ref, kc_ref, vc_ref, ks_ref, vs_ref, kw_ref, vw_ref, gate_ref,
                       o_ref, bias_ref, acc_ref):
    g = pl.program_id(1)
    i = pl.program_id(2)
    t_len = ks_ref.shape[0]
    nb = t_len // BLK
    q0 = i * QB
    span = WINDOW + QB

    jblk = lax.broadcasted_iota(I32, (nb, QB), 0)
    qpos_l = q0 + lax.broadcasted_iota(I32, (nb, QB), 1)
    dist_c = qpos_l - ((jblk + 1) * BLK - 1)
    valid_c = dist_c >= 0
    dist_cf = dist_c.astype(F32)
    kc = kc_ref[...]
    vc = vc_ref[...]

    qs = [q_ref[:, r * HD_A:(r + 1) * HD_A] for r in range(GQA_R)]
    slopes = [slopes_ref[g * GQA_R + r] for r in range(GQA_R)]

    imp = jnp.zeros((nb, QB), F32)
    o_cmp = []
    for r in range(GQA_R):
        s = _dot_nt(kc, qs[r]) * SCALE_A - slopes[r] * dist_cf
        e, den = _softmax_cols(s, valid_c)
        p = e / _safe(den)
        imp = imp + p
        o_cmp.append(_dot_tn(p, vc))

    sel = _select_mask_t(imp, jblk, qpos_l // BLK)
    expand = (lax.broadcasted_iota(I32, (nb, t_len), 1) // BLK
              == lax.broadcasted_iota(I32, (nb, t_len), 0))
    key_sel = _dot_tn(sel.astype(F32), expand.astype(F32))
    row_q = q0 + lax.broadcasted_iota(I32, (QB, KC), 0)
    col_k = lax.broadcasted_iota(I32, (QB, KC), 1)
    for cc in range(t_len // KC):
        @pl.when(cc * KC < q0 + QB)
        def _(cc=cc):
            ok = (key_sel[:, cc * KC:(cc + 1) * KC] > 0.5) & (row_q >= cc * KC + col_k)
            bias_ref[cc] = jnp.where(ok, 0.0, NEG)

    n_chunks = (q0 + QB + KC - 1) // KC
    kpos0 = lax.broadcasted_iota(I32, (1, KC), 1).astype(F32)
    acc_ref[...] = jnp.zeros_like(acc_ref)

    def body(c, carry):
        ms, ls = carry
        k0 = pl.multiple_of(c * KC, KC)
        kk = ks_ref[pl.ds(k0, KC), :]
        vv = vs_ref[pl.ds(k0, KC), :]
        bias = bias_ref[c]
        kpos = kpos0 + k0.astype(F32)
        ss = [_dot_nt(qs[r], kk) * SCALE_A + (bias + slopes[r] * kpos) for r in range(GQA_R)]
        m_new = [jnp.maximum(ms[r], jnp.max(ss[r], axis=-1, keepdims=True)) for r in range(GQA_R)]
        alpha = [jnp.exp(ms[r] - m_new[r]) for r in range(GQA_R)]
        es = [jnp.exp(ss[r] - m_new[r]) for r in range(GQA_R)]
        l_new = [alpha[r] * ls[r] + jnp.sum(es[r], axis=-1, keepdims=True) for r in range(GQA_R)]
        pv = [_dot(es[r], vv) for r in range(GQA_R)]
        for r in range(GQA_R):
            acc_ref[r] = alpha[r] * acc_ref[r] + pv[r]
        return tuple(m_new), tuple(l_new)

    init = (tuple(jnp.full((QB, 1), NEG, F32) for _ in range(GQA_R)),
            tuple(jnp.zeros((QB, 1), F32) for _ in range(GQA_R)))
    _, l_sel = lax.fori_loop(0, n_chunks, body, init)

    start = pl.multiple_of(jnp.maximum(i - WINDOW // QB, 0) * QB, QB)
    kw = kw_ref[pl.ds(start, span), :]
    vw = vw_ref[pl.ds(start, span), :]
    dist_w = (q0 + lax.broadcasted_iota(I32, (QB, span), 0)) - (start + lax.broadcasted_iota(I32, (QB, span), 1))
    bias_w = jnp.where((dist_w >= 0) & (dist_w < WINDOW), 0.0, NEG)
    kpos_w = (start + lax.broadcasted_iota(I32, (1, span), 1)).astype(F32)
    gates = _sigmoid(gate_ref[...])
    sw = [_dot_nt(qs[r], kw) * SCALE_A + (bias_w + slopes[r] * kpos_w) for r in range(GQA_R)]
    ew = [jnp.exp(sw[r] - jnp.max(sw[r], axis=-1, keepdims=True)) for r in range(GQA_R)]
    den_w = [jnp.sum(ew[r], axis=-1, keepdims=True) for r in range(GQA_R)]
    pv_w = [_dot(ew[r], vw) for r in range(GQA_R)]
    for r in range(GQA_R):
        o = (gates[:, 3 * r:3 * r + 1] * o_cmp[r] + gates[:, 3 * r + 1:3 * r + 2] * (acc_ref[r] / l_sel[r])
             + gates[:, 3 * r + 2:3 * r + 3] * (pv_w[r] / den_w[r]))
        o_ref[:, r * HD_A:(r + 1) * HD_A] = o.astype(o_ref.dtype)


def nsa_prompt(slopes, ppb, pp, cmp_kv, b, t):
    nq = t // QB
    nb = t // BLK
    kvb = A_KV // HD_A

    def kv_spec(branch, kv):
        return pl.BlockSpec((t, HD_A), lambda bi, g, i, s: (bi, kvb + branch * 4 + kv * 2 + g))

    grid_spec = pltpu.PrefetchScalarGridSpec(
        num_scalar_prefetch=1,
        grid=(b, N_KV, nq),
        in_specs=[pl.BlockSpec((QB, GQA_R * HD_A), lambda bi, g, i, s: (bi * nq + i, g)),
                  pl.BlockSpec((None, None, nb, HD_A), lambda bi, g, i, s: (0, g, bi, 0)),
                  pl.BlockSpec((None, None, nb, HD_A), lambda bi, g, i, s: (1, g, bi, 0)),
                  kv_spec(1, 0), kv_spec(1, 1), kv_spec(2, 0), kv_spec(2, 1),
                  pl.BlockSpec((QB, LANE), lambda bi, g, i, s: (bi * nq + i, A_GATE // LANE + g))],
        out_specs=pl.BlockSpec((QB, GQA_R * HD_A), lambda bi, g, i, s: (bi * nq + i, g)),
        scratch_shapes=[pltpu.VMEM((t // KC, QB, KC), F32), pltpu.VMEM((GQA_R, QB, HD_A), F32)],
    )
    return pl.pallas_call(
        _nsa_prompt_kernel,
        out_shape=jax.ShapeDtypeStruct((b * t, D_A), BF16),
        grid_spec=grid_spec,
        compiler_params=_cparams(("parallel", "parallel", "arbitrary")),
        name="nsa_prompt",
    )(slopes, ppb, cmp_kv, cmp_kv, ppb, ppb, ppb, ppb, pp)


PAGE_ROWS = PAGE * KV_ROWS
SLAB_PAGES = 64


def _compress_paged_kernel(pt_ref, pool_hbm, w1_ref, b1_ref, w2_ref, o_ref, slab_ref, acc_ref, sem, *, page0):
    s = pl.program_id(0)
    tc = pl.program_id(1)
    n_slabs = pl.num_programs(0)
    nblk = SLAB_PAGES * (PAGE // BLK)

    def page_copy(slab, p, slot):
        phys = pt_ref[slab * SLAB_PAGES + p]
        return pltpu.make_async_copy(pool_hbm.at[pl.ds((page0 + phys) * PAGE_ROWS, PAGE_ROWS), :],
                                     slab_ref.at[slot, pl.ds(p * PAGE_ROWS, PAGE_ROWS), :], sem.at[slot])

    def start_slab(slab, slot):
        lax.fori_loop(0, SLAB_PAGES, lambda p, c: (page_copy(slab, p, slot).start(), c)[1], 0)

    def wait_slab(slab, slot):
        lax.fori_loop(0, SLAB_PAGES, lambda p, c: (page_copy(slab, p, slot).wait(), c)[1], 0)

    slot = s % 2

    @pl.when(tc == 0)
    def _():
        @pl.when(s == 0)
        def _():
            start_slab(0, 0)

        wait_slab(s, slot)

        @pl.when(s + 1 < n_slabs)
        def _():
            start_slab(s + 1, 1 - slot)

        acc_ref[...] = jnp.zeros_like(acc_ref)

    for tl in range(T_PER_STEP):
        t = tc * T_PER_STEP + tl
        for kv in range(2):
            w1t = w1_ref[kv, tl * HD_A:(tl + 1) * HD_A, :]
            for g in range(N_KV):
                c4 = kv * N_KV + g
                xt = slab_ref[slot, pl.ds(t * KV_ROWS + c4, nblk, stride=BLK * KV_ROWS), :]
                acc_ref[c4] += _dot(xt, w1t)

    @pl.when(tc == pl.num_programs(1) - 1)
    def _():
        for kv in range(2):
            for g in range(N_KV):
                h = _gelu_tanh(acc_ref[kv * N_KV + g] + b1_ref[kv])
                o_ref[kv, g] = _dot(h, w2_ref[kv])


def compress_paged(page_table, pool2d, page0, w1b, b1, w2):
    bd, n_pages = page_table.shape
    assert n_pages % SLAB_PAGES == 0
    n_slabs = bd * n_pages // SLAB_PAGES
    nblk = SLAB_PAGES * (PAGE // BLK)
    grid_spec = pltpu.PrefetchScalarGridSpec(
        num_scalar_prefetch=1,
        grid=(n_slabs, BLK // T_PER_STEP),
        in_specs=[pl.BlockSpec(memory_space=pl.ANY),
                  pl.BlockSpec((2, T_PER_STEP * HD_A, CMP_HID), lambda s, tc, pt: (0, tc, 0)),
                  pl.BlockSpec((2, 1, CMP_HID), lambda s, tc, pt: (0, 0, 0)),
                  pl.BlockSpec((2, CMP_HID, HD_A), lambda s, tc, pt: (0, 0, 0))],
        out_specs=pl.BlockSpec((2, N_KV, nblk, HD_A), lambda s, tc, pt: (0, 0, s, 0)),
        scratch_shapes=[pltpu.VMEM((2, SLAB_PAGES * PAGE_ROWS, HD_A), F32),
                        pltpu.VMEM((2 * N_KV, nblk, CMP_HID), F32),
                        pltpu.SemaphoreType.DMA((2,))],
    )
    return pl.pallas_call(
        functools.partial(_compress_paged_kernel, page0=page0),
        out_shape=jax.ShapeDtypeStruct((2, N_KV, n_slabs * nblk, HD_A), F32),
        grid_spec=grid_spec,
        compiler_params=_cparams(("arbitrary", "arbitrary")),
        name="compress_paged",
    )(page_table.reshape(-1), pool2d, w1b, b1.reshape(2, 1, CMP_HID), w2)


SEL_PAD = 128


def _heads_on_sublanes(q_ref):
    q = q_ref[...].astype(F32)
    row = lax.broadcasted_iota(I32, (SUBLANE, HD_A), 0)
    q8 = jnp.zeros((SUBLANE, HD_A), F32)
    for r in range(GQA_R):
        q8 = jnp.where(row == r, jnp.broadcast_to(q[:, r * HD_A:(r + 1) * HD_A], (SUBLANE, HD_A)), q8)
    return q8


def _slopes_on_sublanes(slopes_ref, g):
    row = lax.broadcasted_iota(I32, (SUBLANE, 1), 0)
    slope = jnp.zeros((SUBLANE, 1), F32)
    for r in range(GQA_R):
        slope = jnp.where(row == r, slopes_ref[g * GQA_R + r], slope)
    return slope


def _softmax_rows(s, valid):
    sm = jnp.where(valid, s, NEG)
    m = jnp.max(sm, axis=-1, keepdims=True)
    e = jnp.where(valid, jnp.exp(sm - m), 0.0)
    return e, jnp.sum(e, axis=-1, keepdims=True)


def _nsa_sample_cmp_kernel(slopes_ref, q_ref, kc_ref, vc_ref, o_ref, idx_ref, *, past, nb_real):
    g = pl.program_id(1)
    nbp = kc_ref.shape[0]
    kc = kc_ref[...]
    vc = vc_ref[...]
    q8 = _heads_on_sublanes(q_ref)
    slope = _slopes_on_sublanes(slopes_ref, g)
    head = lax.broadcasted_iota(I32, (SUBLANE, 1), 0) < GQA_R
    jrow = lax.broadcasted_iota(I32, (1, nbp), 1)
    dist = past - ((jrow + 1) * BLK - 1)
    valid = (dist >= 0) & (jrow < nb_real)
    s = _dot_nt(q8, kc) * SCALE_A - slope * dist.astype(F32)
    e, den = _softmax_rows(s, valid)
    p = e / _safe(den)
    o = _dot(p, vc)
    for r in range(GQA_R):
        o_ref[:, r * HD_A:(r + 1) * HD_A] = o[r:r + 1, :]

    imp = jnp.sum(jnp.where(head, p, 0.0), axis=0, keepdims=True)
    cur = past // BLK
    score_r = jnp.where(jrow < cur, imp, jnp.where(jrow == cur, GQA_R + 1.0, -1.0))
    score_r = jnp.where(jrow < nb_real, score_r, -2.0)
    ii = lax.broadcasted_iota(I32, (nbp, nbp), 0)
    jj = lax.broadcasted_iota(I32, (nbp, nbp), 1)
    sc_b = jnp.broadcast_to(score_r, (nbp, nbp))
    score_c = jnp.sum(jnp.where(ii == jj, sc_b, 0.0), axis=1, keepdims=True)
    beats = (score_c > sc_b) | ((score_c == sc_b) & (ii < jj))
    rank_r = jnp.sum(beats.astype(F32), axis=0, keepdims=True)
    nn = lax.broadcasted_iota(I32, (SEL_PAD, nbp), 0).astype(F32)
    jn = lax.broadcasted_iota(I32, (SEL_PAD, nbp), 1).astype(F32)
    hit = jnp.broadcast_to(rank_r, (SEL_PAD, nbp)) == nn
    idx_c = jnp.sum(jnp.where(hit, jn, 0.0), axis=1, keepdims=True)
    idx_ref[...] = jnp.broadcast_to(idx_c, (SEL_PAD, LANE)).astype(I32)


def nsa_sample_cmp(slopes, ppb3, cmp_kv_s, past, nb_real):
    bd = ppb3.shape[0]
    nbp = cmp_kv_s.shape[3]
    grid_spec = pltpu.PrefetchScalarGridSpec(
        num_scalar_prefetch=1,
        grid=(bd, N_KV),
        in_specs=[pl.BlockSpec((None, 1, GQA_R * HD_A), lambda bi, g, s: (bi, 0, g)),
                  pl.BlockSpec((None, None, None, nbp, HD_A), lambda bi, g, s: (0, g, bi, 0, 0)),
                  pl.BlockSpec((None, None, None, nbp, HD_A), lambda bi, g, s: (1, g, bi, 0, 0))],
        out_specs=[pl.BlockSpec((None, 1, GQA_R * HD_A), lambda bi, g, s: (bi, 0, g)),
                   pl.BlockSpec((None, None, SEL_PAD, LANE), lambda bi, g, s: (bi, g, 0, 0))],
    )
    return pl.pallas_call(
        functools.partial(_nsa_sample_cmp_kernel, past=past, nb_real=nb_real),
        out_shape=[jax.ShapeDtypeStruct((bd, 1, D_A), F32),
                   jax.ShapeDtypeStruct((bd, N_KV, SEL_PAD, LANE), I32)],
        grid_spec=grid_spec,
        compiler_params=_cparams(("parallel", "parallel")),
        name="nsa_sample_cmp",
    )(slopes, ppb3, cmp_kv_s, cmp_kv_s)


W_PAD = LANE


def _nsa_sample_sel_kernel(idx_ref, pt_ref, slopes_ref, q_ref, pool_ref, kn_ref, vn_ref,
                           wbuf_ref, kwn_ref, vwn_ref, gate_ref, oc_ref, o_ref,
                           ksel_ref, vsel_ref, kwin_ref, vwin_ref, *, past):
    bi = pl.program_id(0)
    g = pl.program_id(1)
    n = pl.program_id(2)
    nb_past = past // BLK
    base = (bi * N_KV + g) * N_SEL
    idx_n = idx_ref[base + n]
    r0 = pl.multiple_of(n * BLK, BLK)

    @pl.when(idx_n < nb_past)
    def _():
        ksel_ref[pl.ds(r0, BLK), :] = pool_ref[pl.ds(g, BLK, stride=KV_ROWS), :]
        vsel_ref[pl.ds(r0, BLK), :] = pool_ref[pl.ds(N_KV + g, BLK, stride=KV_ROWS), :]

    @pl.when(idx_n >= nb_past)
    def _():
        first = lax.broadcasted_iota(I32, (BLK, HD_A), 0) == 0
        ksel_ref[pl.ds(r0, BLK), :] = jnp.where(first, kn_ref[...], 0.0)
        vsel_ref[pl.ds(r0, BLK), :] = jnp.where(first, vn_ref[...], 0.0)

    @pl.when(n == N_SEL - 1)
    def _():
        q8 = _heads_on_sublanes(q_ref)
        slope = _slopes_on_sublanes(slopes_ref, g)
        cur = past // BLK
        lane = lax.broadcasted_iota(I32, (1, N_SEL * BLK), 1)
        slot = lane // BLK
        idx_row = jnp.zeros((1, N_SEL * BLK), I32)
        for m in range(N_SEL):
            idx_row = jnp.where(slot == m, idx_ref[base + m], idx_row)
        dist = past - (idx_row * BLK + lane % BLK)
        valid = (idx_row <= cur) & (dist >= 0)
        s = _dot_nt(q8, ksel_ref[...]) * SCALE_A - slope * dist.astype(F32)
        e, den = _softmax_rows(s, valid)
        o_sel = _dot(e / _safe(den), vsel_ref[...])
        wb = wbuf_ref.shape[0] // KV_ROWS
        kwin_ref[0:wb, :] = wbuf_ref[pl.ds(g, wb, stride=KV_ROWS), :]
        vwin_ref[0:wb, :] = wbuf_ref[pl.ds(N_KV + g, wb, stride=KV_ROWS), :]
        first = lax.broadcasted_iota(I32, (W_PAD, HD_A), 0) == 0
        kwin_ref[wb:wb + W_PAD, :] = jnp.where(first, kwn_ref[...], 0.0)
        vwin_ref[wb:wb + W_PAD, :] = jnp.where(first, vwn_ref[...], 0.0)
        lane_w = lax.broadcasted_iota(I32, (1, wb + W_PAD), 1)
        dist_w = wb - lane_w
        valid_w = (dist_w >= 0) & (dist_w < WINDOW)
        s = _dot_nt(q8, kwin_ref[...]) * SCALE_A - slope * dist_w.astype(F32)
        e, den = _softmax_rows(s, valid_w)
        o_win = _dot(e / _safe(den), vwin_ref[...])
        gates = _sigmoid(gate_ref[...])
        for r in range(GQA_R):
            o = (gates[:, 3 * r:3 * r + 1] * oc_ref[:, r * HD_A:(r + 1) * HD_A]
                 + gates[:, 3 * r + 1:3 * r + 2] * o_sel[r:r + 1, :]
                 + gates[:, 3 * r + 2:3 * r + 3] * o_win[r:r + 1, :])
            o_ref[:, r * HD_A:(r + 1) * HD_A] = o.astype(o_ref.dtype)


def nsa_sample_sel(idx_flat, page_table, slopes, ppb3, pp3, pool2d, page0, win2d, seq0, wb, o_cmp, past):
    bd = ppb3.shape[0]
    n_pages = page_table.shape[1]
    bpp = PAGE // BLK
    nb_past = past // BLK
    kvb = A_KV // HD_A

    def pool_map(bi, g, n, idx, pt, s):
        ip = jnp.clip(idx[(bi * N_KV + g) * N_SEL + n], 0, nb_past - 1)
        phys = pt[bi * n_pages + ip // bpp]
        return ((page0 + phys) * bpp + ip % bpp, 0)

    def new_spec(branch, kv):
        return pl.BlockSpec((None, 1, HD_A), lambda bi, g, n, idx, pt, s: (bi, 0, kvb + branch * 4 + kv * 2 + g))

    grid_spec = pltpu.PrefetchScalarGridSpec(
        num_scalar_prefetch=3,
        grid=(bd, N_KV, N_SEL),
        in_specs=[pl.BlockSpec((None, 1, GQA_R * HD_A), lambda bi, g, n, idx, pt, s: (bi, 0, g)),
                  pl.BlockSpec((BLK * KV_ROWS, HD_A), pool_map), new_spec(1, 0), new_spec(1, 1),
                  pl.BlockSpec((wb * KV_ROWS, HD_A), lambda bi, g, n, idx, pt, s: (seq0 + bi, 0)),
                  new_spec(2, 0), new_spec(2, 1),
                  pl.BlockSpec((None, 1, LANE), lambda bi, g, n, idx, pt, s: (bi, 0, A_GATE // LANE + g)),
                  pl.BlockSpec((None, 1, GQA_R * HD_A), lambda bi, g, n, idx, pt, s: (bi, 0, g))],
        out_specs=pl.BlockSpec((None, 1, GQA_R * HD_A), lambda bi, g, n, idx, pt, s: (bi, 0, g)),
        scratch_shapes=[pltpu.VMEM((N_SEL * BLK, HD_A), F32), pltpu.VMEM((N_SEL * BLK, HD_A), F32),
                        pltpu.VMEM((wb + W_PAD, HD_A), F32), pltpu.VMEM((wb + W_PAD, HD_A), F32)],
    )
    return pl.pallas_call(
        functools.partial(_nsa_sample_sel_kernel, past=past),
        out_shape=jax.ShapeDtypeStruct((bd, 1, D_A), BF16),
        grid_spec=grid_spec,
        compiler_params=_cparams(("parallel", "parallel", "arbitrary")),
        name="nsa_sample_sel",
    )(idx_flat, page_table.reshape(-1), slopes, ppb3, pool2d, pp3, pp3, win2d, pp3, pp3, pp3, o_cmp)


def _cumsum_rows(x):
    c = x.shape[0]
    row = lax.broadcasted_iota(I32, x.shape, 0)
    sh = 1
    while sh < c:
        x = x + jnp.where(row >= sh, pltpu.roll(x, sh, 0), 0.0)
        sh *= 2
    return x


def _split2(x):
    hi = x.astype(BF16)
    return hi, (x - hi.astype(F32)).astype(BF16)


def _dot2(a, b, dims=(((1,), (0,)), ((), ()))):
    ah, al = _split2(a)
    bh, bl = _split2(b)
    d = lambda x, y: lax.dot_general(x, y, dims, preferred_element_type=F32)
    return d(ah, bh) + (d(ah, bl) + d(al, bh))


_dot_t = _dot2
NT = (((1,), (1,)), ((), ()))
TN = (((0,), (0,)), ((), ()))
CHUNK_LB = 8
STATE_GROUP = 4


def _rwkv_chunk_kernel(pr_ref, pk_ref, pv_ref, pl_ref, qr_ref, qk_ref, qv_ref, ql_ref,
                       sr_ref, sk_ref, sv_ref, sl_ref,
                       mur_ref, muk_ref, muv_ref, mul_ref, w0_ref, w2_ref, a0_ref, a2_ref, g2_ref,
                       kkp_ref, kap_ref, rkp_ref, gnb_ref,
                       y0_ref, rt_ref, bonus_ref, gate_ref, mm_ref, g0_ref, *, n_valid):
    ci = pl.program_id(2)
    c = pr_ref.shape[0]
    row = lax.broadcasted_iota(I32, (c, 1), 0)
    first_chunk = ci == 0

    def mix(p_ref, q_ref, s_ref, mu_ref):
        p = p_ref[...]
        prev = jnp.where(first_chunk, s_ref[...], q_ref[SUBLANE - 1:SUBLANE, :])
        shifted = jnp.where(row == 0, prev, pltpu.roll(p, 1, 0))
        return p + mu_ref[...] * (shifted - p)

    xr = mix(pr_ref, qr_ref, sr_ref, mur_ref)
    xk = mix(pk_ref, qk_ref, sk_ref, muk_ref)
    xv = mix(pv_ref, qv_ref, sv_ref, muv_ref)
    xl = mix(pl_ref, ql_ref, sl_ref, mul_ref)

    dw = xl[:, 0:LORA_W]
    da = xl[:, LORA_W:LORA_W + LORA_A]
    dg = xl[:, LORA_W + LORA_A:LORA_W + LORA_A + LORA_G]
    wlog = -_softplus(-(w0_ref[...] + _dot(jnp.tanh(dw), w2_ref[...]))) - 0.5
    logdec = -jnp.exp(wlog)
    a = _sigmoid(a0_ref[...] + _dot(da, a2_ref[...]))
    gate_ref[...] = _dot(_sigmoid(dg), g2_ref[...])
    kkv = xk * kkp_ref[...]
    kmod = xk * (1.0 + (a - 1.0) * kap_ref[...])
    if n_valid < c:
        live = row < n_valid
        logdec = jnp.where(live, logdec, 0.0)
        kmod = jnp.where(live, kmod, 0.0)
        a = jnp.where(live, a, 0.0)
        xv = jnp.where(live, xv, 0.0)
    cum = _cumsum_rows(logdec)
    cum_ex = cum - logdec
    cum_end = cum[c - 1:c, :]

    ti = lax.broadcasted_iota(I32, (c, c), 0)
    si = lax.broadcasted_iota(I32, (c, c), 1)
    lower_strict = ti > si
    lower_incl = ti >= si
    eye_c = (ti == si).astype(F32)
    eye_k = (lax.broadcasted_iota(I32, (HD_B, HD_B), 0) == lax.broadcasted_iota(I32, (HD_B, HD_B), 1)).astype(F32)
    rk_all = rkp_ref[...]
    gnb = gnb_ref[...]

    heads = range(pr_ref.shape[1] // HD_B)
    sls = [slice(h * HD_B, (h + 1) * HD_B) for h in heads]
    r_ = [xr[:, s] for s in sls]
    k_ = [kmod[:, s] for s in sls]
    v_ = [xv[:, s] for s in sls]
    kk_ = [kkv[:, s] for s in sls]
    kk_ = [x / jnp.maximum(jnp.sqrt(jnp.sum(x * x, axis=-1, keepdims=True)), 1e-12) for x in kk_]
    b_ = [kk_[h] * a[:, sls[h]] for h in heads]
    kap_ = [kk_[h] * jnp.exp(cum_ex[:, sls[h]]) for h in heads]
    rt_ = [r_[h] * jnp.exp(cum[:, sls[h]]) for h in heads]
    inv_ = [jnp.exp(-cum[:, sls[h]]) for h in heads]
    amat = [_dot2(jnp.concatenate([kap_[h], rt_[h]], axis=0),
                  jnp.concatenate([k_[h] * inv_[h], b_[h] * inv_[h]], axis=0), NT) for h in heads]
    a_kk = [jnp.where(lower_strict, m[0:c, 0:c], 0.0) for m in amat]
    a_kb = [jnp.where(lower_strict, m[0:c, c:2 * c], 0.0) for m in amat]
    a_rk = [jnp.where(lower_incl, m[c:2 * c, 0:c], 0.0) for m in amat]
    a_rb = [jnp.where(lower_incl, m[c:2 * c, c:2 * c], 0.0) for m in amat]

    half = ((ti % 2) == 1) & (si == ti - 1)
    tinv = [eye_c - jnp.where(half, m, 0.0) for m in a_kb]
    av = [_dot2(jnp.concatenate([a_kk[h], a_rk[h]], axis=0), v_[h]) for h in heads]
    w = 2
    while w < c:
        off = (ti // (2 * w) == si // (2 * w)) & ((ti // w) % 2 == 1) & ((si // w) % 2 == 0)
        l_off = [jnp.where(off, m, 0.0) for m in a_kb]
        ld = [_dot_t(l_off[h], tinv[h]) for h in heads]
        tinv = [tinv[h] - _dot_t(tinv[h], ld[h]) for h in heads]
        w *= 2

    tx = [_dot2(tinv[h], jnp.concatenate([av[h][0:c], kap_[h]], axis=1)) for h in heads]
    arb_tx = [_dot2(a_rb[h], tx[h]) for h in heads]
    dec_end = [jnp.exp(cum_end[:, sls[h]] - cum[:, sls[h]]) for h in heads]
    k_end = [k_[h] * dec_end[h] for h in heads]
    b_end = [b_[h] * dec_end[h] for h in heads]
    g0 = [_dot2(jnp.concatenate([v_[h], -tx[h][:, 0:HD_B]], axis=0),
                jnp.concatenate([k_end[h], b_end[h]], axis=0), TN) for h in heads]
    ktb = [_dot2(tx[h][:, HD_B:2 * HD_B], b_end[h], TN) for h in heads]
    for h in heads:
        s = sls[h]
        y0_ref[:, s] = av[h][c:2 * c] - arb_tx[h][:, 0:HD_B]
        rt_ref[:, s] = rt_[h] - arb_tx[h][:, HD_B:2 * HD_B]
        bonus_ref[:, s] = jnp.sum(r_[h] * k_[h] * rk_all[:, s], axis=-1, keepdims=True) * v_[h] + gnb[:, s]
        mm_ref[h] = eye_k * jnp.exp(cum_end[:, s]) - ktb[h]
        g0_ref[h] = g0[h]


def _rwkv_state_kernel(y0_ref, rt_ref, bonus_ref, gate_ref, mm_ref, g0_ref, s0_ref, gng_ref,
                       o_ref, sT_ref, st_ref):
    @pl.when(pl.program_id(1) == 0)
    def _():
        st_ref[...] = s0_ref[...]

    gng = gng_ref[...]
    for h0 in range(0, N_HEADS_B, STATE_GROUP):
        heads = range(h0, h0 + STATE_GROUP)
        sls = {h: slice(h * HD_B, (h + 1) * HD_B) for h in heads}
        s0 = {h: st_ref[h] for h in heads}
        s_new = {h: _dot2(s0[h], mm_ref[h]) + g0_ref[h] for h in heads}
        y = {h: y0_ref[:, sls[h]] + _dot2(rt_ref[:, sls[h]], s0[h], NT) for h in heads}
        for h in heads:
            st_ref[h] = s_new[h]
            sT_ref[h] = s_new[h]
        mean = {h: jnp.mean(y[h], axis=-1, keepdims=True) for h in heads}
        dev = {h: y[h] - mean[h] for h in heads}
        var = {h: jnp.mean(jnp.square(dev[h]), axis=-1, keepdims=True) for h in heads}
        for h in heads:
            yn = dev[h] * lax.rsqrt(var[h] + GN_EPS) * gng[:, sls[h]]
            o_ref[:, sls[h]] = ((yn + bonus_ref[:, sls[h]]) * gate_ref[:, sls[h]]).astype(o_ref.dtype)


def rwkv7(pp, shift0, wkv0, prm, bsz, t, chunk, n_valid):
    nc = t // chunk
    lb = CHUNK_LB * LANE
    ngrp = D_B // lb
    hs = lb // HD_B
    assert A_RKV % lb == 0
    rb = A_RKV // lb
    per = D_B // lb
    lo_blk = A_LO // (2 * LANE)
    lo3 = 3 * D_B // (2 * LANE)
    sub = chunk // SUBLANE

    def rows(off):
        return pl.BlockSpec((chunk, lb), lambda b, hg, ci: (b * nc + ci, rb + off * per + hg))

    def prev_rows(off):
        return pl.BlockSpec((SUBLANE, lb),
                            lambda b, hg, ci: (jnp.maximum((b * nc + ci) * sub - 1, 0), rb + off * per + hg))

    def srow(off):
        return pl.BlockSpec((None, 1, lb), lambda b, hg, ci: (b, 0, off * per + hg))

    def prow(off):
        return pl.BlockSpec((1, lb), lambda b, hg, ci: (0, off * per + hg))

    def per_head(shape0):
        return pl.BlockSpec((shape0, lb), lambda b, hg, ci: (0, hg))

    lo_spec = lambda rws, imap: pl.BlockSpec((rws, 2 * LANE), imap)
    in_specs = [rows(0), rows(1), rows(2), lo_spec(chunk, lambda b, hg, ci: (b * nc + ci, lo_blk)),
                prev_rows(0), prev_rows(1), prev_rows(2),
                lo_spec(SUBLANE, lambda b, hg, ci: (jnp.maximum((b * nc + ci) * sub - 1, 0), lo_blk)),
                srow(0), srow(1), srow(2), pl.BlockSpec((None, 1, 2 * LANE), lambda b, hg, ci: (b, 0, lo3)),
                prow(0), prow(1), prow(2), lo_spec(1, lambda b, hg, ci: (0, lo3)),
                per_head(1), per_head(LORA_W), per_head(1), per_head(LORA_A), per_head(LORA_G),
                per_head(1), per_head(1), per_head(1), per_head(1)]
    row_out = pl.BlockSpec((chunk, lb), lambda b, hg, ci: (b * nc + ci, hg))
    mat_out = pl.BlockSpec((None, None, hs, HD_B, HD_B), lambda b, hg, ci: (b, ci, hg, 0, 0))
    n = bsz * t
    y0, rt, bonus, gate, mm, g0 = pl.pallas_call(
        functools.partial(_rwkv_chunk_kernel, n_valid=n_valid),
        out_shape=[jax.ShapeDtypeStruct((n, D_B), F32)] * 4
        + [jax.ShapeDtypeStruct((bsz, nc, N_HEADS_B, HD_B, HD_B), F32)] * 2,
        grid=(bsz, ngrp, nc),
        in_specs=in_specs,
        out_specs=[row_out] * 4 + [mat_out] * 2,
        compiler_params=_cparams(("parallel", "parallel", "parallel")),
        name="rwkv_chunk",
    )(pp, pp, pp, pp, pp, pp, pp, pp, shift0, shift0, shift0, shift0,
      prm["mu"], prm["mu"], prm["mu"], prm["mu"], prm["w0"], prm["w2"], prm["a0"], prm["a2"], prm["g2"],
      prm["kk"], prm["ka"], prm["rk"], prm["gn_b"])

    row_in = pl.BlockSpec((chunk, D_B), lambda b, ci: (b * nc + ci, 0))
    mat_in = pl.BlockSpec((None, None, N_HEADS_B, HD_B, HD_B), lambda b, ci: (b, ci, 0, 0, 0))
    state = pl.BlockSpec((None, N_HEADS_B, HD_B, HD_B), lambda b, ci: (b, 0, 0, 0))
    o_b, s_fin = pl.pallas_call(
        _rwkv_state_kernel,
        out_shape=[jax.ShapeDtypeStruct((n, D_B), BF16),
                   jax.ShapeDtypeStruct((bsz, N_HEADS_B, HD_B, HD_B), F32)],
        grid=(bsz, nc),
        in_specs=[row_in, row_in, row_in, row_in, mat_in, mat_in, state,
                  pl.BlockSpec((1, D_B), lambda b, ci: (0, 0))],
        out_specs=[row_in, state],
        scratch_shapes=[pltpu.VMEM((N_HEADS_B, HD_B, HD_B), F32)],
        compiler_params=_cparams(("parallel", "arbitrary")),
        name="rwkv_state",
    )(y0, rt, bonus, gate, mm, g0, wkv0, prm["gn_g"])
    return o_b, s_fin


def _first_lane(cond, lane):
    return jnp.min(jnp.where(cond, lane, 4 * LANE), axis=-1, keepdims=True)


def _router_kernel(h_ref, w_ref, b_ref, eid_ref, wt_ref):
    logits = _dot3(h_ref[...], w_ref[...]) + b_ref[...]
    lane = lax.broadcasted_iota(I32, logits.shape, 1)
    gmask = lane < N_GROUPS
    lg = jnp.where(gmask, logits, NEG)
    eg = jnp.where(gmask, jnp.exp(lg - jnp.max(lg, axis=-1, keepdims=True)), 0.0)
    gp = eg / jnp.sum(eg, axis=-1, keepdims=True)
    g_w = jnp.max(gp, axis=-1, keepdims=True)
    grp = _first_lane(gmask & (gp == g_w), lane)
    lo = N_GROUPS + grp * E_PER_GROUP
    emask = (lane >= lo) & (lane < lo + E_PER_GROUP)
    le = jnp.where(emask, logits, NEG)
    ee = jnp.where(emask, jnp.exp(le - jnp.max(le, axis=-1, keepdims=True)), 0.0)
    ep = ee / jnp.sum(ee, axis=-1, keepdims=True)
    p1 = jnp.max(jnp.where(emask, ep, -1.0), axis=-1, keepdims=True)
    i1 = _first_lane(emask & (ep == p1), lane)
    rest = emask & (lane != i1)
    p2 = jnp.max(jnp.where(rest, ep, -1.0), axis=-1, keepdims=True)
    i2 = _first_lane(rest & (ep == p2), lane)
    tot = p1 + p2
    eid_ref[...] = jnp.where(lane == 0, i1 - N_GROUPS, jnp.where(lane == 1, i2 - N_GROUPS, 0))
    wt_ref[...] = jnp.where(lane == 0, g_w * p1 / tot, jnp.where(lane == 1, g_w * p2 / tot, 0.0))


def router(h2, wr, br):
    n, d = h2.shape
    tm = _row_tile(n, 512)
    return pl.pallas_call(
        _router_kernel,
        out_shape=[jax.ShapeDtypeStruct((n, LANE), I32), jax.ShapeDtypeStruct((n, LANE), F32)],
        grid=(n // tm,),
        in_specs=[pl.BlockSpec((tm, d), lambda i: (i, 0)),
                  pl.BlockSpec((d, LANE), lambda i: (0, 0)),
                  pl.BlockSpec((1, LANE), lambda i: (0, 0))],
        out_specs=[pl.BlockSpec((tm, LANE), lambda i: (i, 0)), pl.BlockSpec((tm, LANE), lambda i: (i, 0))],
        compiler_params=_cparams(("parallel",)),
        name="router",
    )(h2, wr, br)


def _row_copy(src_hbm, dst_vmem, sem, src_row, dst_row):
    return pltpu.make_async_copy(src_hbm.at[pl.ds(src_row, 1), :], dst_vmem.at[pl.ds(dst_row, 1), :], sem)


def _experts_kernel(blk_e_ref, n_used_ref, tok_ref, h_hbm, w1_ref, w3_ref, w2_ref, y_ref,
                    x_buf, w1b_ref, w3b_ref, w2b_ref, sem):
    i = pl.program_id(0)
    bm = x_buf.shape[1]
    n_used = n_used_ref[0]

    def rows(blk, slot, go):
        def body(r, carry):
            go(_row_copy(h_hbm, x_buf.at[slot], sem.at[slot], tok_ref[blk * bm + r], r))
            return carry
        lax.fori_loop(0, bm, body, 0)

    @pl.when(i < n_used)
    def _():
        slot = i % 2

        @pl.when(i == 0)
        def _():
            rows(0, 0, lambda cp: cp.start())

        rows(i, slot, lambda cp: cp.wait())

        @pl.when(i + 1 < n_used)
        def _():
            rows(i + 1, 1 - slot, lambda cp: cp.start())

        @pl.when((i == 0) | (blk_e_ref[i] != blk_e_ref[jnp.maximum(i - 1, 0)]))
        def _():
            w1b_ref[...] = w1_ref[...].astype(BF16)
            w3b_ref[...] = w3_ref[...].astype(BF16)
            w2b_ref[...] = w2_ref[...].astype(BF16)

        x = x_buf[slot].astype(BF16)
        h1 = _dot(x, w1b_ref[...])
        h3 = _dot(x, w3b_ref[...])
        act = h1 * _sigmoid(h1) * h3
        y_ref[...] = _dot(act, w2b_ref[...])

    @pl.when(i >= n_used)
    def _():
        y_ref[...] = jnp.zeros_like(y_ref)


def experts(h_all, blk_e, n_used, row_tok, w1, w3, w2, l, bm):
    n_blocks = blk_e.shape[0]
    d = h_all.shape[1]
    grid_spec = pltpu.PrefetchScalarGridSpec(
        num_scalar_prefetch=3,
        grid=(n_blocks,),
        in_specs=[pl.BlockSpec(memory_space=pl.ANY),
                  pl.BlockSpec((None, None, d, D_EXPERT), lambda i, be, nu, tk: (l, be[i], 0, 0)),
                  pl.BlockSpec((None, None, d, D_EXPERT), lambda i, be, nu, tk: (l, be[i], 0, 0)),
                  pl.BlockSpec((None, None, D_EXPERT, d), lambda i, be, nu, tk: (l, be[i], 0, 0))],
        out_specs=pl.BlockSpec((bm, d), lambda i, be, nu, tk: (i, 0)),
        scratch_shapes=[pltpu.VMEM((2, bm, d), F32), pltpu.VMEM((d, D_EXPERT), BF16),
                        pltpu.VMEM((d, D_EXPERT), BF16), pltpu.VMEM((D_EXPERT, d), BF16),
                        pltpu.SemaphoreType.DMA((2,))],
    )
    return pl.pallas_call(
        _experts_kernel,
        out_shape=jax.ShapeDtypeStruct((n_blocks * bm, d), F32),
        grid_spec=grid_spec,
        compiler_params=_cparams(("arbitrary",)),
        name="experts",
    )(blk_e, n_used, row_tok, h_all, w1, w3, w2)


def _combine_kernel(dest_ref, ys_hbm, wt_ref, x_ref, g_ref, o_ref, buf, sem):
    tm = x_ref.shape[0]
    step = pl.program_id(0) * pl.num_programs(1) + pl.program_id(1)
    n_steps = pl.num_programs(0) * pl.num_programs(1)

    def rows(tile, slot, go):
        def body(r, carry):
            base = (tile * tm + r) * 2
            go(_row_copy(ys_hbm, buf.at[slot, 0], sem.at[slot], dest_ref[base], r))
            go(_row_copy(ys_hbm, buf.at[slot, 1], sem.at[slot], dest_ref[base + 1], r))
            return carry
        lax.fori_loop(0, tm, body, 0)

    slot = step % 2

    @pl.when(step == 0)
    def _():
        rows(0, 0, lambda cp: cp.start())

    rows(step, slot, lambda cp: cp.wait())

    @pl.when(step + 1 < n_steps)
    def _():
        rows(step + 1, 1 - slot, lambda cp: cp.start())

    wt = wt_ref[...]
    moe = wt[:, 0:1] * buf[slot, 0] + wt[:, 1:2] * buf[slot, 1]
    o_ref[...] = x_ref[...] + g_ref[...] * moe


def combine(ys, dest, wts, x, gate):
    b, t, d = x.shape
    tm = _row_tile(t, 128)
    nt = t // tm
    grid_spec = pltpu.PrefetchScalarGridSpec(
        num_scalar_prefetch=1,
        grid=(b, nt),
        in_specs=[pl.BlockSpec(memory_space=pl.ANY),
                  pl.BlockSpec((None, tm, LANE), lambda i, j, ds: (i, j, 0)),
                  pl.BlockSpec((None, tm, d), lambda i, j, ds: (i, j, 0)),
                  pl.BlockSpec((None, 1, d), lambda i, j, ds: (i, 0, 0))],
        out_specs=pl.BlockSpec((None, tm, d), lambda i, j, ds: (i, j, 0)),
        scratch_shapes=[pltpu.VMEM((2, 2, tm, d), F32), pltpu.SemaphoreType.DMA((2,))],
    )
    return pl.pallas_call(
        _combine_kernel,
        out_shape=jax.ShapeDtypeStruct((b, t, d), F32),
        grid_spec=grid_spec,
        compiler_params=_cparams(("arbitrary", "arbitrary")),
        name="moe_combine",
    )(dest.reshape(-1), ys, wts.reshape(b, t, LANE), x, gate.reshape(b, 1, d))


def _dispatch_tables(eid, bm):
    n = eid.shape[0]
    nk = n * 2
    n_blocks = (nk + N_EXPERTS * (bm - 1) + bm - 1) // bm
    flat_e = eid.reshape(-1)
    onehot = (flat_e[:, None] == jnp.arange(N_EXPERTS, dtype=I32)[None, :]).astype(I32)
    csum = jnp.cumsum(onehot, axis=0)
    counts = csum[-1]
    rank = jnp.sum(onehot * (csum - 1), axis=1)
    padded = (counts + bm - 1) // bm * bm
    pad_end = jnp.cumsum(padded)
    pad_start = pad_end - padded
    dest = (pad_start[flat_e] + rank).astype(I32)
    row_tok = jnp.zeros((n_blocks * bm,), I32).at[dest].set(jnp.arange(nk, dtype=I32) // 2)
    blk_start = jnp.arange(n_blocks, dtype=I32) * bm
    blk_e = jnp.minimum(jnp.sum((pad_end[None, :] <= blk_start[:, None]).astype(I32), axis=1), N_EXPERTS - 1)
    n_used = (pad_end[-1] // bm).astype(I32).reshape(1)
    return blk_e, n_used, row_tok, dest.reshape(n, 2)


def _final_norm_kernel(x_ref, g_ref, o_ref):
    x = x_ref[...]
    o_ref[...] = x * lax.rsqrt(jnp.mean(x * x, axis=-1, keepdims=True) + RMS_EPS) * g_ref[...]


def final_norm(x, g):
    b, t, d = x.shape
    tr = min(t, 256)
    return pl.pallas_call(
        _final_norm_kernel,
        out_shape=jax.ShapeDtypeStruct((b, t, d), F32),
        grid=(b, t // tr),
        in_specs=[pl.BlockSpec((None, tr, d), lambda i, j: (i, j, 0)), pl.BlockSpec((1, d), lambda i, j: (0, 0))],
        out_specs=pl.BlockSpec((None, tr, d), lambda i, j: (i, j, 0)),
        compiler_params=_cparams(("parallel", "parallel")),
        name="final_norm",
    )(x, g.reshape(1, d))


def _align_in_cols(w, axis):
    def take(a, b_):
        return lax.slice_in_dim(w, a, b_, axis=axis)

    def zeros(nz):
        shp = list(w.shape)
        shp[axis] = nz
        return jnp.zeros(shp, w.dtype)

    gate_parts = []
    for g in range(N_KV):
        gate_parts += [take(OFF_GATE_A + g * 3 * GQA_R, OFF_GATE_A + (g + 1) * 3 * GQA_R), zeros(LANE - 3 * GQA_R)]
    parts = [take(0, OFF_GATE_A)] + gate_parts + [take(OFF_RWKV + 3 * D_B, OFF_MERGE), zeros(RW_W - SHIFT_W),
                                                   take(OFF_RWKV, OFF_RWKV + 3 * D_B),
                                                   take(OFF_MERGE, OFF_MERGE + 2 * D_MODEL)]
    return jnp.concatenate(parts, axis=axis)


def _rwkv_cols(p):
    return jnp.concatenate([p[..., A_RKV:A_RKV + 3 * D_B], p[..., A_LO:A_LO + SHIFT_W - 3 * D_B]], axis=-1)


def _pad_lanes(v, width):
    return jnp.pad(v, [(0, 0)] * (v.ndim - 1) + [(0, width - v.shape[-1])])


def kernel(x_prompt, x_sample, cache_cmp, cache_slc, cache_win, state_shift, state_wkv, page_table, c_prompt, c_sample, ln1_g, ln2_g, ada_w, ada_b, w_in, cmp_w1, cmp_b1, cmp_w2, rwkv_mu, rwkv_w0, rwkv_w2, rwkv_a0, rwkv_a2, rwkv_g2, rwkv_kk, rwkv_ka, rwkv_rk, rwkv_gn_g, rwkv_gn_b, w_branch, w_out, router_g_w, router_g_b, router_e_w, router_e_b, exp_w1, exp_w3, exp_w2, final_g):
    depth = w_in.shape[0]
    bp, t, d = x_prompt.shape
    bd, ts, _ = x_sample.shape
    assert ts == 1 and t % KC == 0 and t >= WINDOW + QB
    n_pages = page_table.shape[1]
    past = n_pages * PAGE
    wb = cache_win.shape[2]
    n_phys = cache_cmp.shape[1]
    n_p = bp * t

    slopes = jnp.exp2(-8.0 * (jnp.arange(N_HEADS_A, dtype=F32) + 1.0) / N_HEADS_A)
    c_rows = bp + bd
    c16 = jnp.zeros(((c_rows + SUBLANE - 1) // SUBLANE * SUBLANE, d), F32).at[:bp].set(c_prompt).at[bp:c_rows].set(c_sample)
    mod = adaln(c16, ada_w, ada_b).reshape(depth, c16.shape[0], 6, d)

    xp, xs = x_prompt, x_sample
    outs = {k: [] for k in ("cmp_p", "slc_p", "win_p", "shf_p", "wkv_p", "cmp_s", "slc_s", "win_s", "shf_s", "wkv_s")}
    pool_cmp2d = cache_cmp.reshape(-1, HD_A)
    pool_slc2d = cache_slc.reshape(-1, HD_A)
    win2d = cache_win.reshape(-1, HD_A)

    for l in range(depth):
        mp, ms = mod[l, :bp], mod[l, bp:c_rows]
        w_in_al = _align_in_cols(w_in[l], 1).astype(BF16)
        w1b = cmp_w1[l].astype(BF16)
        w_out_b = w_out[l].astype(BF16)
        prm = {
            "mu": _pad_lanes(rwkv_mu[l][None, :], RW_W),
            "w0": rwkv_w0[l][None, :], "w2": rwkv_w2[l], "a0": rwkv_a0[l][None, :], "a2": rwkv_a2[l],
            "g2": rwkv_g2[l], "kk": rwkv_kk[l][None, :], "ka": rwkv_ka[l][None, :],
            "rk": rwkv_rk[l].reshape(1, D_B), "gn_g": rwkv_gn_g[l][None, :], "gn_b": rwkv_gn_b[l][None, :],
        }
        wr = _pad_lanes(jnp.concatenate([router_g_w[l], router_e_w[l]], axis=1), LANE)
        br = _pad_lanes(jnp.concatenate([router_g_b[l], router_e_b[l]])[None, :], LANE)

        hp_ = norm_mod(xp, ln1_g[l], mp[:, 0], mp[:, 1], BF16).reshape(n_p, d)
        pp, ppb, kv_p = in_proj(hp_, w_in_al)
        cmp_kv = compress(pp, A_KV, 1, w1b, cmp_b1[l], cmp_w2[l])
        o_a = nsa_prompt(slopes, ppb, pp, cmp_kv, bp, t)
        o_b, s_fin = rwkv7(pp, jnp.zeros((bp, 1, RW_W), F32), jnp.zeros((bp, N_HEADS_B, HD_B, HD_B), F32),
                           prm, bp, t, 64, 64)
        merged = branch_merge(o_a, o_b, w_branch, l, pp)
        xp = out_proj_residual(merged.reshape(bp, t, d), w_out_b, xp, mp[:, 2])
        kv_p = kv_p.reshape(3, bp, t, 2, N_KV, HD_A)
        outs["cmp_p"].append(kv_p[0])
        outs["slc_p"].append(kv_p[1])
        outs["win_p"].append(kv_p[2, :, t - min(WINDOW, t):])
        outs["shf_p"].append(_rwkv_cols(pp.reshape(bp, t, IN_AL)[:, -1]))
        outs["wkv_p"].append(s_fin)

        hs_ = norm_mod(xs, ln1_g[l], ms[:, 0], ms[:, 1], BF16).reshape(bd, d)
        ps, psb, kv_s = in_proj(hs_, w_in_al)
        cmp_past = compress_paged(page_table, pool_cmp2d, l * n_phys, w1b, cmp_b1[l], cmp_w2[l])
        new_rows = jnp.zeros((bd, BLK, 2 * KV_W), F32).at[:, 0].set(ps[:, A_KV:A_KV + 2 * KV_W])
        cmp_new = compress(new_rows.reshape(bd * BLK, 2 * KV_W), 0, 1, w1b, cmp_b1[l], cmp_w2[l])
        nb_past = past // BLK
        nb_real = nb_past + 1
        nbp = (nb_real + LANE - 1) // LANE * LANE
        cmp_s = jnp.concatenate([cmp_past.reshape(2, N_KV, bd, nb_past, HD_A), cmp_new[:, :, :, None, :],
                                 jnp.zeros((2, N_KV, bd, nbp - nb_real, HD_A), F32)], axis=3)
        ps3 = ps.reshape(bd, 1, IN_AL)
        psb3 = psb.reshape(bd, 1, A_GATE)
        o_cmp, idx_full = nsa_sample_cmp(slopes, psb3, cmp_s, past, nb_real)
        idx_flat = idx_full[:, :, :N_SEL, 0].reshape(-1)
        o_a_s = nsa_sample_sel(idx_flat, page_table, slopes, psb3, ps3, pool_slc2d, l * n_phys,
                               win2d, l * bd, wb, o_cmp, past).reshape(bd, D_A)
        ps_pad = jnp.zeros((bd, SUBLANE, IN_AL), F32).at[:, 0].set(ps).reshape(bd * SUBLANE, IN_AL)
        o_b_s, s_fin_s = rwkv7(ps_pad, _pad_lanes(state_shift[l], RW_W)[:, None, :], state_wkv[l],
                               prm, bd, SUBLANE, SUBLANE, 1)
        o_b_s = o_b_s.reshape(bd, SUBLANE, D_B)[:, 0]
        merged_s = branch_merge(o_a_s, o_b_s, w_branch, l, ps)
        xs = out_proj_residual(merged_s.reshape(bd, 1, d), w_out_b, xs, ms[:, 2])
        kv_s = kv_s.reshape(3, bd, 1, 2, N_KV, HD_A)
        outs["cmp_s"].append(kv_s[0])
        outs["slc_s"].append(kv_s[1])
        outs["win_s"].append(jnp.concatenate([cache_win[l, :, 1:], kv_s[2]], axis=1))
        outs["shf_s"].append(_rwkv_cols(ps))
        outs["wkv_s"].append(s_fin_s)

        h2p = norm_mod(xp, ln2_g[l], mp[:, 3], mp[:, 4], F32).reshape(n_p, d)
        h2s = norm_mod(xs, ln2_g[l], ms[:, 3], ms[:, 4], F32).reshape(bd, d)
        eid_p, wt_p = router(h2p, wr, br)
        eid_s, wt_s = router(h2s, wr, br)
        h_all = jnp.concatenate([h2p, h2s], axis=0)
        eid = jnp.concatenate([eid_p[:, :2], eid_s[:, :2]], axis=0)
        blk_e, n_used, row_tok, dest = _dispatch_tables(eid, MOE_BLOCK)
        ys = experts(h_all, blk_e, n_used, row_tok, exp_w1, exp_w3, exp_w2, l, MOE_BLOCK)
        xp = combine(ys, dest[:n_p], wt_p, xp, mp[:, 5])
        xs = combine(ys, dest[n_p:], wt_s, xs, ms[:, 5])

    y_prompt = final_norm(xp, final_g)
    y_sample = final_norm(xs, final_g)
    st = lambda k: jnp.stack(outs[k])
    return (y_prompt, y_sample, st("cmp_p"), st("slc_p"), st("win_p"), st("shf_p"), st("wkv_p"),
            st("cmp_s"), st("slc_s"), st("win_s"), st("shf_s"), st("wkv_s"))
```

```python
import functools

import jax
import jax.numpy as jnp
from jax import lax
from jax.experimental import pallas as pl
from jax.experimental.pallas import tpu as pltpu

F32 = jnp.float32
BF16 = jnp.bfloat16
I32 = jnp.int32

LANE = 128
SUBLANE = 8
VMEM_LIMIT = 56 * 1024 * 1024

D_MODEL = 2048
HD_A = 128
N_HEADS_A = 8
N_KV = 2
GQA_R = 4
BLK = 64
N_SEL = 16
WINDOW = 512
CMP_HID = 256
SCALE_A = HD_A ** -0.5
D_A = 1024
D_B = 1024
HD_B = 64
N_HEADS_B = 16
LORA_W, LORA_A, LORA_G = 64, 64, 32
GN_EPS = HD_B * 1e-5
N_GROUPS = 4
E_PER_GROUP = 8
N_EXPERTS = 32
D_EXPERT = 512
MOE_BLOCK = 128
RMS_EPS = 1e-6
PAGE = 128

Q_W = N_HEADS_A * HD_A
KV_W = N_KV * HD_A
OFF_KV = Q_W
OFF_GATE_A = OFF_KV + 6 * KV_W
OFF_RWKV = OFF_GATE_A + 3 * N_HEADS_A
SHIFT_W = 3 * D_B + LORA_W + LORA_A + LORA_G
OFF_MERGE = OFF_RWKV + SHIFT_W

A_KV = Q_W
A_GATE = A_KV + 6 * KV_W
A_LO = A_GATE + N_KV * LANE
A_RKV = A_LO + 2 * LANE
RW_W = 3 * D_B + 2 * LANE
A_MERGE = A_RKV + 3 * D_B
IN_AL = A_MERGE + 2 * D_MODEL
NEG = -1e30


def _cparams(sem):
    return pltpu.CompilerParams(dimension_semantics=sem, vmem_limit_bytes=VMEM_LIMIT)


def _dot(a, b, dims=(((1,), (0,)), ((), ()))):
    return lax.dot_general(a.astype(BF16), b.astype(BF16), dims, preferred_element_type=F32)


def _dot_nt(a, b):
    return _dot(a, b, (((1,), (1,)), ((), ())))


def _dot_tn(a, b):
    return _dot(a, b, (((0,), (0,)), ((), ())))


def _split3(x):
    h = x.astype(BF16)
    r1 = x - h.astype(F32)
    m = r1.astype(BF16)
    lo = (r1 - m.astype(F32)).astype(BF16)
    return h, m, lo


def _dot3(a, b, dims=(((1,), (0,)), ((), ()))):
    ah, am, al = _split3(a)
    bh, bm, bl = _split3(b)
    d = lambda x, y: lax.dot_general(x, y, dims, preferred_element_type=F32)
    return (d(ah, bh) + (d(ah, bm) + d(am, bh))) + ((d(am, bm) + d(ah, bl)) + d(al, bh))


def _sigmoid(x):
    return 1.0 / (1.0 + jnp.exp(-x))


def _softplus(x):
    return jnp.maximum(x, 0.0) + jnp.log(1.0 + jnp.exp(-jnp.abs(x)))


def _adaln_kernel(c_ref, w_ref, b_ref, o_ref):
    c = c_ref[...]
    h = c * _sigmoid(c)
    o_ref[...] = _dot(h, w_ref[...]) + b_ref[...]


def adaln(c16, ada_w, ada_b):
    depth, d, n = ada_w.shape
    tn = 1024
    return pl.pallas_call(
        _adaln_kernel,
        out_shape=jax.ShapeDtypeStruct((depth, c16.shape[0], n), F32),
        grid=(depth, n // tn),
        in_specs=[pl.BlockSpec(c16.shape, lambda l, j: (0, 0)),
                  pl.BlockSpec((None, d, tn), lambda l, j: (l, 0, j)),
                  pl.BlockSpec((None, 1, tn), lambda l, j: (l, 0, j))],
        out_specs=pl.BlockSpec((None, c16.shape[0], tn), lambda l, j: (l, 0, j)),
        compiler_params=_cparams(("parallel", "parallel")),
        name="adaln",
    )(c16, ada_w, ada_b.reshape(depth, 1, n))


def _norm_kernel(x_ref, g_ref, sh_ref, sc_ref, o_ref):
    x = x_ref[...]
    y = x * lax.rsqrt(jnp.mean(x * x, axis=-1, keepdims=True) + RMS_EPS)
    y = y * g_ref[...]
    o_ref[...] = (y * (1.0 + sc_ref[...]) + sh_ref[...]).astype(o_ref.dtype)


def norm_mod(x, g, shift, scale, out_dtype):
    b, t, d = x.shape
    tr = min(t, 256)
    return pl.pallas_call(
        _norm_kernel,
        out_shape=jax.ShapeDtypeStruct((b, t, d), out_dtype),
        grid=(b, t // tr),
        in_specs=[pl.BlockSpec((None, tr, d), lambda i, j: (i, j, 0)),
                  pl.BlockSpec((1, d), lambda i, j: (0, 0)),
                  pl.BlockSpec((None, 1, d), lambda i, j: (i, 0, 0)),
                  pl.BlockSpec((None, 1, d), lambda i, j: (i, 0, 0))],
        out_specs=pl.BlockSpec((None, tr, d), lambda i, j: (i, j, 0)),
        compiler_params=_cparams(("parallel", "parallel")),
        name="norm_mod",
    )(x, g.reshape(1, d), shift.reshape(b, 1, d), scale.reshape(b, 1, d))


IN_TN = 2 * KV_W
N_BF16_TILES = A_GATE // IN_TN
KV_TILE0 = A_KV // IN_TN
KV_ROWS = 2 * N_KV


def _in_proj_kernel(a_ref, w_ref, o_ref, ob_ref, kv_ref):
    j = pl.program_id(1)
    acc = _dot(a_ref[...], w_ref[...])
    o_ref[...] = acc

    @pl.when(j < N_BF16_TILES)
    def _():
        ob_ref[...] = acc.astype(BF16)

    @pl.when((j >= KV_TILE0) & (j < N_BF16_TILES))
    def _():
        tm = acc.shape[0]
        for c4 in range(KV_ROWS):
            kv_ref[pl.ds(c4, tm, stride=KV_ROWS), :] = acc[:, c4 * HD_A:(c4 + 1) * HD_A]


def _row_tile(m, cap):
    return m if m <= cap else cap


def in_proj(a, w):
    m, k = a.shape
    tm = _row_tile(m, 2048)
    tn = IN_TN
    assert m % tm == 0
    return pl.pallas_call(
        _in_proj_kernel,
        out_shape=[jax.ShapeDtypeStruct((m, IN_AL), F32), jax.ShapeDtypeStruct((m, A_GATE), BF16),
                   jax.ShapeDtypeStruct((3, m * KV_ROWS, HD_A), F32)],
        grid=(m // tm, IN_AL // tn),
        in_specs=[pl.BlockSpec((tm, k), lambda i, j: (i, 0)),
                  pl.BlockSpec((k, tn), lambda i, j: (0, j))],
        out_specs=[pl.BlockSpec((tm, tn), lambda i, j: (i, j)),
                   pl.BlockSpec((tm, tn), lambda i, j: (i, jnp.minimum(j, N_BF16_TILES - 1))),
                   pl.BlockSpec((None, tm * KV_ROWS, HD_A),
                                lambda i, j: (jnp.clip(j - KV_TILE0, 0, 2), i, 0))],
        compiler_params=_cparams(("parallel", "arbitrary")),
        name="proj_in",
    )(a, w)


def _branch_kernel(oa_ref, ob_ref, wa_ref, wb_ref, ga_ref, gb_ref, o_ref):
    ya = _dot(oa_ref[...], wa_ref[...])
    yb = _dot(ob_ref[...], wb_ref[...])
    o_ref[...] = (_sigmoid(ga_ref[...]) * ya + _sigmoid(gb_ref[...]) * yb).astype(o_ref.dtype)


def branch_merge(o_a, o_b, w_branch, l, pp):
    m = o_a.shape[0]
    tm = _row_tile(m, 1024)
    tn = 512
    gcol = A_MERGE // tn
    return pl.pallas_call(
        _branch_kernel,
        out_shape=jax.ShapeDtypeStruct((m, D_MODEL), BF16),
        grid=(m // tm, D_MODEL // tn),
        in_specs=[pl.BlockSpec((tm, D_A), lambda i, j: (i, 0)),
                  pl.BlockSpec((tm, D_B), lambda i, j: (i, 0)),
                  pl.BlockSpec((None, D_A, tn), lambda i, j: (l, 0, j)),
                  pl.BlockSpec((None, D_B, tn), lambda i, j: (l, D_A // D_B, j)),
                  pl.BlockSpec((tm, tn), lambda i, j: (i, gcol + j)),
                  pl.BlockSpec((tm, tn), lambda i, j: (i, gcol + D_MODEL // tn + j))],
        out_specs=pl.BlockSpec((tm, tn), lambda i, j: (i, j)),
        compiler_params=_cparams(("parallel", "parallel")),
        name="branch_merge",
    )(o_a, o_b, w_branch, w_branch, pp, pp)


def _resid_kernel(m_ref, w_ref, x_ref, g_ref, o_ref):
    o_ref[...] = x_ref[...] + g_ref[...] * _dot(m_ref[...], w_ref[...])


def out_proj_residual(merged, w_out_l, x, gate):
    b, t, d = x.shape
    tm = _row_tile(t, 1024)
    tn = 512
    nt = t // tm
    return pl.pallas_call(
        _resid_kernel,
        out_shape=jax.ShapeDtypeStruct((b, t, d), F32),
        grid=(b, nt, d // tn),
        in_specs=[pl.BlockSpec((None, tm, d), lambda i, r, j: (i, r, 0)),
                  pl.BlockSpec((d, tn), lambda i, r, j: (0, j)),
                  pl.BlockSpec((None, tm, tn), lambda i, r, j: (i, r, j)),
                  pl.BlockSpec((None, 1, tn), lambda i, r, j: (i, 0, j))],
        out_specs=pl.BlockSpec((None, tm, tn), lambda i, r, j: (i, r, j)),
        compiler_params=_cparams(("parallel", "parallel", "parallel")),
        name="out_proj",
    )(merged, w_out_l, x, gate.reshape(b, 1, d))


T_PER_STEP = 8


def _gelu_tanh(x):
    return 0.5 * x * (1.0 + jnp.tanh(0.7978845608028654 * (x + 0.044715 * x * x * x)))


def _compress_kernel(x_ref, w1_ref, b1_ref, w2_ref, o_ref, acc_ref):
    tc = pl.program_id(3)
    nblk = o_ref.shape[0]

    @pl.when(tc == 0)
    def _():
        acc_ref[...] = jnp.zeros_like(acc_ref)

    acc = acc_ref[...]
    for tl in range(T_PER_STEP):
        t = tc * T_PER_STEP + tl
        xt = x_ref[pl.ds(t, nblk, stride=BLK), :]
        acc = acc + _dot(xt, w1_ref[tl * HD_A:(tl + 1) * HD_A, :])
    acc_ref[...] = acc

    @pl.when(tc == pl.num_programs(3) - 1)
    def _():
        h = _gelu_tanh(acc + b1_ref[...])
        o_ref[...] = _dot(h, w2_ref[...])


def compress(rows, col0, n_slabs, w1b, b1, w2):
    r = rows.shape[0] // n_slabs
    nblk = r // BLK
    cb = col0 // HD_A
    return pl.pallas_call(
        _compress_kernel,
        out_shape=jax.ShapeDtypeStruct((2, N_KV, n_slabs * nblk, HD_A), F32),
        grid=(2, N_KV, n_slabs, BLK // T_PER_STEP),
        in_specs=[pl.BlockSpec((r, HD_A), lambda kv, g, s, tc: (s, cb + kv * N_KV + g)),
                  pl.BlockSpec((None, T_PER_STEP * HD_A, CMP_HID), lambda kv, g, s, tc: (kv, tc, 0)),
                  pl.BlockSpec((None, 1, CMP_HID), lambda kv, g, s, tc: (kv, 0, 0)),
                  pl.BlockSpec((None, CMP_HID, HD_A), lambda kv, g, s, tc: (kv, 0, 0))],
        out_specs=pl.BlockSpec((None, None, nblk, HD_A), lambda kv, g, s, tc: (kv, g, s, 0)),
        scratch_shapes=[pltpu.VMEM((nblk, CMP_HID), F32)],
        compiler_params=_cparams(("parallel", "parallel", "parallel", "arbitrary")),
        name="compress",
    )(rows, w1b, b1.reshape(2, 1, CMP_HID), w2)


QB = 128
KC = 512


def _softmax_cols(s, valid):
    sm = jnp.where(valid, s, NEG)
    m = jnp.max(sm, axis=0, keepdims=True)
    e = jnp.where(valid, jnp.exp(sm - m), 0.0)
    return e, jnp.sum(e, axis=0, keepdims=True)


def _safe(den):
    return jnp.where(den > 0, den, 1.0)


def _select_mask_t(imp, jblk, cur):
    nb = imp.shape[0]
    score = jnp.where(jblk < cur, imp, jnp.where(jblk == cur, GQA_R + 1.0, -1.0))
    rank = jnp.zeros(imp.shape, I32)
    for i in range(nb):
        row = score[i:i + 1, :]
        beats = (row > score) | ((row == score) & (jblk > i))
        rank = rank + beats.astype(I32)
    return (rank < N_SEL) & (jblk <= cur)


def _nsa_prompt_kernel(slopes_ref, q_ref, kc_ref, vc_ref, ks_ref, vs_ref, kw_ref, vw_ref, gate_ref,
                       o_ref, bias_ref, acc_ref):
    g = pl.program_id(1)
    i = pl.program_id(2)
    t_len = ks_ref.shape[0]
    nb = t_len // BLK
    q0 = i * QB
    span = WINDOW + QB

    jblk = lax.broadcasted_iota(I32, (nb, QB), 0)
    qpos_l = q0 + lax.broadcasted_iota(I32, (nb, QB), 1)
    dist_c = qpos_l - ((jblk + 1) * BLK - 1)
    valid_c = dist_c >= 0
    dist_cf = dist_c.astype(F32)
    kc = kc_ref[...]
    vc = vc_ref[...]

    qs = [q_ref[:, r * HD_A:(r + 1) * HD_A] for r in range(GQA_R)]
    slopes = [slopes_ref[g * GQA_R + r] for r in range(GQA_R)]

    imp = jnp.zeros((nb, QB), F32)
    o_cmp = []
    for r in range(GQA_R):
        s = _dot_nt(kc, qs[r]) * SCALE_A - slopes[r] * dist_cf
        e, den = _softmax_cols(s, valid_c)
        p = e / _safe(den)
        imp = imp + p
        o_cmp.append(_dot_tn(p, vc))

    sel = _select_mask_t(imp, jblk, qpos_l // BLK)
    expand = (lax.broadcasted_iota(I32, (nb, t_len), 1) // BLK
              == lax.broadcasted_iota(I32, (nb, t_len), 0))
    key_sel = _dot_tn(sel.astype(F32), expand.astype(F32))
    row_q = q0 + lax.broadcasted_iota(I32, (QB, KC), 0)
    col_k = lax.broadcasted_iota(I32, (QB, KC), 1)
    for cc in range(t_len // KC):
        @pl.when(cc * KC < q0 + QB)
        def _(cc=cc):
            ok = (key_sel[:, cc * KC:(cc + 1) * KC] > 0.5) & (row_q >= cc * KC + col_k)
            bias_ref[cc] = jnp.where(ok, 0.0, NEG)

    n_chunks = (q0 + QB + KC - 1) // KC
    kpos0 = lax.broadcasted_iota(I32, (1, KC), 1).astype(F32)
    acc_ref[...] = jnp.zeros_like(acc_ref)

    def body(c, carry):
        ms, ls = carry
        k0 = pl.multiple_of(c * KC, KC)
        kk = ks_ref[pl.ds(k0, KC), :]
        vv = vs_ref[pl.ds(k0, KC), :]
        bias = bias_ref[c]
        kpos = kpos0 + k0.astype(F32)
        ss = [_dot_nt(qs[r], kk) * SCALE_A + (bias + slopes[r] * kpos) for r in range(GQA_R)]
        m_new = [jnp.maximum(ms[r], jnp.max(ss[r], axis=-1, keepdims=True)) for r in range(GQA_R)]
        alpha = [jnp.exp(ms[r] - m_new[r]) for r in range(GQA_R)]
        es = [jnp.exp(ss[r] - m_new[r]) for r in range(GQA_R)]
        l_new = [alpha[r] * ls[r] + jnp.sum(es[r], axis=-1, keepdims=True) for r in range(GQA_R)]
        pv = [_dot(es[r], vv) for r in range(GQA_R)]
        for r in range(GQA_R):
            acc_ref[r] = alpha[r] * acc_ref[r] + pv[r]
        return tuple(m_new), tuple(l_new)

    init = (tuple(jnp.full((QB, 1), NEG, F32) for _ in range(GQA_R)),
            tuple(jnp.zeros((QB, 1), F32) for _ in range(GQA_R)))
    _, l_sel = lax.fori_loop(0, n_chunks, body, init)

    start = pl.multiple_of(jnp.maximum(i - WINDOW // QB, 0) * QB, QB)
    kw = kw_ref[pl.ds(start, span), :]
    vw = vw_ref[pl.ds(start, span), :]
    dist_w = (q0 + lax.broadcasted_iota(I32, (QB, span), 0)) - (start + lax.broadcasted_iota(I32, (QB, span), 1))
    bias_w = jnp.where((dist_w >= 0) & (dist_w < WINDOW), 0.0, NEG)
    kpos_w = (start + lax.broadcasted_iota(I32, (1, span), 1)).astype(F32)
    gates = _sigmoid(gate_ref[...])
    sw = [_dot_nt(qs[r], kw) * SCALE_A + (bias_w + slopes[r] * kpos_w) for r in range(GQA_R)]
    ew = [jnp.exp(sw[r] - jnp.max(sw[r], axis=-1, keepdims=True)) for r in range(GQA_R)]
    den_w = [jnp.sum(ew[r], axis=-1, keepdims=True) for r in range(GQA_R)]
    pv_w = [_dot(ew[r], vw) for r in range(GQA_R)]
    for r in range(GQA_R):
        o = (gates[:, 3 * r:3 * r + 1] * o_cmp[r] + gates[:, 3 * r + 1:3 * r + 2] * (acc_ref[r] / l_sel[r])
             + gates[:, 3 * r + 2:3 * r + 3] * (pv_w[r] / den_w[r]))
        o_ref[:, r * HD_A:(r + 1) * HD_A] = o.astype(o_ref.dtype)


def nsa_prompt(slopes, ppb, pp, cmp_kv, b, t):
    nq = t // QB
    nb = t // BLK
    kvb = A_KV // HD_A

    def kv_spec(branch, kv):
        return pl.BlockSpec((t, HD_A), lambda bi, g, i, s: (bi, kvb + branch * 4 + kv * 2 + g))

    grid_spec = pltpu.PrefetchScalarGridSpec(
        num_scalar_prefetch=1,
        grid=(b, N_KV, nq),
        in_specs=[pl.BlockSpec((QB, GQA_R * HD_A), lambda bi, g, i, s: (bi * nq + i, g)),
                  pl.BlockSpec((None, None, nb, HD_A), lambda bi, g, i, s: (0, g, bi, 0)),
                  pl.BlockSpec((None, None, nb, HD_A), lambda bi, g, i, s: (1, g, bi, 0)),
                  kv_spec(1, 0), kv_spec(1, 1), kv_spec(2, 0), kv_spec(2, 1),
                  pl.BlockSpec((QB, LANE), lambda bi, g, i, s: (bi * nq + i, A_GATE // LANE + g))],
        out_specs=pl.BlockSpec((QB, GQA_R * HD_A), lambda bi, g, i, s: (bi * nq + i, g)),
        scratch_shapes=[pltpu.VMEM((t // KC, QB, KC), F32), pltpu.VMEM((GQA_R, QB, HD_A), F32)],
    )
    return pl.pallas_call(
        _nsa_prompt_kernel,
        out_shape=jax.ShapeDtypeStruct((b * t, D_A), BF16),
        grid_spec=grid_spec,
        compiler_params=_cparams(("parallel", "parallel", "arbitrary")),
        name="nsa_prompt",
    )(slopes, ppb, cmp_kv, cmp_kv, ppb, ppb, ppb, ppb, pp)


PAGE_ROWS = PAGE * KV_ROWS
SLAB_PAGES = 64


def _compress_paged_kernel(pt_ref, pool_hbm, w1_ref, b1_ref, w2_ref, o_ref, slab_ref, acc_ref, sem, *, page0):
    s = pl.program_id(0)
    tc = pl.program_id(1)
    n_slabs = pl.num_programs(0)
    nblk = SLAB_PAGES * (PAGE // BLK)

    def page_copy(slab, p, slot):
        phys = pt_ref[slab * SLAB_PAGES + p]
        return pltpu.make_async_copy(pool_hbm.at[pl.ds((page0 + phys) * PAGE_ROWS, PAGE_ROWS), :],
                                     slab_ref.at[slot, pl.ds(p * PAGE_ROWS, PAGE_ROWS), :], sem.at[slot])

    def start_slab(slab, slot):
        lax.fori_loop(0, SLAB_PAGES, lambda p, c: (page_copy(slab, p, slot).start(), c)[1], 0)

    def wait_slab(slab, slot):
        lax.fori_loop(0, SLAB_PAGES, lambda p, c: (page_copy(slab, p, slot).wait(), c)[1], 0)

    slot = s % 2

    @pl.when(tc == 0)
    def _():
        @pl.when(s == 0)
        def _():
            start_slab(0, 0)

        wait_slab(s, slot)

        @pl.when(s + 1 < n_slabs)
        def _():
            start_slab(s + 1, 1 - slot)

        acc_ref[...] = jnp.zeros_like(acc_ref)

    for tl in range(T_PER_STEP):
        t = tc * T_PER_STEP + tl
        for kv in range(2):
            w1t = w1_ref[kv, tl * HD_A:(tl + 1) * HD_A, :]
            for g in range(N_KV):
                c4 = kv * N_KV + g
                xt = slab_ref[slot, pl.ds(t * KV_ROWS + c4, nblk, stride=BLK * KV_ROWS), :]
                acc_ref[c4] += _dot(xt, w1t)

    @pl.when(tc == pl.num_programs(1) - 1)
    def _():
        for kv in range(2):
            for g in range(N_KV):
                h = _gelu_tanh(acc_ref[kv * N_KV + g] + b1_ref[kv])
                o_ref[kv, g] = _dot(h, w2_ref[kv])


def compress_paged(page_table, pool2d, page0, w1b, b1, w2):
    bd, n_pages = page_table.shape
    assert n_pages % SLAB_PAGES == 0
    n_slabs = bd * n_pages // SLAB_PAGES
    nblk = SLAB_PAGES * (PAGE // BLK)
    grid_spec = pltpu.PrefetchScalarGridSpec(
        num_scalar_prefetch=1,
        grid=(n_slabs, BLK // T_PER_STEP),
        in_specs=[pl.BlockSpec(memory_space=pl.ANY),
                  pl.BlockSpec((2, T_PER_STEP * HD_A, CMP_HID), lambda s, tc, pt: (0, tc, 0)),
                  pl.BlockSpec((2, 1, CMP_HID), lambda s, tc, pt: (0, 0, 0)),
                  pl.BlockSpec((2, CMP_HID, HD_A), lambda s, tc, pt: (0, 0, 0))],
        out_specs=pl.BlockSpec((2, N_KV, nblk, HD_A), lambda s, tc, pt: (0, 0, s, 0)),
        scratch_shapes=[pltpu.VMEM((2, SLAB_PAGES * PAGE_ROWS, HD_A), F32),
                        pltpu.VMEM((2 * N_KV, nblk, CMP_HID), F32),
                        pltpu.SemaphoreType.DMA((2,))],
    )
    return pl.pallas_call(
        functools.partial(_compress_paged_kernel, page0=page0),
        out_shape=jax.ShapeDtypeStruct((2, N_KV, n_slabs * nblk, HD_A), F32),
        grid_spec=grid_spec,
        compiler_params=_cparams(("arbitrary", "arbitrary")),
        name="compress_paged",
    )(page_table.reshape(-1), pool2d, w1b, b1.reshape(2, 1, CMP_HID), w2)


SEL_PAD = 128


def _heads_on_sublanes(q_ref):
    q = q_ref[...].astype(F32)
    row = lax.broadcasted_iota(I32, (SUBLANE, HD_A), 0)
    q8 = jnp.zeros((SUBLANE, HD_A), F32)
    for r in range(GQA_R):
        q8 = jnp.where(row == r, jnp.broadcast_to(q[:, r * HD_A:(r + 1) * HD_A], (SUBLANE, HD_A)), q8)
    return q8


def _slopes_on_sublanes(slopes_ref, g):
    row = lax.broadcasted_iota(I32, (SUBLANE, 1), 0)
    slope = jnp.zeros((SUBLANE, 1), F32)
    for r in range(GQA_R):
        slope = jnp.where(row == r, slopes_ref[g * GQA_R + r], slope)
    return slope


def _softmax_rows(s, valid):
    sm = jnp.where(valid, s, NEG)
    m = jnp.max(sm, axis=-1, keepdims=True)
    e = jnp.where(valid, jnp.exp(sm - m), 0.0)
    return e, jnp.sum(e, axis=-1, keepdims=True)


def _nsa_sample_cmp_kernel(slopes_ref, q_ref, kc_ref, vc_ref, o_ref, idx_ref, *, past, nb_real):
    g = pl.program_id(1)
    nbp = kc_ref.shape[0]
    kc = kc_ref[...]
    vc = vc_ref[...]
    q8 = _heads_on_sublanes(q_ref)
    slope = _slopes_on_sublanes(slopes_ref, g)
    head = lax.broadcasted_iota(I32, (SUBLANE, 1), 0) < GQA_R
    jrow = lax.broadcasted_iota(I32, (1, nbp), 1)
    dist = past - ((jrow + 1) * BLK - 1)
    valid = (dist >= 0) & (jrow < nb_real)
    s = _dot_nt(q8, kc) * SCALE_A - slope * dist.astype(F32)
    e, den = _softmax_rows(s, valid)
    p = e / _safe(den)
    o = _dot(p, vc)
    for r in range(GQA_R):
        o_ref[:, r * HD_A:(r + 1) * HD_A] = o[r:r + 1, :]

    imp = jnp.sum(jnp.where(head, p, 0.0), axis=0, keepdims=True)
    cur = past // BLK
    score_r = jnp.where(jrow < cur, imp, jnp.where(jrow == cur, GQA_R + 1.0, -1.0))
    score_r = jnp.where(jrow < nb_real, score_r, -2.0)
    ii = lax.broadcasted_iota(I32, (nbp, nbp), 0)
    jj = lax.broadcasted_iota(I32, (nbp, nbp), 1)
    sc_b = jnp.broadcast_to(score_r, (nbp, nbp))
    score_c = jnp.sum(jnp.where(ii == jj, sc_b, 0.0), axis=1, keepdims=True)
    beats = (score_c > sc_b) | ((score_c == sc_b) & (ii < jj))
    rank_r = jnp.sum(beats.astype(F32), axis=0, keepdims=True)
    nn = lax.broadcasted_iota(I32, (SEL_PAD, nbp), 0).astype(F32)
    jn = lax.broadcasted_iota(I32, (SEL_PAD, nbp), 1).astype(F32)
    hit = jnp.broadcast_to(rank_r, (SEL_PAD, nbp)) == nn
    idx_c = jnp.sum(jnp.where(hit, jn, 0.0), axis=1, keepdims=True)
    idx_ref[...] = jnp.broadcast_to(idx_c, (SEL_PAD, LANE)).astype(I32)


def nsa_sample_cmp(slopes, ppb3, cmp_kv_s, past, nb_real):
    bd = ppb3.shape[0]
    nbp = cmp_kv_s.shape[3]
    grid_spec = pltpu.PrefetchScalarGridSpec(
        num_scalar_prefetch=1,
        grid=(bd, N_KV),
        in_specs=[pl.BlockSpec((None, 1, GQA_R * HD_A), lambda bi, g, s: (bi, 0, g)),
                  pl.BlockSpec((None, None, None, nbp, HD_A), lambda bi, g, s: (0, g, bi, 0, 0)),
                  pl.BlockSpec((None, None, None, nbp, HD_A), lambda bi, g, s: (1, g, bi, 0, 0))],
        out_specs=[pl.BlockSpec((None, 1, GQA_R * HD_A), lambda bi, g, s: (bi, 0, g)),
                   pl.BlockSpec((None, None, SEL_PAD, LANE), lambda bi, g, s: (bi, g, 0, 0))],
    )
    return pl.pallas_call(
        functools.partial(_nsa_sample_cmp_kernel, past=past, nb_real=nb_real),
        out_shape=[jax.ShapeDtypeStruct((bd, 1, D_A), F32),
                   jax.ShapeDtypeStruct((bd, N_KV, SEL_PAD, LANE), I32)],
        grid_spec=grid_spec,
        compiler_params=_cparams(("parallel", "parallel")),
        name="nsa_sample_cmp",
    )(slopes, ppb3, cmp_kv_s, cmp_kv_s)


W_PAD = LANE


def _nsa_sample_sel_kernel(idx_ref, pt_ref, slopes_ref, q_ref, pool_ref, kn_ref, vn_ref,
                           wbuf_ref, kwn_ref, vwn_ref, gate_ref, oc_ref, o_ref,
                           ksel_ref, vsel_ref, kwin_ref, vwin_ref, *, past):
    bi = pl.program_id(0)
    g = pl.program_id(1)
    n = pl.program_id(2)
    nb_past = past // BLK
    base = (bi * N_KV + g) * N_SEL
    idx_n = idx_ref[base + n]
    r0 = pl.multiple_of(n * BLK, BLK)

    @pl.when(idx_n < nb_past)
    def _():
        ksel_ref[pl.ds(r0, BLK), :] = pool_ref[pl.ds(g, BLK, stride=KV_ROWS), :]
        vsel_ref[pl.ds(r0, BLK), :] = pool_ref[pl.ds(N_KV + g, BLK, stride=KV_ROWS), :]

    @pl.when(idx_n >= nb_past)
    def _():
        first = lax.broadcasted_iota(I32, (BLK, HD_A), 0) == 0
        ksel_ref[pl.ds(r0, BLK), :] = jnp.where(first, kn_ref[...], 0.0)
        vsel_ref[pl.ds(r0, BLK), :] = jnp.where(first, vn_ref[...], 0.0)

    @pl.when(n == N_SEL - 1)
    def _():
        q8 = _heads_on_sublanes(q_ref)
        slope = _slopes_on_sublanes(slopes_ref, g)
        cur = past // BLK
        lane = lax.broadcasted_iota(I32, (1, N_SEL * BLK), 1)
        slot = lane // BLK
        idx_row = jnp.zeros((1, N_SEL * BLK), I32)
        for m in range(N_SEL):
            idx_row = jnp.where(slot == m, idx_ref[base + m], idx_row)
        dist = past - (idx_row * BLK + lane % BLK)
        valid = (idx_row <= cur) & (dist >= 0)
        s = _dot_nt(q8, ksel_ref[...]) * SCALE_A - slope * dist.astype(F32)
        e, den = _softmax_rows(s, valid)
        o_sel = _dot(e / _safe(den), vsel_ref[...])
        wb = wbuf_ref.shape[0] // KV_ROWS
        kwin_ref[0:wb, :] = wbuf_ref[pl.ds(g, wb, stride=KV_ROWS), :]
        vwin_ref[0:wb, :] = wbuf_ref[pl.ds(N_KV + g, wb, stride=KV_ROWS), :]
        first = lax.broadcasted_iota(I32, (W_PAD, HD_A), 0) == 0
        kwin_ref[wb:wb + W_PAD, :] = jnp.where(first, kwn_ref[...], 0.0)
        vwin_ref[wb:wb + W_PAD, :] = jnp.where(first, vwn_ref[...], 0.0)
        lane_w = lax.broadcasted_iota(I32, (1, wb + W_PAD), 1)
        dist_w = wb - lane_w
        valid_w = (dist_w >= 0) & (dist_w < WINDOW)
        s = _dot_nt(q8, kwin_ref[...]) * SCALE_A - slope * dist_w.astype(F32)
        e, den = _softmax_rows(s, valid_w)
        o_win = _dot(e / _safe(den), vwin_ref[...])
        gates = _sigmoid(gate_ref[...])
        for r in range(GQA_R):
            o = (gates[:, 3 * r:3 * r + 1] * oc_ref[:, r * HD_A:(r + 1) * HD_A]
                 + gates[:, 3 * r + 1:3 * r + 2] * o_sel[r:r + 1, :]
                 + gates[:, 3 * r + 2:3 * r + 3] * o_win[r:r + 1, :])
            o_ref[:, r * HD_A:(r + 1) * HD_A] = o.astype(o_ref.dtype)


def nsa_sample_sel(idx_flat, page_table, slopes, ppb3, pp3, pool2d, page0, win2d, seq0, wb, o_cmp, past):
    bd = ppb3.shape[0]
    n_pages = page_table.shape[1]
    bpp = PAGE // BLK
    nb_past = past // BLK
    kvb = A_KV // HD_A

    def pool_map(bi, g, n, idx, pt, s):
        ip = jnp.clip(idx[(bi * N_KV + g) * N_SEL + n], 0, nb_past - 1)
        phys = pt[bi * n_pages + ip // bpp]
        return ((page0 + phys) * bpp + ip % bpp, 0)

    def new_spec(branch, kv):
        return pl.BlockSpec((None, 1, HD_A), lambda bi, g, n, idx, pt, s: (bi, 0, kvb + branch * 4 + kv * 2 + g))

    grid_spec = pltpu.PrefetchScalarGridSpec(
        num_scalar_prefetch=3,
        grid=(bd, N_KV, N_SEL),
        in_specs=[pl.BlockSpec((None, 1, GQA_R * HD_A), lambda bi, g, n, idx, pt, s: (bi, 0, g)),
                  pl.BlockSpec((BLK * KV_ROWS, HD_A), pool_map), new_spec(1, 0), new_spec(1, 1),
                  pl.BlockSpec((wb * KV_ROWS, HD_A), lambda bi, g, n, idx, pt, s: (seq0 + bi, 0)),
                  new_spec(2, 0), new_spec(2, 1),
                  pl.BlockSpec((None, 1, LANE), lambda bi, g, n, idx, pt, s: (bi, 0, A_GATE // LANE + g)),
                  pl.BlockSpec((None, 1, GQA_R * HD_A), lambda bi, g, n, idx, pt, s: (bi, 0, g))],
        out_specs=pl.BlockSpec((None, 1, GQA_R * HD_A), lambda bi, g, n, idx, pt, s: (bi, 0, g)),
        scratch_shapes=[pltpu.VMEM((N_SEL * BLK, HD_A), F32), pltpu.VMEM((N_SEL * BLK, HD_A), F32),
                        pltpu.VMEM((wb + W_PAD, HD_A), F32), pltpu.VMEM((wb + W_PAD, HD_A), F32)],
    )
    return pl.pallas_call(
        functools.partial(_nsa_sample_sel_kernel, past=past),
        out_shape=jax.ShapeDtypeStruct((bd, 1, D_A), BF16),
        grid_spec=grid_spec,
        compiler_params=_cparams(("parallel", "parallel", "arbitrary")),
        name="nsa_sample_sel",
    )(idx_flat, page_table.reshape(-1), slopes, ppb3, pool2d, pp3, pp3, win2d, pp3, pp3, pp3, o_cmp)


def _cumsum_rows(x):
    c = x.shape[0]
    row = lax.broadcasted_iota(I32, x.shape, 0)
    sh = 1
    while sh < c:
        x = x + jnp.where(row >= sh, pltpu.roll(x, sh, 0), 0.0)
        sh *= 2
    return x


def _split2(x):
    hi = x.astype(BF16)
    return hi, (x - hi.astype(F32)).astype(BF16)


def _dot2(a, b, dims=(((1,), (0,)), ((), ()))):
    ah, al = _split2(a)
    bh, bl = _split2(b)
    d = lambda x, y: lax.dot_general(x, y, dims, preferred_element_type=F32)
    return d(ah, bh) + (d(ah, bl) + d(al, bh))


_dot_t = _dot2
NT = (((1,), (1,)), ((), ()))
TN = (((0,), (0,)), ((), ()))
CHUNK_LB = 8
STATE_GROUP = 4


def _rwkv_chunk_kernel(pr_ref, pk_ref, pv_ref, pl_ref, qr_ref, qk_ref, qv_ref, ql_ref,
                       sr_ref, sk_ref, sv_ref, sl_ref,
                       mur_ref, muk_ref, muv_ref, mul_ref, w0_ref, w2_ref, a0_ref, a2_ref, g2_ref,
                       kkp_ref, kap_ref, rkp_ref, gnb_ref,
                       y0_ref, rt_ref, bonus_ref, gate_ref, mm_ref, g0_ref, *, n_valid):
    ci = pl.program_id(2)
    c = pr_ref.shape[0]
    row = lax.broadcasted_iota(I32, (c, 1), 0)
    first_chunk = ci == 0

    def mix(p_ref, q_ref, s_ref, mu_ref):
        p = p_ref[...]
        prev = jnp.where(first_chunk, s_ref[...], q_ref[SUBLANE - 1:SUBLANE, :])
        shifted = jnp.where(row == 0, prev, pltpu.roll(p, 1, 0))
        return p + mu_ref[...] * (shifted - p)

    xr = mix(pr_ref, qr_ref, sr_ref, mur_ref)
    xk = mix(pk_ref, qk_ref, sk_ref, muk_ref)
    xv = mix(pv_ref, qv_ref, sv_ref, muv_ref)
    xl = mix(pl_ref, ql_ref, sl_ref, mul_ref)

    dw = xl[:, 0:LORA_W]
    da = xl[:, LORA_W:LORA_W + LORA_A]
    dg = xl[:, LORA_W + LORA_A:LORA_W + LORA_A + LORA_G]
    wlog = -_softplus(-(w0_ref[...] + _dot(jnp.tanh(dw), w2_ref[...]))) - 0.5
    logdec = -jnp.exp(wlog)
    a = _sigmoid(a0_ref[...] + _dot(da, a2_ref[...]))
    gate_ref[...] = _dot(_sigmoid(dg), g2_ref[...])
    kkv = xk * kkp_ref[...]
    kmod = xk * (1.0 + (a - 1.0) * kap_ref[...])
    if n_valid < c:
        live = row < n_valid
        logdec = jnp.where(live, logdec, 0.0)
        kmod = jnp.where(live, kmod, 0.0)
        a = jnp.where(live, a, 0.0)
        xv = jnp.where(live, xv, 0.0)
    cum = _cumsum_rows(logdec)
    cum_ex = cum - logdec
    cum_end = cum[c - 1:c, :]

    ti = lax.broadcasted_iota(I32, (c, c), 0)
    si = lax.broadcasted_iota(I32, (c, c), 1)
    lower_strict = ti > si
    lower_incl = ti >= si
    eye_c = (ti == si).astype(F32)
    eye_k = (lax.broadcasted_iota(I32, (HD_B, HD_B), 0) == lax.broadcasted_iota(I32, (HD_B, HD_B), 1)).astype(F32)
    rk_all = rkp_ref[...]
    gnb = gnb_ref[...]

    heads = range(pr_ref.shape[1] // HD_B)
    sls = [slice(h * HD_B, (h + 1) * HD_B) for h in heads]
    r_ = [xr[:, s] for s in sls]
    k_ = [kmod[:, s] for s in sls]
    v_ = [xv[:, s] for s in sls]
    kk_ = [kkv[:, s] for s in sls]
    kk_ = [x / jnp.maximum(jnp.sqrt(jnp.sum(x * x, axis=-1, keepdims=True)), 1e-12) for x in kk_]
    b_ = [kk_[h] * a[:, sls[h]] for h in heads]
    kap_ = [kk_[h] * jnp.exp(cum_ex[:, sls[h]]) for h in heads]
    rt_ = [r_[h] * jnp.exp(cum[:, sls[h]]) for h in heads]
    inv_ = [jnp.exp(-cum[:, sls[h]]) for h in heads]
    paired = 2 * c == LANE and len(heads) % 2 == 0
    kb_first = [paired and h % 2 == 0 for h in heads]
    amat = [_dot2(jnp.concatenate([kap_[h], rt_[h]], axis=0),
                  jnp.concatenate([b_[h] * inv_[h], k_[h] * inv_[h]] if kb_first[h]
                                  else [k_[h] * inv_[h], b_[h] * inv_[h]], axis=0), NT) for h in heads]
    kcol = [slice(c, 2 * c) if kb_first[h] else slice(0, c) for h in heads]
    bcol = [slice(0, c) if kb_first[h] else slice(c, 2 * c) for h in heads]
    a_kk = [jnp.where(lower_strict, amat[h][0:c, kcol[h]], 0.0) for h in heads]
    a_rk = [jnp.where(lower_incl, amat[h][c:2 * c, kcol[h]], 0.0) for h in heads]
    a_rb = [jnp.where(lower_incl, amat[h][c:2 * c, bcol[h]], 0.0) for h in heads]
    av = [_dot2(jnp.concatenate([a_kk[h], a_rk[h]], axis=0), v_[h]) for h in heads]

    def level_mask(rows_i, cols_i, w):
        return (rows_i // (2 * w) == cols_i // (2 * w)) & ((rows_i // w) % 2 == 1) & ((cols_i // w) % 2 == 0)

    if paired:
        tp = lax.broadcasted_iota(I32, (c, LANE), 0)
        lane_p = lax.broadcasted_iota(I32, (c, LANE), 1)
        sp = lane_p % c
        left = lane_p < c

        left_b = left.astype(BF16)
        right_b = 1.0 - left_b

        def blockdiag(x):
            return jnp.concatenate([x * left_b, x * right_b], axis=0)

        def dot3p(x_hi, x_lo, y_hi, y_lo):
            d = lambda a_, b_: lax.dot_general(a_, b_, (((1,), (0,)), ((), ())), preferred_element_type=F32)
            return d(x_hi, y_hi) + (d(x_hi, y_lo) + d(x_lo, y_hi))

        pairs = range(len(heads) // 2)
        l_pair = [jnp.where(tp > sp, jnp.where(left, amat[2 * p][0:c, :], amat[2 * p + 1][0:c, :]), 0.0)
                  for p in pairs]
        l_split = [_split2(m) for m in l_pair]
        tpair = [(tp == sp).astype(F32) - jnp.where(((tp % 2) == 1) & (sp == tp - 1), m, 0.0) for m in l_pair]
        w = 2
        while w < c:
            off_b = level_mask(tp, sp, w).astype(BF16)
            t_split = [_split2(m) for m in tpair]
            ld = [dot3p(l_split[p][0] * off_b, l_split[p][1] * off_b,
                        blockdiag(t_split[p][0]), blockdiag(t_split[p][1])) for p in pairs]
            ld_split = [_split2(m) for m in ld]
            tpair = [tpair[p] - dot3p(t_split[p][0], t_split[p][1],
                                      blockdiag(ld_split[p][0]), blockdiag(ld_split[p][1])) for p in pairs]
            w *= 2
        tinv = [tpair[h // 2][:, (h % 2) * c:(h % 2 + 1) * c] for h in heads]
    else:
        a_kb = [jnp.where(lower_strict, amat[h][0:c, bcol[h]], 0.0) for h in heads]
        tinv = [eye_c - jnp.where(((ti % 2) == 1) & (si == ti - 1), m, 0.0) for m in a_kb]
        w = 2
        while w < c:
            off = level_mask(ti, si, w)
            ld = [_dot_t(jnp.where(off, a_kb[h], 0.0), tinv[h]) for h in heads]
            tinv = [tinv[h] - _dot_t(tinv[h], ld[h]) for h in heads]
            w *= 2

    tx = [_dot2(tinv[h], jnp.concatenate([av[h][0:c], kap_[h]], axis=1)) for h in heads]
    arb_tx = [_dot2(a_rb[h], tx[h]) for h in heads]
    dec_end = [jnp.exp(cum_end[:, sls[h]] - cum[:, sls[h]]) for h in heads]
    k_end = [k_[h] * dec_end[h] for h in heads]
    b_end = [b_[h] * dec_end[h] for h in heads]
    g0 = [_dot2(jnp.concatenate([v_[h], -tx[h][:, 0:HD_B]], axis=0),
                jnp.concatenate([k_end[h], b_end[h]], axis=0), TN) for h in heads]
    ktb = [_dot2(tx[h][:, HD_B:2 * HD_B], b_end[h], TN) for h in heads]
    for h in heads:
        s = sls[h]
        y0_ref[:, s] = av[h][c:2 * c] - arb_tx[h][:, 0:HD_B]
        rt_ref[:, s] = rt_[h] - arb_tx[h][:, HD_B:2 * HD_B]
        bonus_ref[:, s] = jnp.sum(r_[h] * k_[h] * rk_all[:, s], axis=-1, keepdims=True) * v_[h] + gnb[:, s]
        mm_ref[h] = eye_k * jnp.exp(cum_end[:, s]) - ktb[h]
        g0_ref[h] = g0[h]


def _rwkv_state_kernel(y0_ref, rt_ref, bonus_ref, gate_ref, mm_ref, g0_ref, s0_ref, gng_ref,
                       o_ref, sT_ref, st_ref):
    @pl.when(pl.program_id(1) == 0)
    def _():
        st_ref[...] = s0_ref[...]

    gng = gng_ref[...]
    for h0 in range(0, N_HEADS_B, STATE_GROUP):
        heads = range(h0, h0 + STATE_GROUP)
        sls = {h: slice(h * HD_B, (h + 1) * HD_B) for h in heads}
        s0 = {h: st_ref[h] for h in heads}
        s_new = {h: _dot2(s0[h], mm_ref[h]) + g0_ref[h] for h in heads}
        y = {h: y0_ref[:, sls[h]] + _dot2(rt_ref[:, sls[h]], s0[h], NT) for h in heads}
        for h in heads:
            st_ref[h] = s_new[h]
            sT_ref[h] = s_new[h]
        mean = {h: jnp.mean(y[h], axis=-1, keepdims=True) for h in heads}
        dev = {h: y[h] - mean[h] for h in heads}
        var = {h: jnp.mean(jnp.square(dev[h]), axis=-1, keepdims=True) for h in heads}
        for h in heads:
            yn = dev[h] * lax.rsqrt(var[h] + GN_EPS) * gng[:, sls[h]]
            o_ref[:, sls[h]] = ((yn + bonus_ref[:, sls[h]]) * gate_ref[:, sls[h]]).astype(o_ref.dtype)


def rwkv7(pp, shift0, wkv0, prm, bsz, t, chunk, n_valid):
    nc = t // chunk
    lb = CHUNK_LB * LANE
    ngrp = D_B // lb
    hs = lb // HD_B
    assert A_RKV % lb == 0
    rb = A_RKV // lb
    per = D_B // lb
    lo_blk = A_LO // (2 * LANE)
    lo3 = 3 * D_B // (2 * LANE)
    sub = chunk // SUBLANE

    def rows(off):
        return pl.BlockSpec((chunk, lb), lambda b, hg, ci: (b * nc + ci, rb + off * per + hg))

    def prev_rows(off):
        return pl.BlockSpec((SUBLANE, lb),
                            lambda b, hg, ci: (jnp.maximum((b * nc + ci) * sub - 1, 0), rb + off * per + hg))

    def srow(off):
        return pl.BlockSpec((None, 1, lb), lambda b, hg, ci: (b, 0, off * per + hg))

    def prow(off):
        return pl.BlockSpec((1, lb), lambda b, hg, ci: (0, off * per + hg))

    def per_head(shape0):
        return pl.BlockSpec((shape0, lb), lambda b, hg, ci: (0, hg))

    lo_spec = lambda rws, imap: pl.BlockSpec((rws, 2 * LANE), imap)
    in_specs = [rows(0), rows(1), rows(2), lo_spec(chunk, lambda b, hg, ci: (b * nc + ci, lo_blk)),
                prev_rows(0), prev_rows(1), prev_rows(2),
                lo_spec(SUBLANE, lambda b, hg, ci: (jnp.maximum((b * nc + ci) * sub - 1, 0), lo_blk)),
                srow(0), srow(1), srow(2), pl.BlockSpec((None, 1, 2 * LANE), lambda b, hg, ci: (b, 0, lo3)),
                prow(0), prow(1), prow(2), lo_spec(1, lambda b, hg, ci: (0, lo3)),
                per_head(1), per_head(LORA_W), per_head(1), per_head(LORA_A), per_head(LORA_G),
                per_head(1), per_head(1), per_head(1), per_head(1)]
    row_out = pl.BlockSpec((chunk, lb), lambda b, hg, ci: (b * nc + ci, hg))
    mat_out = pl.BlockSpec((None, None, hs, HD_B, HD_B), lambda b, hg, ci: (b, ci, hg, 0, 0))
    n = bsz * t
    y0, rt, bonus, gate, mm, g0 = pl.pallas_call(
        functools.partial(_rwkv_chunk_kernel, n_valid=n_valid),
        out_shape=[jax.ShapeDtypeStruct((n, D_B), F32)] * 4
        + [jax.ShapeDtypeStruct((bsz, nc, N_HEADS_B, HD_B, HD_B), F32)] * 2,
        grid=(bsz, ngrp, nc),
        in_specs=in_specs,
        out_specs=[row_out] * 4 + [mat_out] * 2,
        compiler_params=_cparams(("parallel", "parallel", "parallel")),
        name="rwkv_chunk",
    )(pp, pp, pp, pp, pp, pp, pp, pp, shift0, shift0, shift0, shift0,
      prm["mu"], prm["mu"], prm["mu"], prm["mu"], prm["w0"], prm["w2"], prm["a0"], prm["a2"], prm["g2"],
      prm["kk"], prm["ka"], prm["rk"], prm["gn_b"])

    row_in = pl.BlockSpec((chunk, D_B), lambda b, ci: (b * nc + ci, 0))
    mat_in = pl.BlockSpec((None, None, N_HEADS_B, HD_B, HD_B), lambda b, ci: (b, ci, 0, 0, 0))
    state = pl.BlockSpec((None, N_HEADS_B, HD_B, HD_B), lambda b, ci: (b, 0, 0, 0))
    o_b, s_fin = pl.pallas_call(
        _rwkv_state_kernel,
        out_shape=[jax.ShapeDtypeStruct((n, D_B), BF16),
                   jax.ShapeDtypeStruct((bsz, N_HEADS_B, HD_B, HD_B), F32)],
        grid=(bsz, nc),
        in_specs=[row_in, row_in, row_in, row_in, mat_in, mat_in, state,
                  pl.BlockSpec((1, D_B), lambda b, ci: (0, 0))],
        out_specs=[row_in, state],
        scratch_shapes=[pltpu.VMEM((N_HEADS_B, HD_B, HD_B), F32)],
        compiler_params=_cparams(("parallel", "arbitrary")),
        name="rwkv_state",
    )(y0, rt, bonus, gate, mm, g0, wkv0, prm["gn_g"])
    return o_b, s_fin


def _first_lane(cond, lane):
    return jnp.min(jnp.where(cond, lane, 4 * LANE), axis=-1, keepdims=True)


def _router_kernel(h_ref, w_ref, b_ref, eid_ref, wt_ref):
    logits = _dot3(h_ref[...], w_ref[...]) + b_ref[...]
    lane = lax.broadcasted_iota(I32, logits.shape, 1)
    gmask = lane < N_GROUPS
    lg = jnp.where(gmask, logits, NEG)
    eg = jnp.where(gmask, jnp.exp(lg - jnp.max(lg, axis=-1, keepdims=True)), 0.0)
    gp = eg / jnp.sum(eg, axis=-1, keepdims=True)
    g_w = jnp.max(gp, axis=-1, keepdims=True)
    grp = _first_lane(gmask & (gp == g_w), lane)
    lo = N_GROUPS + grp * E_PER_GROUP
    emask = (lane >= lo) & (lane < lo + E_PER_GROUP)
    le = jnp.where(emask, logits, NEG)
    ee = jnp.where(emask, jnp.exp(le - jnp.max(le, axis=-1, keepdims=True)), 0.0)
    ep = ee / jnp.sum(ee, axis=-1, keepdims=True)
    p1 = jnp.max(jnp.where(emask, ep, -1.0), axis=-1, keepdims=True)
    i1 = _first_lane(emask & (ep == p1), lane)
    rest = emask & (lane != i1)
    p2 = jnp.max(jnp.where(rest, ep, -1.0), axis=-1, keepdims=True)
    i2 = _first_lane(rest & (ep == p2), lane)
    tot = p1 + p2
    eid_ref[...] = jnp.where(lane == 0, i1 - N_GROUPS, jnp.where(lane == 1, i2 - N_GROUPS, 0))
    wt_ref[...] = jnp.where(lane == 0, g_w * p1 / tot, jnp.where(lane == 1, g_w * p2 / tot, 0.0))


def router(h2, wr, br):
    n, d = h2.shape
    tm = _row_tile(n, 512)
    return pl.pallas_call(
        _router_kernel,
        out_shape=[jax.ShapeDtypeStruct((n, LANE), I32), jax.ShapeDtypeStruct((n, LANE), F32)],
        grid=(n // tm,),
        in_specs=[pl.BlockSpec((tm, d), lambda i: (i, 0)),
                  pl.BlockSpec((d, LANE), lambda i: (0, 0)),
                  pl.BlockSpec((1, LANE), lambda i: (0, 0))],
        out_specs=[pl.BlockSpec((tm, LANE), lambda i: (i, 0)), pl.BlockSpec((tm, LANE), lambda i: (i, 0))],
        compiler_params=_cparams(("parallel",)),
        name="router",
    )(h2, wr, br)


def _row_copy(src_hbm, dst_vmem, sem, src_row, dst_row):
    return pltpu.make_async_copy(src_hbm.at[pl.ds(src_row, 1), :], dst_vmem.at[pl.ds(dst_row, 1), :], sem)


def _experts_kernel(blk_e_ref, n_used_ref, tok_ref, h_hbm, w1_ref, w3_ref, w2_ref, y_ref,
                    x_even, x_odd, w1b_ref, w3b_ref, w2b_ref, sem):
    i = pl.program_id(0)
    bm = x_even.shape[0]
    n_used = n_used_ref[0]
    bufs = (x_even, x_odd)

    def rows(blk, slot, go):
        for r in range(bm):
            go(_row_copy(h_hbm, bufs[slot], sem.at[slot], tok_ref[blk * bm + r], r))

    def block(slot):
        @pl.when(i == 0)
        def _():
            rows(0, 0, lambda cp: cp.start())

        rows(i, slot, lambda cp: cp.wait())

        @pl.when((i == 0) | (blk_e_ref[i] != blk_e_ref[jnp.maximum(i - 1, 0)]))
        def _():
            w1b_ref[...] = w1_ref[...].astype(BF16)
            w3b_ref[...] = w3_ref[...].astype(BF16)
            w2b_ref[...] = w2_ref[...].astype(BF16)

        nxt = jnp.minimum(i + 1, n_used - 1)
        rows(nxt, 1 - slot, lambda cp: cp.start())
        x = bufs[slot][...].astype(BF16)
        h1 = _dot(x, w1b_ref[...])
        h3 = _dot(x, w3b_ref[...])
        act = h1 * _sigmoid(h1) * h3
        y_ref[...] = _dot(act, w2b_ref[...])

        @pl.when(i == n_used - 1)
        def _():
            rows(nxt, 1 - slot, lambda cp: cp.wait())

    for slot in range(2):
        pl.when((i < n_used) & (i % 2 == slot))(functools.partial(block, slot))

    @pl.when(i >= n_used)
    def _():
        y_ref[...] = jnp.zeros_like(y_ref)


def experts(h_all, blk_e, n_used, row_tok, w1, w3, w2, l, bm):
    n_blocks = blk_e.shape[0]
    d = h_all.shape[1]
    grid_spec = pltpu.PrefetchScalarGridSpec(
        num_scalar_prefetch=3,
        grid=(n_blocks,),
        in_specs=[pl.BlockSpec(memory_space=pl.ANY),
                  pl.BlockSpec((None, None, d, D_EXPERT), lambda i, be, nu, tk: (l, be[i], 0, 0)),
                  pl.BlockSpec((None, None, d, D_EXPERT), lambda i, be, nu, tk: (l, be[i], 0, 0)),
                  pl.BlockSpec((None, None, D_EXPERT, d), lambda i, be, nu, tk: (l, be[i], 0, 0))],
        out_specs=pl.BlockSpec((bm, d), lambda i, be, nu, tk: (i, 0)),
        scratch_shapes=[pltpu.VMEM((bm, d), F32), pltpu.VMEM((bm, d), F32), pltpu.VMEM((d, D_EXPERT), BF16),
                        pltpu.VMEM((d, D_EXPERT), BF16), pltpu.VMEM((D_EXPERT, d), BF16),
                        pltpu.SemaphoreType.DMA((2,))],
    )
    return pl.pallas_call(
        _experts_kernel,
        out_shape=jax.ShapeDtypeStruct((n_blocks * bm, d), F32),
        grid_spec=grid_spec,
        compiler_params=_cparams(("arbitrary",)),
        name="experts",
    )(blk_e, n_used, row_tok, h_all, w1, w3, w2)


def _combine_kernel(dest_ref, ys_hbm, wt_ref, x_ref, g_ref, o_ref, buf, sem):
    tm = x_ref.shape[0]
    step = pl.program_id(0) * pl.num_programs(1) + pl.program_id(1)
    n_steps = pl.num_programs(0) * pl.num_programs(1)

    def rows(tile, slot, go):
        for r in range(tm):
            base = (tile * tm + r) * 2
            go(_row_copy(ys_hbm, buf.at[slot, 0], sem.at[slot], dest_ref[base], r))
            go(_row_copy(ys_hbm, buf.at[slot, 1], sem.at[slot], dest_ref[base + 1], r))

    slot = step % 2

    @pl.when(step == 0)
    def _():
        rows(0, 0, lambda cp: cp.start())

    rows(step, slot, lambda cp: cp.wait())
    nxt = jnp.minimum(step + 1, n_steps - 1)
    rows(nxt, 1 - slot, lambda cp: cp.start())
    wt = wt_ref[...]
    moe = wt[:, 0:1] * buf[slot, 0] + wt[:, 1:2] * buf[slot, 1]
    o_ref[...] = x_ref[...] + g_ref[...] * moe

    @pl.when(step == n_steps - 1)
    def _():
        rows(nxt, 1 - slot, lambda cp: cp.wait())


def combine(ys, dest, wts, x, gate):
    b, t, d = x.shape
    tm = _row_tile(t, 128)
    nt = t // tm
    grid_spec = pltpu.PrefetchScalarGridSpec(
        num_scalar_prefetch=1,
        grid=(b, nt),
        in_specs=[pl.BlockSpec(memory_space=pl.ANY),
                  pl.BlockSpec((None, tm, LANE), lambda i, j, ds: (i, j, 0)),
                  pl.BlockSpec((None, tm, d), lambda i, j, ds: (i, j, 0)),
                  pl.BlockSpec((None, 1, d), lambda i, j, ds: (i, 0, 0))],
        out_specs=pl.BlockSpec((None, tm, d), lambda i, j, ds: (i, j, 0)),
        scratch_shapes=[pltpu.VMEM((2, 2, tm, d), F32), pltpu.SemaphoreType.DMA((2,))],
    )
    return pl.pallas_call(
        _combine_kernel,
        out_shape=jax.ShapeDtypeStruct((b, t, d), F32),
        grid_spec=grid_spec,
        compiler_params=_cparams(("arbitrary", "arbitrary")),
        name="moe_combine",
    )(dest.reshape(-1), ys, wts.reshape(b, t, LANE), x, gate.reshape(b, 1, d))


def _dispatch_tables(eid, bm):
    n = eid.shape[0]
    nk = n * 2
    n_blocks = (nk + N_EXPERTS * (bm - 1) + bm - 1) // bm
    flat_e = eid.reshape(-1)
    onehot = (flat_e[:, None] == jnp.arange(N_EXPERTS, dtype=I32)[None, :]).astype(I32)
    csum = jnp.cumsum(onehot, axis=0)
    counts = csum[-1]
    rank = jnp.sum(onehot * (csum - 1), axis=1)
    padded = (counts + bm - 1) // bm * bm
    pad_end = jnp.cumsum(padded)
    pad_start = pad_end - padded
    dest = (pad_start[flat_e] + rank).astype(I32)
    row_tok = jnp.zeros((n_blocks * bm,), I32).at[dest].set(jnp.arange(nk, dtype=I32) // 2)
    blk_start = jnp.arange(n_blocks, dtype=I32) * bm
    blk_e = jnp.minimum(jnp.sum((pad_end[None, :] <= blk_start[:, None]).astype(I32), axis=1), N_EXPERTS - 1)
    n_used = (pad_end[-1] // bm).astype(I32).reshape(1)
    return blk_e, n_used, row_tok, dest.reshape(n, 2)


def _final_norm_kernel(x_ref, g_ref, o_ref):
    x = x_ref[...]
    o_ref[...] = x * lax.rsqrt(jnp.mean(x * x, axis=-1, keepdims=True) + RMS_EPS) * g_ref[...]


def final_norm(x, g):
    b, t, d = x.shape
    tr = min(t, 256)
    return pl.pallas_call(
        _final_norm_kernel,
        out_shape=jax.ShapeDtypeStruct((b, t, d), F32),
        grid=(b, t // tr),
        in_specs=[pl.BlockSpec((None, tr, d), lambda i, j: (i, j, 0)), pl.BlockSpec((1, d), lambda i, j: (0, 0))],
        out_specs=pl.BlockSpec((None, tr, d), lambda i, j: (i, j, 0)),
        compiler_params=_cparams(("parallel", "parallel")),
        name="final_norm",
    )(x, g.reshape(1, d))


def _align_in_cols(w, axis):
    def take(a, b_):
        return lax.slice_in_dim(w, a, b_, axis=axis)

    def zeros(nz):
        shp = list(w.shape)
        shp[axis] = nz
        return jnp.zeros(shp, w.dtype)

    gate_parts = []
    for g in range(N_KV):
        gate_parts += [take(OFF_GATE_A + g * 3 * GQA_R, OFF_GATE_A + (g + 1) * 3 * GQA_R), zeros(LANE - 3 * GQA_R)]
    parts = [take(0, OFF_GATE_A)] + gate_parts + [take(OFF_RWKV + 3 * D_B, OFF_MERGE), zeros(RW_W - SHIFT_W),
                                                   take(OFF_RWKV, OFF_RWKV + 3 * D_B),
                                                   take(OFF_MERGE, OFF_MERGE + 2 * D_MODEL)]
    return jnp.concatenate(parts, axis=axis)


def _rwkv_cols(p):
    return jnp.concatenate([p[..., A_RKV:A_RKV + 3 * D_B], p[..., A_LO:A_LO + SHIFT_W - 3 * D_B]], axis=-1)


def _pad_lanes(v, width):
    return jnp.pad(v, [(0, 0)] * (v.ndim - 1) + [(0, width - v.shape[-1])])


def kernel(x_prompt, x_sample, cache_cmp, cache_slc, cache_win, state_shift, state_wkv, page_table, c_prompt, c_sample, ln1_g, ln2_g, ada_w, ada_b, w_in, cmp_w1, cmp_b1, cmp_w2, rwkv_mu, rwkv_w0, rwkv_w2, rwkv_a0, rwkv_a2, rwkv_g2, rwkv_kk, rwkv_ka, rwkv_rk, rwkv_gn_g, rwkv_gn_b, w_branch, w_out, router_g_w, router_g_b, router_e_w, router_e_b, exp_w1, exp_w3, exp_w2, final_g):
    depth = w_in.shape[0]
    bp, t, d = x_prompt.shape
    bd, ts, _ = x_sample.shape
    assert ts == 1 and t % KC == 0 and t >= WINDOW + QB
    n_pages = page_table.shape[1]
    past = n_pages * PAGE
    wb = cache_win.shape[2]
    n_phys = cache_cmp.shape[1]
    n_p = bp * t

    slopes = jnp.exp2(-8.0 * (jnp.arange(N_HEADS_A, dtype=F32) + 1.0) / N_HEADS_A)
    c_rows = bp + bd
    c16 = jnp.zeros(((c_rows + SUBLANE - 1) // SUBLANE * SUBLANE, d), F32).at[:bp].set(c_prompt).at[bp:c_rows].set(c_sample)
    mod = adaln(c16, ada_w, ada_b).reshape(depth, c16.shape[0], 6, d)

    xp, xs = x_prompt, x_sample
    outs = {k: [] for k in ("cmp_p", "slc_p", "win_p", "shf_p", "wkv_p", "cmp_s", "slc_s", "win_s", "shf_s", "wkv_s")}
    pool_cmp2d = cache_cmp.reshape(-1, HD_A)
    pool_slc2d = cache_slc.reshape(-1, HD_A)
    win2d = cache_win.reshape(-1, HD_A)

    for l in range(depth):
        mp, ms = mod[l, :bp], mod[l, bp:c_rows]
        w_in_al = _align_in_cols(w_in[l], 1).astype(BF16)
        w1b = cmp_w1[l].astype(BF16)
        w_out_b = w_out[l].astype(BF16)
        prm = {
            "mu": _pad_lanes(rwkv_mu[l][None, :], RW_W),
            "w0": rwkv_w0[l][None, :], "w2": rwkv_w2[l], "a0": rwkv_a0[l][None, :], "a2": rwkv_a2[l],
            "g2": rwkv_g2[l], "kk": rwkv_kk[l][None, :], "ka": rwkv_ka[l][None, :],
            "rk": rwkv_rk[l].reshape(1, D_B), "gn_g": rwkv_gn_g[l][None, :], "gn_b": rwkv_gn_b[l][None, :],
        }
        wr = _pad_lanes(jnp.concatenate([router_g_w[l], router_e_w[l]], axis=1), LANE)
        br = _pad_lanes(jnp.concatenate([router_g_b[l], router_e_b[l]])[None, :], LANE)

        hp_ = norm_mod(xp, ln1_g[l], mp[:, 0], mp[:, 1], BF16).reshape(n_p, d)
        pp, ppb, kv_p = in_proj(hp_, w_in_al)
        cmp_kv = compress(pp, A_KV, 1, w1b, cmp_b1[l], cmp_w2[l])
        o_a = nsa_prompt(slopes, ppb, pp, cmp_kv, bp, t)
        o_b, s_fin = rwkv7(pp, jnp.zeros((bp, 1, RW_W), F32), jnp.zeros((bp, N_HEADS_B, HD_B, HD_B), F32),
                           prm, bp, t, 64, 64)
        merged = branch_merge(o_a, o_b, w_branch, l, pp)
        xp = out_proj_residual(merged.reshape(bp, t, d), w_out_b, xp, mp[:, 2])
        kv_p = kv_p.reshape(3, bp, t, 2, N_KV, HD_A)
        outs["cmp_p"].append(kv_p[0])
        outs["slc_p"].append(kv_p[1])
        outs["win_p"].append(kv_p[2, :, t - min(WINDOW, t):])
        outs["shf_p"].append(_rwkv_cols(pp.reshape(bp, t, IN_AL)[:, -1]))
        outs["wkv_p"].append(s_fin)

        hs_ = norm_mod(xs, ln1_g[l], ms[:, 0], ms[:, 1], BF16).reshape(bd, d)
        ps, psb, kv_s = in_proj(hs_, w_in_al)
        cmp_past = compress_paged(page_table, pool_cmp2d, l * n_phys, w1b, cmp_b1[l], cmp_w2[l])
        new_rows = jnp.zeros((bd, BLK, 2 * KV_W), F32).at[:, 0].set(ps[:, A_KV:A_KV + 2 * KV_W])
        cmp_new = compress(new_rows.reshape(bd * BLK, 2 * KV_W), 0, 1, w1b, cmp_b1[l], cmp_w2[l])
        nb_past = past // BLK
        nb_real = nb_past + 1
        nbp = (nb_real + LANE - 1) // LANE * LANE
        cmp_s = jnp.concatenate([cmp_past.reshape(2, N_KV, bd, nb_past, HD_A), cmp_new[:, :, :, None, :],
                                 jnp.zeros((2, N_KV, bd, nbp - nb_real, HD_A), F32)], axis=3)
        ps3 = ps.reshape(bd, 1, IN_AL)
        psb3 = psb.reshape(bd, 1, A_GATE)
        o_cmp, idx_full = nsa_sample_cmp(slopes, psb3, cmp_s, past, nb_real)
        idx_flat = idx_full[:, :, :N_SEL, 0].reshape(-1)
        o_a_s = nsa_sample_sel(idx_flat, page_table, slopes, psb3, ps3, pool_slc2d, l * n_phys,
                               win2d, l * bd, wb, o_cmp, past).reshape(bd, D_A)
        ps_pad = jnp.zeros((bd, SUBLANE, IN_AL), F32).at[:, 0].set(ps).reshape(bd * SUBLANE, IN_AL)
        o_b_s, s_fin_s = rwkv7(ps_pad, _pad_lanes(state_shift[l], RW_W)[:, None, :], state_wkv[l],
                               prm, bd, SUBLANE, SUBLANE, 1)
        o_b_s = o_b_s.reshape(bd, SUBLANE, D_B)[:, 0]
        merged_s = branch_merge(o_a_s, o_b_s, w_branch, l, ps)
        xs = out_proj_residual(merged_s.reshape(bd, 1, d), w_out_b, xs, ms[:, 2])
        kv_s = kv_s.reshape(3, bd, 1, 2, N_KV, HD_A)
        outs["cmp_s"].append(kv_s[0])
        outs["slc_s"].append(kv_s[1])
        outs["win_s"].append(jnp.concatenate([cache_win[l, :, 1:], kv_s[2]], axis=1))
        outs["shf_s"].append(_rwkv_cols(ps))
        outs["wkv_s"].append(s_fin_s)

        h2p = norm_mod(xp, ln2_g[l], mp[:, 3], mp[:, 4], F32).reshape(n_p, d)
        h2s = norm_mod(xs, ln2_g[l], ms[:, 3], ms[:, 4], F32).reshape(bd, d)
        eid_p, wt_p = router(h2p, wr, br)
        eid_s, wt_s = router(h2s, wr, br)
        h_all = jnp.concatenate([h2p, h2s], axis=0)
        eid = jnp.concatenate([eid_p[:, :2], eid_s[:, :2]], axis=0)
        blk_e, n_used, row_tok, dest = _dispatch_tables(eid, MOE_BLOCK)
        ys = experts(h_all, blk_e, n_used, row_tok, exp_w1, exp_w3, exp_w2, l, MOE_BLOCK)
        xp = combine(ys, dest[:n_p], wt_p, xp, mp[:, 5])
        xs = combine(ys, dest[n_p:], wt_s, xs, ms[:, 5])

    y_prompt = final_norm(xp, final_g)
    y_sample = final_norm(xs, final_g)
    st = lambda k: jnp.stack(outs[k])
    return (y_prompt, y_sample, st("cmp_p"), st("slc_p"), st("win_p"), st("shf_p"), st("wkv_p"),
            st("cmp_s"), st("slc_s"), st("win_s"), st("shf_s"), st("wkv_s"))
```

```python
import functools

import jax
import jax.numpy as jnp
from jax import lax
from jax.experimental import pallas as pl
from jax.experimental.pallas import tpu as pltpu

F32 = jnp.float32
BF16 = jnp.bfloat16
I32 = jnp.int32

LANE = 128
SUBLANE = 8
VMEM_LIMIT = 56 * 1024 * 1024

D_MODEL = 2048
HD_A = 128
N_HEADS_A = 8
N_KV = 2
GQA_R = 4
BLK = 64
N_SEL = 16
WINDOW = 512
CMP_HID = 256
SCALE_A = HD_A ** -0.5
D_A = 1024
D_B = 1024
HD_B = 64
N_HEADS_B = 16
LORA_W, LORA_A, LORA_G = 64, 64, 32
GN_EPS = HD_B * 1e-5
N_GROUPS = 4
E_PER_GROUP = 8
N_EXPERTS = 32
D_EXPERT = 512
MOE_BLOCK = 128
RMS_EPS = 1e-6
PAGE = 128

Q_W = N_HEADS_A * HD_A
KV_W = N_KV * HD_A
OFF_KV = Q_W
OFF_GATE_A = OFF_KV + 6 * KV_W
OFF_RWKV = OFF_GATE_A + 3 * N_HEADS_A
SHIFT_W = 3 * D_B + LORA_W + LORA_A + LORA_G
OFF_MERGE = OFF_RWKV + SHIFT_W

A_KV = Q_W
A_GATE = A_KV + 6 * KV_W
A_LO = A_GATE + N_KV * LANE
A_RKV = A_LO + 2 * LANE
RW_W = 3 * D_B + 2 * LANE
A_MERGE = A_RKV + 3 * D_B
IN_AL = A_MERGE + 2 * D_MODEL
NEG = -1e30


def _cparams(sem):
    return pltpu.CompilerParams(dimension_semantics=sem, vmem_limit_bytes=VMEM_LIMIT)


def _dot(a, b, dims=(((1,), (0,)), ((), ()))):
    return lax.dot_general(a.astype(BF16), b.astype(BF16), dims, preferred_element_type=F32)


def _dot_nt(a, b):
    return _dot(a, b, (((1,), (1,)), ((), ())))


def _dot_tn(a, b):
    return _dot(a, b, (((0,), (0,)), ((), ())))


def _split3(x):
    h = x.astype(BF16)
    r1 = x - h.astype(F32)
    m = r1.astype(BF16)
    lo = (r1 - m.astype(F32)).astype(BF16)
    return h, m, lo


def _dot3(a, b, dims=(((1,), (0,)), ((), ()))):
    ah, am, al = _split3(a)
    bh, bm, bl = _split3(b)
    d = lambda x, y: lax.dot_general(x, y, dims, preferred_element_type=F32)
    return (d(ah, bh) + (d(ah, bm) + d(am, bh))) + ((d(am, bm) + d(ah, bl)) + d(al, bh))


def _sigmoid(x):
    return 1.0 / (1.0 + jnp.exp(-x))


def _softplus(x):
    return jnp.maximum(x, 0.0) + jnp.log(1.0 + jnp.exp(-jnp.abs(x)))


def _adaln_kernel(c_ref, w_ref, b_ref, o_ref):
    c = c_ref[...]
    h = c * _sigmoid(c)
    o_ref[...] = _dot(h, w_ref[...]) + b_ref[...]


def adaln(c16, ada_w, ada_b):
    depth, d, n = ada_w.shape
    tn = 1024
    return pl.pallas_call(
        _adaln_kernel,
        out_shape=jax.ShapeDtypeStruct((depth, c16.shape[0], n), F32),
        grid=(depth, n // tn),
        in_specs=[pl.BlockSpec(c16.shape, lambda l, j: (0, 0)),
                  pl.BlockSpec((None, d, tn), lambda l, j: (l, 0, j)),
                  pl.BlockSpec((None, 1, tn), lambda l, j: (l, 0, j))],
        out_specs=pl.BlockSpec((None, c16.shape[0], tn), lambda l, j: (l, 0, j)),
        compiler_params=_cparams(("parallel", "parallel")),
        name="adaln",
    )(c16, ada_w, ada_b.reshape(depth, 1, n))


ROW_TILES = D_MODEL // LANE


def _to_row_tiles(ref, val):
    rows = val.shape[0]
    for c in range(ROW_TILES):
        ref[pl.ds(c, rows, stride=ROW_TILES), :] = val[:, c * LANE:(c + 1) * LANE]


def _from_row_tiles(ref, rows, c):
    return ref[pl.ds(c, rows, stride=ROW_TILES), :]


def _norm_kernel(x_ref, g_ref, sh_ref, sc_ref, o_ref):
    x = x_ref[...]
    y = x * lax.rsqrt(jnp.mean(x * x, axis=-1, keepdims=True) + RMS_EPS)
    y = y * g_ref[...]
    o_ref[...] = (y * (1.0 + sc_ref[...]) + sh_ref[...]).astype(o_ref.dtype)


def _norm_tiles_kernel(x_ref, g_ref, sh_ref, sc_ref, *refs):
    o_ref, tiles_ref = refs[-2:]
    _norm_kernel(x_ref, g_ref, sh_ref, sc_ref, o_ref)
    _to_row_tiles(tiles_ref, o_ref[...])


def norm_mod(x, g, shift, scale, out_dtype, tiles_rows=0, tiles_into=None, tiles_row0=0):
    b, t, d = x.shape
    tr = min(t, 256)
    nt = t // tr
    in_specs = [pl.BlockSpec((None, tr, d), lambda i, j: (i, j, 0)),
                pl.BlockSpec((1, d), lambda i, j: (0, 0)),
                pl.BlockSpec((None, 1, d), lambda i, j: (i, 0, 0)),
                pl.BlockSpec((None, 1, d), lambda i, j: (i, 0, 0))]
    args = [x, g.reshape(1, d), shift.reshape(b, 1, d), scale.reshape(b, 1, d)]
    row_spec = pl.BlockSpec((None, tr, d), lambda i, j: (i, j, 0))
    if not tiles_rows:
        return pl.pallas_call(
            _norm_kernel, out_shape=jax.ShapeDtypeStruct((b, t, d), out_dtype), grid=(b, nt),
            in_specs=in_specs, out_specs=row_spec,
            compiler_params=_cparams(("parallel", "parallel")), name="norm_mod")(*args)
    assert out_dtype == F32 and tiles_row0 % tr == 0
    blk0 = tiles_row0 // tr
    aliases = {}
    if tiles_into is not None:
        in_specs.append(pl.BlockSpec(memory_space=pl.ANY))
        args.append(tiles_into)
        aliases = {len(args) - 1: 1}
    return pl.pallas_call(
        _norm_tiles_kernel,
        out_shape=[jax.ShapeDtypeStruct((b, t, d), F32),
                   jax.ShapeDtypeStruct((tiles_rows * ROW_TILES, LANE), F32)],
        grid=(b, nt),
        in_specs=in_specs,
        out_specs=[row_spec, pl.BlockSpec((tr * ROW_TILES, LANE), lambda i, j: (blk0 + i * nt + j, 0))],
        input_output_aliases=aliases,
        compiler_params=_cparams(("parallel", "parallel")),
        name="norm_tiles",
    )(*args)


IN_TN = 2 * KV_W
N_BF16_TILES = A_GATE // IN_TN
KV_TILE0 = A_KV // IN_TN
KV_ROWS = 2 * N_KV


def _in_proj_kernel(a_ref, w_ref, o_ref, ob_ref, kv_ref):
    j = pl.program_id(1)
    acc = _dot(a_ref[...], w_ref[...])
    o_ref[...] = acc

    @pl.when(j < N_BF16_TILES)
    def _():
        ob_ref[...] = acc.astype(BF16)

    @pl.when((j >= KV_TILE0) & (j < N_BF16_TILES))
    def _():
        tm = acc.shape[0]
        for c4 in range(KV_ROWS):
            kv_ref[pl.ds(c4, tm, stride=KV_ROWS), :] = acc[:, c4 * HD_A:(c4 + 1) * HD_A]


def _row_tile(m, cap):
    return m if m <= cap else cap


def in_proj(a, w):
    m, k = a.shape
    tm = _row_tile(m, 2048)
    tn = IN_TN
    assert m % tm == 0
    return pl.pallas_call(
        _in_proj_kernel,
        out_shape=[jax.ShapeDtypeStruct((m, IN_AL), F32), jax.ShapeDtypeStruct((m, A_GATE), BF16),
                   jax.ShapeDtypeStruct((3, m * KV_ROWS, HD_A), F32)],
        grid=(m // tm, IN_AL // tn),
        in_specs=[pl.BlockSpec((tm, k), lambda i, j: (i, 0)),
                  pl.BlockSpec((k, tn), lambda i, j: (0, j))],
        out_specs=[pl.BlockSpec((tm, tn), lambda i, j: (i, j)),
                   pl.BlockSpec((tm, tn), lambda i, j: (i, jnp.minimum(j, N_BF16_TILES - 1))),
                   pl.BlockSpec((None, tm * KV_ROWS, HD_A),
                                lambda i, j: (jnp.clip(j - KV_TILE0, 0, 2), i, 0))],
        compiler_params=_cparams(("parallel", "arbitrary")),
        name="proj_in",
    )(a, w)


def _branch_kernel(oa_ref, ob_ref, wa_ref, wb_ref, ga_ref, gb_ref, o_ref):
    ya = _dot(oa_ref[...], wa_ref[...])
    yb = _dot(ob_ref[...], wb_ref[...])
    o_ref[...] = (_sigmoid(ga_ref[...]) * ya + _sigmoid(gb_ref[...]) * yb).astype(o_ref.dtype)


def branch_merge(o_a, o_b, w_branch, l, pp):
    m = o_a.shape[0]
    tm = _row_tile(m, 1024)
    tn = 512
    gcol = A_MERGE // tn
    return pl.pallas_call(
        _branch_kernel,
        out_shape=jax.ShapeDtypeStruct((m, D_MODEL), BF16),
        grid=(m // tm, D_MODEL // tn),
        in_specs=[pl.BlockSpec((tm, D_A), lambda i, j: (i, 0)),
                  pl.BlockSpec((tm, D_B), lambda i, j: (i, 0)),
                  pl.BlockSpec((None, D_A, tn), lambda i, j: (l, 0, j)),
                  pl.BlockSpec((None, D_B, tn), lambda i, j: (l, D_A // D_B, j)),
                  pl.BlockSpec((tm, tn), lambda i, j: (i, gcol + j)),
                  pl.BlockSpec((tm, tn), lambda i, j: (i, gcol + D_MODEL // tn + j))],
        out_specs=pl.BlockSpec((tm, tn), lambda i, j: (i, j)),
        compiler_params=_cparams(("parallel", "parallel")),
        name="branch_merge",
    )(o_a, o_b, w_branch, w_branch, pp, pp)


def _resid_kernel(m_ref, w_ref, x_ref, g_ref, o_ref):
    o_ref[...] = x_ref[...] + g_ref[...] * _dot(m_ref[...], w_ref[...])


def out_proj_residual(merged, w_out_l, x, gate):
    b, t, d = x.shape
    tm = _row_tile(t, 2048)
    tn = 512
    nt = t // tm
    return pl.pallas_call(
        _resid_kernel,
        out_shape=jax.ShapeDtypeStruct((b, t, d), F32),
        grid=(b, nt, d // tn),
        in_specs=[pl.BlockSpec((None, tm, d), lambda i, r, j: (i, r, 0)),
                  pl.BlockSpec((d, tn), lambda i, r, j: (0, j)),
                  pl.BlockSpec((None, tm, tn), lambda i, r, j: (i, r, j)),
                  pl.BlockSpec((None, 1, tn), lambda i, r, j: (i, 0, j))],
        out_specs=pl.BlockSpec((None, tm, tn), lambda i, r, j: (i, r, j)),
        compiler_params=_cparams(("parallel", "parallel", "parallel")),
        name="out_proj",
    )(merged, w_out_l, x, gate.reshape(b, 1, d))


T_PER_STEP = 8


def _gelu_tanh(x):
    return 0.5 * x * (1.0 + jnp.tanh(0.7978845608028654 * (x + 0.044715 * x * x * x)))


def _compress_kernel(x_ref, w1_ref, b1_ref, w2_ref, o_ref, acc_ref):
    tc = pl.program_id(3)
    nblk = o_ref.shape[0]

    @pl.when(tc == 0)
    def _():
        acc_ref[...] = jnp.zeros_like(acc_ref)

    acc = acc_ref[...]
    for tl in range(T_PER_STEP):
        t = tc * T_PER_STEP + tl
        xt = x_ref[pl.ds(t, nblk, stride=BLK), :]
        acc = acc + _dot(xt, w1_ref[tl * HD_A:(tl + 1) * HD_A, :])
    acc_ref[...] = acc

    @pl.when(tc == pl.num_programs(3) - 1)
    def _():
        h = _gelu_tanh(acc + b1_ref[...])
        o_ref[...] = _dot(h, w2_ref[...])


def compress(rows, col0, n_slabs, w1b, b1, w2):
    r = rows.shape[0] // n_slabs
    nblk = r // BLK
    cb = col0 // HD_A
    return pl.pallas_call(
        _compress_kernel,
        out_shape=jax.ShapeDtypeStruct((2, N_KV, n_slabs * nblk, HD_A), F32),
        grid=(2, N_KV, n_slabs, BLK // T_PER_STEP),
        in_specs=[pl.BlockSpec((r, HD_A), lambda kv, g, s, tc: (s, cb + kv * N_KV + g)),
                  pl.BlockSpec((None, T_PER_STEP * HD_A, CMP_HID), lambda kv, g, s, tc: (kv, tc, 0)),
                  pl.BlockSpec((None, 1, CMP_HID), lambda kv, g, s, tc: (kv, 0, 0)),
                  pl.BlockSpec((None, CMP_HID, HD_A), lambda kv, g, s, tc: (kv, 0, 0))],
        out_specs=pl.BlockSpec((None, None, nblk, HD_A), lambda kv, g, s, tc: (kv, g, s, 0)),
        scratch_shapes=[pltpu.VMEM((nblk, CMP_HID), F32)],
        compiler_params=_cparams(("parallel", "parallel", "parallel", "arbitrary")),
        name="compress",
    )(rows, w1b, b1.reshape(2, 1, CMP_HID), w2)


QB = 128
KC = 512


def _softmax_cols(s, valid):
    sm = jnp.where(valid, s, NEG)
    m = jnp.max(sm, axis=0, keepdims=True)
    e = jnp.where(valid, jnp.exp(sm - m), 0.0)
    return e, jnp.sum(e, axis=0, keepdims=True)


def _safe(den):
    return jnp.where(den > 0, den, 1.0)


def _select_mask_t(imp, jblk, cur):
    nb = imp.shape[0]
    score = jnp.where(jblk < cur, imp, jnp.where(jblk == cur, GQA_R + 1.0, -1.0))
    rank = jnp.zeros(imp.shape, I32)
    for i in range(nb):
        row = score[i:i + 1, :]
        beats = (row > score) | ((row == score) & (jblk > i))
        rank = rank + beats.astype(I32)
    return (rank < N_SEL) & (jblk <= cur)


def _nsa_prompt_kernel(slopes_ref, q_ref, kc_ref, vc_ref, ks_ref, vs_ref, kw_ref, vw_ref, gate_ref,
                       o_ref, bias_ref, acc_ref):
    g = pl.program_id(1)
    i = pl.program_id(2)
    t_len = ks_ref.shape[0]
    nb = t_len // BLK
    q0 = i * QB
    span = WINDOW + QB

    jblk = lax.broadcasted_iota(I32, (nb, QB), 0)
    qpos_l = q0 + lax.broadcasted_iota(I32, (nb, QB), 1)
    dist_c = qpos_l - ((jblk + 1) * BLK - 1)
    valid_c = dist_c >= 0
    dist_cf = dist_c.astype(F32)
    kc = kc_ref[...]
    vc = vc_ref[...]

    qs = [q_ref[:, r * HD_A:(r + 1) * HD_A] for r in range(GQA_R)]
    slopes = [slopes_ref[g * GQA_R + r] for r in range(GQA_R)]

    imp = jnp.zeros((nb, QB), F32)
    o_cmp = []
    for r in range(GQA_R):
        s = _dot_nt(kc, qs[r]) * SCALE_A - slopes[r] * dist_cf
        e, den = _softmax_cols(s, valid_c)
        p = e / _safe(den)
        imp = imp + p
        o_cmp.append(_dot_tn(p, vc))

    sel = _select_mask_t(imp, jblk, qpos_l // BLK)
    expand = (lax.broadcasted_iota(I32, (nb, t_len), 1) // BLK
              == lax.broadcasted_iota(I32, (nb, t_len), 0))
    key_sel = _dot_tn(sel.astype(F32), expand.astype(F32))
    row_q = q0 + lax.broadcasted_iota(I32, (QB, KC), 0)
    col_k = lax.broadcasted_iota(I32, (QB, KC), 1)
    for cc in range(t_len // KC):
        @pl.when(cc * KC < q0 + QB)
        def _(cc=cc):
            ok = (key_sel[:, cc * KC:(cc + 1) * KC] > 0.5) & (row_q >= cc * KC + col_k)
            bias_ref[cc] = jnp.where(ok, 0.0, NEG)

    n_chunks = (q0 + QB + KC - 1) // KC
    kpos0 = lax.broadcasted_iota(I32, (1, KC), 1).astype(F32)
    acc_ref[...] = jnp.zeros_like(acc_ref)

    def body(c, carry):
        ms, ls = carry
        k0 = pl.multiple_of(c * KC, KC)
        kk = ks_ref[pl.ds(k0, KC), :]
        vv = vs_ref[pl.ds(k0, KC), :]
        bias = bias_ref[c]
        kpos = kpos0 + k0.astype(F32)
        ss = [_dot_nt(qs[r], kk) * SCALE_A + (bias + slopes[r] * kpos) for r in range(GQA_R)]
        m_new = [jnp.maximum(ms[r], jnp.max(ss[r], axis=-1, keepdims=True)) for r in range(GQA_R)]
        alpha = [jnp.exp(ms[r] - m_new[r]) for r in range(GQA_R)]
        es = [jnp.exp(ss[r] - m_new[r]) for r in range(GQA_R)]
        l_new = [alpha[r] * ls[r] + jnp.sum(es[r], axis=-1, keepdims=True) for r in range(GQA_R)]
        pv = [_dot(es[r], vv) for r in range(GQA_R)]
        for r in range(GQA_R):
            acc_ref[r] = alpha[r] * acc_ref[r] + pv[r]
        return tuple(m_new), tuple(l_new)

    init = (tuple(jnp.full((QB, 1), NEG, F32) for _ in range(GQA_R)),
            tuple(jnp.zeros((QB, 1), F32) for _ in range(GQA_R)))
    _, l_sel = lax.fori_loop(0, n_chunks, body, init)

    start = pl.multiple_of(jnp.maximum(i - WINDOW // QB, 0) * QB, QB)
    kw = kw_ref[pl.ds(start, span), :]
    vw = vw_ref[pl.ds(start, span), :]
    dist_w = (q0 + lax.broadcasted_iota(I32, (QB, span), 0)) - (start + lax.broadcasted_iota(I32, (QB, span), 1))
    bias_w = jnp.where((dist_w >= 0) & (dist_w < WINDOW), 0.0, NEG)
    kpos_w = (start + lax.broadcasted_iota(I32, (1, span), 1)).astype(F32)
    gates = _sigmoid(gate_ref[...])
    sw = [_dot_nt(qs[r], kw) * SCALE_A + (bias_w + slopes[r] * kpos_w) for r in range(GQA_R)]
    ew = [jnp.exp(sw[r] - jnp.max(sw[r], axis=-1, keepdims=True)) for r in range(GQA_R)]
    den_w = [jnp.sum(ew[r], axis=-1, keepdims=True) for r in range(GQA_R)]
    pv_w = [_dot(ew[r], vw) for r in range(GQA_R)]
    for r in range(GQA_R):
        o = (gates[:, 3 * r:3 * r + 1] * o_cmp[r] + gates[:, 3 * r + 1:3 * r + 2] * (acc_ref[r] / l_sel[r])
             + gates[:, 3 * r + 2:3 * r + 3] * (pv_w[r] / den_w[r]))
        o_ref[:, r * HD_A:(r + 1) * HD_A] = o.astype(o_ref.dtype)


def nsa_prompt(slopes, ppb, pp, cmp_kv, b, t):
    nq = t // QB
    nb = t // BLK
    kvb = A_KV // HD_A

    def kv_spec(branch, kv):
        return pl.BlockSpec((t, HD_A), lambda bi, g, i, s: (bi, kvb + branch * 4 + kv * 2 + g))

    grid_spec = pltpu.PrefetchScalarGridSpec(
        num_scalar_prefetch=1,
        grid=(b, N_KV, nq),
        in_specs=[pl.BlockSpec((QB, GQA_R * HD_A), lambda bi, g, i, s: (bi * nq + i, g)),
                  pl.BlockSpec((None, None, nb, HD_A), lambda bi, g, i, s: (0, g, bi, 0)),
                  pl.BlockSpec((None, None, nb, HD_A), lambda bi, g, i, s: (1, g, bi, 0)),
                  kv_spec(1, 0), kv_spec(1, 1), kv_spec(2, 0), kv_spec(2, 1),
                  pl.BlockSpec((QB, LANE), lambda bi, g, i, s: (bi * nq + i, A_GATE // LANE + g))],
        out_specs=pl.BlockSpec((QB, GQA_R * HD_A), lambda bi, g, i, s: (bi * nq + i, g)),
        scratch_shapes=[pltpu.VMEM((t // KC, QB, KC), F32), pltpu.VMEM((GQA_R, QB, HD_A), F32)],
    )
    return pl.pallas_call(
        _nsa_prompt_kernel,
        out_shape=jax.ShapeDtypeStruct((b * t, D_A), BF16),
        grid_spec=grid_spec,
        compiler_params=_cparams(("parallel", "parallel", "arbitrary")),
        name="nsa_prompt",
    )(slopes, ppb, cmp_kv, cmp_kv, ppb, ppb, ppb, ppb, pp)


PAGE_ROWS = PAGE * KV_ROWS
SLAB_PAGES = 64


def _compress_paged_kernel(pt_ref, pool_hbm, w1_ref, b1_ref, w2_ref, o_ref, slab_ref, acc_ref, sem, *, page0):
    s = pl.program_id(0)
    tc = pl.program_id(1)
    n_slabs = pl.num_programs(0)
    nblk = SLAB_PAGES * (PAGE // BLK)

    def page_copy(slab, p, slot):
        phys = pt_ref[slab * SLAB_PAGES + p]
        return pltpu.make_async_copy(pool_hbm.at[pl.ds((page0 + phys) * PAGE_ROWS, PAGE_ROWS), :],
                                     slab_ref.at[slot, pl.ds(p * PAGE_ROWS, PAGE_ROWS), :], sem.at[slot])

    def start_slab(slab, slot):
        lax.fori_loop(0, SLAB_PAGES, lambda p, c: (page_copy(slab, p, slot).start(), c)[1], 0)

    def wait_slab(slab, slot):
        lax.fori_loop(0, SLAB_PAGES, lambda p, c: (page_copy(slab, p, slot).wait(), c)[1], 0)

    slot = s % 2

    @pl.when(tc == 0)
    def _():
        @pl.when(s == 0)
        def _():
            start_slab(0, 0)

        wait_slab(s, slot)

        @pl.when(s + 1 < n_slabs)
        def _():
            start_slab(s + 1, 1 - slot)

        acc_ref[...] = jnp.zeros_like(acc_ref)

    for tl in range(T_PER_STEP):
        t = tc * T_PER_STEP + tl
        for kv in range(2):
            w1t = w1_ref[kv, tl * HD_A:(tl + 1) * HD_A, :]
            for g in range(N_KV):
                c4 = kv * N_KV + g
                xt = slab_ref[slot, pl.ds(t * KV_ROWS + c4, nblk, stride=BLK * KV_ROWS), :]
                acc_ref[c4] += _dot(xt, w1t)

    @pl.when(tc == pl.num_programs(1) - 1)
    def _():
        for kv in range(2):
            for g in range(N_KV):
                h = _gelu_tanh(acc_ref[kv * N_KV + g] + b1_ref[kv])
                o_ref[kv, g] = _dot(h, w2_ref[kv])


def compress_paged(page_table, pool2d, page0, w1b, b1, w2):
    bd, n_pages = page_table.shape
    assert n_pages % SLAB_PAGES == 0
    n_slabs = bd * n_pages // SLAB_PAGES
    nblk = SLAB_PAGES * (PAGE // BLK)
    grid_spec = pltpu.PrefetchScalarGridSpec(
        num_scalar_prefetch=1,
        grid=(n_slabs, BLK // T_PER_STEP),
        in_specs=[pl.BlockSpec(memory_space=pl.ANY),
                  pl.BlockSpec((2, T_PER_STEP * HD_A, CMP_HID), lambda s, tc, pt: (0, tc, 0)),
                  pl.BlockSpec((2, 1, CMP_HID), lambda s, tc, pt: (0, 0, 0)),
                  pl.BlockSpec((2, CMP_HID, HD_A), lambda s, tc, pt: (0, 0, 0))],
        out_specs=pl.BlockSpec((2, N_KV, nblk, HD_A), lambda s, tc, pt: (0, 0, s, 0)),
        scratch_shapes=[pltpu.VMEM((2, SLAB_PAGES * PAGE_ROWS, HD_A), F32),
                        pltpu.VMEM((2 * N_KV, nblk, CMP_HID), F32),
                        pltpu.SemaphoreType.DMA((2,))],
    )
    return pl.pallas_call(
        functools.partial(_compress_paged_kernel, page0=page0),
        out_shape=jax.ShapeDtypeStruct((2, N_KV, n_slabs * nblk, HD_A), F32),
        grid_spec=grid_spec,
        compiler_params=_cparams(("arbitrary", "arbitrary")),
        name="compress_paged",
    )(page_table.reshape(-1), pool2d, w1b, b1.reshape(2, 1, CMP_HID), w2)


SEL_PAD = 128


def _heads_on_sublanes(q_ref):
    q = q_ref[...].astype(F32)
    row = lax.broadcasted_iota(I32, (SUBLANE, HD_A), 0)
    q8 = jnp.zeros((SUBLANE, HD_A), F32)
    for r in range(GQA_R):
        q8 = jnp.where(row == r, jnp.broadcast_to(q[:, r * HD_A:(r + 1) * HD_A], (SUBLANE, HD_A)), q8)
    return q8


def _slopes_on_sublanes(slopes_ref, g):
    row = lax.broadcasted_iota(I32, (SUBLANE, 1), 0)
    slope = jnp.zeros((SUBLANE, 1), F32)
    for r in range(GQA_R):
        slope = jnp.where(row == r, slopes_ref[g * GQA_R + r], slope)
    return slope


def _softmax_rows(s, valid):
    sm = jnp.where(valid, s, NEG)
    m = jnp.max(sm, axis=-1, keepdims=True)
    e = jnp.where(valid, jnp.exp(sm - m), 0.0)
    return e, jnp.sum(e, axis=-1, keepdims=True)


def _nsa_sample_cmp_kernel(slopes_ref, q_ref, kc_ref, vc_ref, o_ref, idx_ref, *, past, nb_real):
    g = pl.program_id(1)
    nbp = kc_ref.shape[0]
    kc = kc_ref[...]
    vc = vc_ref[...]
    q8 = _heads_on_sublanes(q_ref)
    slope = _slopes_on_sublanes(slopes_ref, g)
    head = lax.broadcasted_iota(I32, (SUBLANE, 1), 0) < GQA_R
    jrow = lax.broadcasted_iota(I32, (1, nbp), 1)
    dist = past - ((jrow + 1) * BLK - 1)
    valid = (dist >= 0) & (jrow < nb_real)
    s = _dot_nt(q8, kc) * SCALE_A - slope * dist.astype(F32)
    e, den = _softmax_rows(s, valid)
    p = e / _safe(den)
    o = _dot(p, vc)
    for r in range(GQA_R):
        o_ref[:, r * HD_A:(r + 1) * HD_A] = o[r:r + 1, :]

    imp = jnp.sum(jnp.where(head, p, 0.0), axis=0, keepdims=True)
    cur = past // BLK
    score_r = jnp.where(jrow < cur, imp, jnp.where(jrow == cur, GQA_R + 1.0, -1.0))
    score_r = jnp.where(jrow < nb_real, score_r, -2.0)
    ii = lax.broadcasted_iota(I32, (nbp, nbp), 0)
    jj = lax.broadcasted_iota(I32, (nbp, nbp), 1)
    sc_b = jnp.broadcast_to(score_r, (nbp, nbp))
    score_c = jnp.sum(jnp.where(ii == jj, sc_b, 0.0), axis=1, keepdims=True)
    beats = (score_c > sc_b) | ((score_c == sc_b) & (ii < jj))
    rank_r = jnp.sum(beats.astype(F32), axis=0, keepdims=True)
    nn = lax.broadcasted_iota(I32, (SEL_PAD, nbp), 0).astype(F32)
    jn = lax.broadcasted_iota(I32, (SEL_PAD, nbp), 1).astype(F32)
    hit = jnp.broadcast_to(rank_r, (SEL_PAD, nbp)) == nn
    idx_c = jnp.sum(jnp.where(hit, jn, 0.0), axis=1, keepdims=True)
    idx_ref[...] = jnp.broadcast_to(idx_c, (SEL_PAD, LANE)).astype(I32)


def nsa_sample_cmp(slopes, ppb3, cmp_kv_s, past, nb_real):
    bd = ppb3.shape[0]
    nbp = cmp_kv_s.shape[3]
    grid_spec = pltpu.PrefetchScalarGridSpec(
        num_scalar_prefetch=1,
        grid=(bd, N_KV),
        in_specs=[pl.BlockSpec((None, 1, GQA_R * HD_A), lambda bi, g, s: (bi, 0, g)),
                  pl.BlockSpec((None, None, None, nbp, HD_A), lambda bi, g, s: (0, g, bi, 0, 0)),
                  pl.BlockSpec((None, None, None, nbp, HD_A), lambda bi, g, s: (1, g, bi, 0, 0))],
        out_specs=[pl.BlockSpec((None, 1, GQA_R * HD_A), lambda bi, g, s: (bi, 0, g)),
                   pl.BlockSpec((None, None, SEL_PAD, LANE), lambda bi, g, s: (bi, g, 0, 0))],
    )
    return pl.pallas_call(
        functools.partial(_nsa_sample_cmp_kernel, past=past, nb_real=nb_real),
        out_shape=[jax.ShapeDtypeStruct((bd, 1, D_A), F32),
                   jax.ShapeDtypeStruct((bd, N_KV, SEL_PAD, LANE), I32)],
        grid_spec=grid_spec,
        compiler_params=_cparams(("parallel", "parallel")),
        name="nsa_sample_cmp",
    )(slopes, ppb3, cmp_kv_s, cmp_kv_s)


W_PAD = LANE


def _nsa_sample_sel_kernel(idx_ref, pt_ref, slopes_ref, q_ref, pool_hbm, kn_ref, vn_ref,
                           wbuf_ref, kwn_ref, vwn_ref, gate_ref, oc_ref, o_ref,
                           blk_ref, ksel_ref, vsel_ref, kwin_ref, vwin_ref, sem, *, past, page0, n_pages):
    bi = pl.program_id(0)
    g = pl.program_id(1)
    nb_past = past // BLK
    bpp = PAGE // BLK
    blk_rows = BLK * KV_ROWS
    base = (bi * N_KV + g) * N_SEL

    def block_copy(n):
        ip = jnp.clip(idx_ref[base + n], 0, nb_past - 1)
        phys = pt_ref[bi * n_pages + ip // bpp]
        row0 = ((page0 + phys) * bpp + ip % bpp) * blk_rows
        return pltpu.make_async_copy(pool_hbm.at[pl.ds(row0, blk_rows), :], blk_ref.at[n], sem)

    for n in range(N_SEL):
        block_copy(n).start()
    for n in range(N_SEL):
        block_copy(n).wait()
    first = lax.broadcasted_iota(I32, (BLK, HD_A), 0) == 0
    k_new = jnp.where(first, kn_ref[...], 0.0)
    v_new = jnp.where(first, vn_ref[...], 0.0)
    for n in range(N_SEL):
        is_past = idx_ref[base + n] < nb_past
        blk = blk_ref.at[n]
        ksel_ref[n * BLK:(n + 1) * BLK, :] = jnp.where(is_past, blk[pl.ds(g, BLK, stride=KV_ROWS), :], k_new)
        vsel_ref[n * BLK:(n + 1) * BLK, :] = jnp.where(is_past, blk[pl.ds(N_KV + g, BLK, stride=KV_ROWS), :], v_new)

    q8 = _heads_on_sublanes(q_ref)
    slope = _slopes_on_sublanes(slopes_ref, g)
    cur = past // BLK
    lane = lax.broadcasted_iota(I32, (1, N_SEL * BLK), 1)
    slot = lane // BLK
    idx_row = jnp.zeros((1, N_SEL * BLK), I32)
    for m in range(N_SEL):
        idx_row = jnp.where(slot == m, idx_ref[base + m], idx_row)
    dist = past - (idx_row * BLK + lane % BLK)
    valid = (idx_row <= cur) & (dist >= 0)
    s = _dot_nt(q8, ksel_ref[...]) * SCALE_A - slope * dist.astype(F32)
    e, den = _softmax_rows(s, valid)
    o_sel = _dot(e / _safe(den), vsel_ref[...])
    wb = wbuf_ref.shape[0] // KV_ROWS
    kwin_ref[0:wb, :] = wbuf_ref[pl.ds(g, wb, stride=KV_ROWS), :]
    vwin_ref[0:wb, :] = wbuf_ref[pl.ds(N_KV + g, wb, stride=KV_ROWS), :]
    first_w = lax.broadcasted_iota(I32, (W_PAD, HD_A), 0) == 0
    kwin_ref[wb:wb + W_PAD, :] = jnp.where(first_w, kwn_ref[...], 0.0)
    vwin_ref[wb:wb + W_PAD, :] = jnp.where(first_w, vwn_ref[...], 0.0)
    lane_w = lax.broadcasted_iota(I32, (1, wb + W_PAD), 1)
    dist_w = wb - lane_w
    valid_w = (dist_w >= 0) & (dist_w < WINDOW)
    s = _dot_nt(q8, kwin_ref[...]) * SCALE_A - slope * dist_w.astype(F32)
    e, den = _softmax_rows(s, valid_w)
    o_win = _dot(e / _safe(den), vwin_ref[...])
    gates = _sigmoid(gate_ref[...])
    for r in range(GQA_R):
        o = (gates[:, 3 * r:3 * r + 1] * oc_ref[:, r * HD_A:(r + 1) * HD_A]
             + gates[:, 3 * r + 1:3 * r + 2] * o_sel[r:r + 1, :]
             + gates[:, 3 * r + 2:3 * r + 3] * o_win[r:r + 1, :])
        o_ref[:, r * HD_A:(r + 1) * HD_A] = o.astype(o_ref.dtype)


def nsa_sample_sel(idx_flat, page_table, slopes, ppb3, pp3, pool2d, page0, win2d, seq0, wb, o_cmp, past):
    bd = ppb3.shape[0]
    n_pages = page_table.shape[1]
    kvb = A_KV // HD_A

    def new_spec(branch, kv):
        return pl.BlockSpec((None, 1, HD_A), lambda bi, g, idx, pt, s: (bi, 0, kvb + branch * 4 + kv * 2 + g))

    grid_spec = pltpu.PrefetchScalarGridSpec(
        num_scalar_prefetch=3,
        grid=(bd, N_KV),
        in_specs=[pl.BlockSpec((None, 1, GQA_R * HD_A), lambda bi, g, idx, pt, s: (bi, 0, g)),
                  pl.BlockSpec(memory_space=pl.ANY), new_spec(1, 0), new_spec(1, 1),
                  pl.BlockSpec((wb * KV_ROWS, HD_A), lambda bi, g, idx, pt, s: (seq0 + bi, 0)),
                  new_spec(2, 0), new_spec(2, 1),
                  pl.BlockSpec((None, 1, LANE), lambda bi, g, idx, pt, s: (bi, 0, A_GATE // LANE + g)),
                  pl.BlockSpec((None, 1, GQA_R * HD_A), lambda bi, g, idx, pt, s: (bi, 0, g))],
        out_specs=pl.BlockSpec((None, 1, GQA_R * HD_A), lambda bi, g, idx, pt, s: (bi, 0, g)),
        scratch_shapes=[pltpu.VMEM((N_SEL, BLK * KV_ROWS, HD_A), F32),
                        pltpu.VMEM((N_SEL * BLK, HD_A), F32), pltpu.VMEM((N_SEL * BLK, HD_A), F32),
                        pltpu.VMEM((wb + W_PAD, HD_A), F32), pltpu.VMEM((wb + W_PAD, HD_A), F32),
                        pltpu.SemaphoreType.DMA(())],
    )
    return pl.pallas_call(
        functools.partial(_nsa_sample_sel_kernel, past=past, page0=page0, n_pages=n_pages),
        out_shape=jax.ShapeDtypeStruct((bd, 1, D_A), BF16),
        grid_spec=grid_spec,
        compiler_params=_cparams(("arbitrary", "arbitrary")),
        name="nsa_sample_sel",
    )(idx_flat, page_table.reshape(-1), slopes, ppb3, pool2d, pp3, pp3, win2d, pp3, pp3, pp3, o_cmp)


def _cumsum_rows(x):
    c = x.shape[0]
    row = lax.broadcasted_iota(I32, x.shape, 0)
    sh = 1
    while sh < c:
        x = x + jnp.where(row >= sh, pltpu.roll(x, sh, 0), 0.0)
        sh *= 2
    return x


def _split2(x):
    hi = x.astype(BF16)
    return hi, (x - hi.astype(F32)).astype(BF16)


def _dot2(a, b, dims=(((1,), (0,)), ((), ()))):
    ah, al = _split2(a)
    bh, bl = _split2(b)
    d = lambda x, y: lax.dot_general(x, y, dims, preferred_element_type=F32)
    return d(ah, bh) + (d(ah, bl) + d(al, bh))


_dot_t = _dot2
NT = (((1,), (1,)), ((), ()))
TN = (((0,), (0,)), ((), ()))
CHUNK_LB = 8
STATE_GROUP = 4


def _rwkv_chunk_kernel(pr_ref, pk_ref, pv_ref, pl_ref, qr_ref, qk_ref, qv_ref, ql_ref,
                       sr_ref, sk_ref, sv_ref, sl_ref,
                       mur_ref, muk_ref, muv_ref, mul_ref, w0_ref, w2_ref, a0_ref, a2_ref, g2_ref,
                       kkp_ref, kap_ref, rkp_ref, gnb_ref,
                       y0_ref, rt_ref, bonus_ref, gate_ref, mm_ref, g0_ref, *, n_valid):
    ci = pl.program_id(2)
    c = pr_ref.shape[0]
    row = lax.broadcasted_iota(I32, (c, 1), 0)
    first_chunk = ci == 0

    def mix(p_ref, q_ref, s_ref, mu_ref):
        p = p_ref[...]
        prev = jnp.where(first_chunk, s_ref[...], q_ref[SUBLANE - 1:SUBLANE, :])
        shifted = jnp.where(row == 0, prev, pltpu.roll(p, 1, 0))
        return p + mu_ref[...] * (shifted - p)

    xr = mix(pr_ref, qr_ref, sr_ref, mur_ref)
    xk = mix(pk_ref, qk_ref, sk_ref, muk_ref)
    xv = mix(pv_ref, qv_ref, sv_ref, muv_ref)
    xl = mix(pl_ref, ql_ref, sl_ref, mul_ref)

    dw = xl[:, 0:LORA_W]
    da = xl[:, LORA_W:LORA_W + LORA_A]
    dg = xl[:, LORA_W + LORA_A:LORA_W + LORA_A + LORA_G]
    wlog = -_softplus(-(w0_ref[...] + _dot(jnp.tanh(dw), w2_ref[...]))) - 0.5
    logdec = -jnp.exp(wlog)
    a = _sigmoid(a0_ref[...] + _dot(da, a2_ref[...]))
    gate_ref[...] = _dot(_sigmoid(dg), g2_ref[...])
    kkv = xk * kkp_ref[...]
    kmod = xk * (1.0 + (a - 1.0) * kap_ref[...])
    if n_valid < c:
        live = row < n_valid
        logdec = jnp.where(live, logdec, 0.0)
        kmod = jnp.where(live, kmod, 0.0)
        a = jnp.where(live, a, 0.0)
        xv = jnp.where(live, xv, 0.0)
    cum = _cumsum_rows(logdec)
    cum_ex = cum - logdec
    cum_end = cum[c - 1:c, :]

    ti = lax.broadcasted_iota(I32, (c, c), 0)
    si = lax.broadcasted_iota(I32, (c, c), 1)
    lower_strict = ti > si
    lower_incl = ti >= si
    eye_c = (ti == si).astype(F32)
    eye_k = (lax.broadcasted_iota(I32, (HD_B, HD_B), 0) == lax.broadcasted_iota(I32, (HD_B, HD_B), 1)).astype(F32)
    rk_all = rkp_ref[...]
    gnb = gnb_ref[...]

    heads = range(pr_ref.shape[1] // HD_B)
    sls = [slice(h * HD_B, (h + 1) * HD_B) for h in heads]
    r_ = [xr[:, s] for s in sls]
    k_ = [kmod[:, s] for s in sls]
    v_ = [xv[:, s] for s in sls]
    kk_ = [kkv[:, s] for s in sls]
    kk_ = [x / jnp.maximum(jnp.sqrt(jnp.sum(x * x, axis=-1, keepdims=True)), 1e-12) for x in kk_]
    b_ = [kk_[h] * a[:, sls[h]] for h in heads]
    kap_ = [kk_[h] * jnp.exp(cum_ex[:, sls[h]]) for h in heads]
    rt_ = [r_[h] * jnp.exp(cum[:, sls[h]]) for h in heads]
    inv_ = [jnp.exp(-cum[:, sls[h]]) for h in heads]
    paired = 2 * c == LANE and len(heads) % 2 == 0
    kb_first = [paired and h % 2 == 0 for h in heads]
    amat = [_dot2(jnp.concatenate([kap_[h], rt_[h]], axis=0),
                  jnp.concatenate([b_[h] * inv_[h], k_[h] * inv_[h]] if kb_first[h]
                                  else [k_[h] * inv_[h], b_[h] * inv_[h]], axis=0), NT) for h in heads]
    kcol = [slice(c, 2 * c) if kb_first[h] else slice(0, c) for h in heads]
    bcol = [slice(0, c) if kb_first[h] else slice(c, 2 * c) for h in heads]
    a_kk = [jnp.where(lower_strict, amat[h][0:c, kcol[h]], 0.0) for h in heads]
    a_rk = [jnp.where(lower_incl, amat[h][c:2 * c, kcol[h]], 0.0) for h in heads]
    a_rb = [jnp.where(lower_incl, amat[h][c:2 * c, bcol[h]], 0.0) for h in heads]
    av = [_dot2(jnp.concatenate([a_kk[h], a_rk[h]], axis=0), v_[h]) for h in heads]

    def level_mask(rows_i, cols_i, w):
        return (rows_i // (2 * w) == cols_i // (2 * w)) & ((rows_i // w) % 2 == 1) & ((cols_i // w) % 2 == 0)

    if paired:
        tp = lax.broadcasted_iota(I32, (c, LANE), 0)
        lane_p = lax.broadcasted_iota(I32, (c, LANE), 1)
        sp = lane_p % c
        left = lane_p < c

        left_b = left.astype(BF16)
        right_b = 1.0 - left_b

        def blockdiag(x):
            return jnp.concatenate([x * left_b, x * right_b], axis=0)

        def dot3p(x_hi, x_lo, y_hi, y_lo):
            d = lambda a_, b_: lax.dot_general(a_, b_, (((1,), (0,)), ((), ())), preferred_element_type=F32)
            return d(x_hi, y_hi) + (d(x_hi, y_lo) + d(x_lo, y_hi))

        pairs = range(len(heads) // 2)
        l_pair = [jnp.where(tp > sp, jnp.where(left, amat[2 * p][0:c, :], amat[2 * p + 1][0:c, :]), 0.0)
                  for p in pairs]
        l_split = [_split2(m) for m in l_pair]
        tpair = [(tp == sp).astype(F32) - jnp.where(((tp % 2) == 1) & (sp == tp - 1), m, 0.0) for m in l_pair]
        w = 2
        while w < c:
            off_b = level_mask(tp, sp, w).astype(BF16)
            t_split = [_split2(m) for m in tpair]
            ld = [dot3p(l_split[p][0] * off_b, l_split[p][1] * off_b,
                        blockdiag(t_split[p][0]), blockdiag(t_split[p][1])) for p in pairs]
            ld_split = [_split2(m) for m in ld]
            tpair = [tpair[p] - dot3p(t_split[p][0], t_split[p][1],
                                      blockdiag(ld_split[p][0]), blockdiag(ld_split[p][1])) for p in pairs]
            w *= 2
        tinv = [tpair[h // 2][:, (h % 2) * c:(h % 2 + 1) * c] for h in heads]
    else:
        a_kb = [jnp.where(lower_strict, amat[h][0:c, bcol[h]], 0.0) for h in heads]
        tinv = [eye_c - jnp.where(((ti % 2) == 1) & (si == ti - 1), m, 0.0) for m in a_kb]
        w = 2
        while w < c:
            off = level_mask(ti, si, w)
            ld = [_dot_t(jnp.where(off, a_kb[h], 0.0), tinv[h]) for h in heads]
            tinv = [tinv[h] - _dot_t(tinv[h], ld[h]) for h in heads]
            w *= 2

    tx = [_dot2(tinv[h], jnp.concatenate([av[h][0:c], kap_[h]], axis=1)) for h in heads]
    arb_tx = [_dot2(a_rb[h], tx[h]) for h in heads]
    dec_end = [jnp.exp(cum_end[:, sls[h]] - cum[:, sls[h]]) for h in heads]
    k_end = [k_[h] * dec_end[h] for h in heads]
    b_end = [b_[h] * dec_end[h] for h in heads]
    g0 = [_dot2(jnp.concatenate([v_[h], -tx[h][:, 0:HD_B]], axis=0),
                jnp.concatenate([k_end[h], b_end[h]], axis=0), TN) for h in heads]
    ktb = [_dot2(tx[h][:, HD_B:2 * HD_B], b_end[h], TN) for h in heads]
    for h in heads:
        s = sls[h]
        y0_ref[:, s] = av[h][c:2 * c] - arb_tx[h][:, 0:HD_B]
        rt_ref[:, s] = rt_[h] - arb_tx[h][:, HD_B:2 * HD_B]
        bonus_ref[:, s] = jnp.sum(r_[h] * k_[h] * rk_all[:, s], axis=-1, keepdims=True) * v_[h] + gnb[:, s]
        mm_ref[h] = eye_k * jnp.exp(cum_end[:, s]) - ktb[h]
        g0_ref[h] = g0[h]


def _rwkv_state_kernel(y0_ref, rt_ref, bonus_ref, gate_ref, mm_ref, g0_ref, s0_ref, gng_ref,
                       o_ref, sT_ref, st_ref):
    @pl.when(pl.program_id(1) == 0)
    def _():
        st_ref[...] = s0_ref[...]

    gng = gng_ref[...]
    for h0 in range(0, N_HEADS_B, STATE_GROUP):
        heads = range(h0, h0 + STATE_GROUP)
        sls = {h: slice(h * HD_B, (h + 1) * HD_B) for h in heads}
        s0 = {h: st_ref[h] for h in heads}
        s_new = {h: _dot2(s0[h], mm_ref[h]) + g0_ref[h] for h in heads}
        y = {h: y0_ref[:, sls[h]] + _dot2(rt_ref[:, sls[h]], s0[h], NT) for h in heads}
        for h in heads:
            st_ref[h] = s_new[h]
            sT_ref[h] = s_new[h]
        mean = {h: jnp.mean(y[h], axis=-1, keepdims=True) for h in heads}
        dev = {h: y[h] - mean[h] for h in heads}
        var = {h: jnp.mean(jnp.square(dev[h]), axis=-1, keepdims=True) for h in heads}
        for h in heads:
            yn = dev[h] * lax.rsqrt(var[h] + GN_EPS) * gng[:, sls[h]]
            o_ref[:, sls[h]] = ((yn + bonus_ref[:, sls[h]]) * gate_ref[:, sls[h]]).astype(o_ref.dtype)


def rwkv7(pp, shift0, wkv0, prm, bsz, t, chunk, n_valid):
    nc = t // chunk
    lb = CHUNK_LB * LANE
    ngrp = D_B // lb
    hs = lb // HD_B
    assert A_RKV % lb == 0
    rb = A_RKV // lb
    per = D_B // lb
    lo_blk = A_LO // (2 * LANE)
    lo3 = 3 * D_B // (2 * LANE)
    sub = chunk // SUBLANE

    def rows(off):
        return pl.BlockSpec((chunk, lb), lambda b, hg, ci: (b * nc + ci, rb + off * per + hg))

    def prev_rows(off):
        return pl.BlockSpec((SUBLANE, lb),
                            lambda b, hg, ci: (jnp.maximum((b * nc + ci) * sub - 1, 0), rb + off * per + hg))

    def srow(off):
        return pl.BlockSpec((None, 1, lb), lambda b, hg, ci: (b, 0, off * per + hg))

    def prow(off):
        return pl.BlockSpec((1, lb), lambda b, hg, ci: (0, off * per + hg))

    def per_head(shape0):
        return pl.BlockSpec((shape0, lb), lambda b, hg, ci: (0, hg))

    lo_spec = lambda rws, imap: pl.BlockSpec((rws, 2 * LANE), imap)
    in_specs = [rows(0), rows(1), rows(2), lo_spec(chunk, lambda b, hg, ci: (b * nc + ci, lo_blk)),
                prev_rows(0), prev_rows(1), prev_rows(2),
                lo_spec(SUBLANE, lambda b, hg, ci: (jnp.maximum((b * nc + ci) * sub - 1, 0), lo_blk)),
                srow(0), srow(1), srow(2), pl.BlockSpec((None, 1, 2 * LANE), lambda b, hg, ci: (b, 0, lo3)),
                prow(0), prow(1), prow(2), lo_spec(1, lambda b, hg, ci: (0, lo3)),
                per_head(1), per_head(LORA_W), per_head(1), per_head(LORA_A), per_head(LORA_G),
                per_head(1), per_head(1), per_head(1), per_head(1)]
    row_out = pl.BlockSpec((chunk, lb), lambda b, hg, ci: (b * nc + ci, hg))
    mat_out = pl.BlockSpec((None, None, hs, HD_B, HD_B), lambda b, hg, ci: (b, ci, hg, 0, 0))
    n = bsz * t
    y0, rt, bonus, gate, mm, g0 = pl.pallas_call(
        functools.partial(_rwkv_chunk_kernel, n_valid=n_valid),
        out_shape=[jax.ShapeDtypeStruct((n, D_B), F32)] * 4
        + [jax.ShapeDtypeStruct((bsz, nc, N_HEADS_B, HD_B, HD_B), F32)] * 2,
        grid=(bsz, ngrp, nc),
        in_specs=in_specs,
        out_specs=[row_out] * 4 + [mat_out] * 2,
        compiler_params=_cparams(("parallel", "parallel", "parallel")),
        name="rwkv_chunk",
    )(pp, pp, pp, pp, pp, pp, pp, pp, shift0, shift0, shift0, shift0,
      prm["mu"], prm["mu"], prm["mu"], prm["mu"], prm["w0"], prm["w2"], prm["a0"], prm["a2"], prm["g2"],
      prm["kk"], prm["ka"], prm["rk"], prm["gn_b"])

    row_in = pl.BlockSpec((chunk, D_B), lambda b, ci: (b * nc + ci, 0))
    mat_in = pl.BlockSpec((None, None, N_HEADS_B, HD_B, HD_B), lambda b, ci: (b, ci, 0, 0, 0))
    state = pl.BlockSpec((None, N_HEADS_B, HD_B, HD_B), lambda b, ci: (b, 0, 0, 0))
    o_b, s_fin = pl.pallas_call(
        _rwkv_state_kernel,
        out_shape=[jax.ShapeDtypeStruct((n, D_B), BF16),
                   jax.ShapeDtypeStruct((bsz, N_HEADS_B, HD_B, HD_B), F32)],
        grid=(bsz, nc),
        in_specs=[row_in, row_in, row_in, row_in, mat_in, mat_in, state,
                  pl.BlockSpec((1, D_B), lambda b, ci: (0, 0))],
        out_specs=[row_in, state],
        scratch_shapes=[pltpu.VMEM((N_HEADS_B, HD_B, HD_B), F32)],
        compiler_params=_cparams(("parallel", "arbitrary")),
        name="rwkv_state",
    )(y0, rt, bonus, gate, mm, g0, wkv0, prm["gn_g"])
    return o_b, s_fin


def _first_lane(cond, lane):
    return jnp.min(jnp.where(cond, lane, 4 * LANE), axis=-1, keepdims=True)


def _router_kernel(h_ref, w_ref, b_ref, eid_ref, wt_ref):
    logits = _dot3(h_ref[...], w_ref[...]) + b_ref[...]
    lane = lax.broadcasted_iota(I32, logits.shape, 1)
    gmask = lane < N_GROUPS
    lg = jnp.where(gmask, logits, NEG)
    eg = jnp.where(gmask, jnp.exp(lg - jnp.max(lg, axis=-1, keepdims=True)), 0.0)
    gp = eg / jnp.sum(eg, axis=-1, keepdims=True)
    g_w = jnp.max(gp, axis=-1, keepdims=True)
    grp = _first_lane(gmask & (gp == g_w), lane)
    lo = N_GROUPS + grp * E_PER_GROUP
    emask = (lane >= lo) & (lane < lo + E_PER_GROUP)
    le = jnp.where(emask, logits, NEG)
    ee = jnp.where(emask, jnp.exp(le - jnp.max(le, axis=-1, keepdims=True)), 0.0)
    ep = ee / jnp.sum(ee, axis=-1, keepdims=True)
    p1 = jnp.max(jnp.where(emask, ep, -1.0), axis=-1, keepdims=True)
    i1 = _first_lane(emask & (ep == p1), lane)
    rest = emask & (lane != i1)
    p2 = jnp.max(jnp.where(rest, ep, -1.0), axis=-1, keepdims=True)
    i2 = _first_lane(rest & (ep == p2), lane)
    tot = p1 + p2
    eid_ref[...] = jnp.where(lane == 0, i1 - N_GROUPS, jnp.where(lane == 1, i2 - N_GROUPS, 0))
    wt_ref[...] = jnp.where(lane == 0, g_w * p1 / tot, jnp.where(lane == 1, g_w * p2 / tot, 0.0))


def router(h2, wr, br):
    n, d = h2.shape
    tm = _row_tile(n, 512)
    return pl.pallas_call(
        _router_kernel,
        out_shape=[jax.ShapeDtypeStruct((n, LANE), I32), jax.ShapeDtypeStruct((n, LANE), F32)],
        grid=(n // tm,),
        in_specs=[pl.BlockSpec((tm, d), lambda i: (i, 0)),
                  pl.BlockSpec((d, LANE), lambda i: (0, 0)),
                  pl.BlockSpec((1, LANE), lambda i: (0, 0))],
        out_specs=[pl.BlockSpec((tm, LANE), lambda i: (i, 0)), pl.BlockSpec((tm, LANE), lambda i: (i, 0))],
        compiler_params=_cparams(("parallel",)),
        name="router",
    )(h2, wr, br)


def _row_copy(src_hbm, dst_vmem, sem, src_row, dst_row):
    return pltpu.make_async_copy(src_hbm.at[pl.ds(src_row * ROW_TILES, ROW_TILES), :],
                                 dst_vmem.at[pl.ds(dst_row * ROW_TILES, ROW_TILES), :], sem)


def _experts_kernel(blk_e_ref, n_used_ref, tok_ref, h_hbm, w1_ref, w3_ref, w2_ref, y_ref,
                    x_even, x_odd, w1b_ref, w3b_ref, w2b_ref, sem):
    i = pl.program_id(0)
    bm = x_even.shape[0] // ROW_TILES
    n_used = n_used_ref[0]
    bufs = (x_even, x_odd)

    def rows(blk, slot, go):
        for r in range(bm):
            go(_row_copy(h_hbm, bufs[slot], sem.at[slot], tok_ref[blk * bm + r], r))

    def block(slot):
        @pl.when(i == 0)
        def _():
            rows(0, 0, lambda cp: cp.start())

        rows(i, slot, lambda cp: cp.wait())

        @pl.when((i == 0) | (blk_e_ref[i] != blk_e_ref[jnp.maximum(i - 1, 0)]))
        def _():
            w1b_ref[...] = w1_ref[...].astype(BF16)
            w3b_ref[...] = w3_ref[...].astype(BF16)
            w2b_ref[...] = w2_ref[...].astype(BF16)

        nxt = jnp.minimum(i + 1, n_used - 1)
        rows(nxt, 1 - slot, lambda cp: cp.start())
        x = jnp.concatenate([_from_row_tiles(bufs[slot], bm, c).astype(BF16) for c in range(ROW_TILES)], axis=1)
        h1 = _dot(x, w1b_ref[...])
        h3 = _dot(x, w3b_ref[...])
        act = h1 * _sigmoid(h1) * h3
        _to_row_tiles(y_ref, _dot(act, w2b_ref[...]))

        @pl.when(i == n_used - 1)
        def _():
            rows(nxt, 1 - slot, lambda cp: cp.wait())

    for slot in range(2):
        pl.when((i < n_used) & (i % 2 == slot))(functools.partial(block, slot))

    @pl.when(i >= n_used)
    def _():
        y_ref[...] = jnp.zeros_like(y_ref)


def experts(h_all, blk_e, n_used, row_tok, w1, w3, w2, l, bm):
    n_blocks = blk_e.shape[0]
    d = D_MODEL
    grid_spec = pltpu.PrefetchScalarGridSpec(
        num_scalar_prefetch=3,
        grid=(n_blocks,),
        in_specs=[pl.BlockSpec(memory_space=pl.ANY),
                  pl.BlockSpec((None, None, d, D_EXPERT), lambda i, be, nu, tk: (l, be[i], 0, 0)),
                  pl.BlockSpec((None, None, d, D_EXPERT), lambda i, be, nu, tk: (l, be[i], 0, 0)),
                  pl.BlockSpec((None, None, D_EXPERT, d), lambda i, be, nu, tk: (l, be[i], 0, 0))],
        out_specs=pl.BlockSpec((bm * ROW_TILES, LANE), lambda i, be, nu, tk: (i, 0)),
        scratch_shapes=[pltpu.VMEM((bm * ROW_TILES, LANE), F32), pltpu.VMEM((bm * ROW_TILES, LANE), F32),
                        pltpu.VMEM((d, D_EXPERT), BF16), pltpu.VMEM((d, D_EXPERT), BF16),
                        pltpu.VMEM((D_EXPERT, d), BF16), pltpu.SemaphoreType.DMA((2,))],
    )
    return pl.pallas_call(
        _experts_kernel,
        out_shape=jax.ShapeDtypeStruct((n_blocks * bm * ROW_TILES, LANE), F32),
        grid_spec=grid_spec,
        compiler_params=_cparams(("arbitrary",)),
        name="experts",
    )(blk_e, n_used, row_tok, h_all, w1, w3, w2)


def _combine_kernel(dest_ref, ys_hbm, wt_ref, x_ref, g_ref, o_ref, buf, sem):
    tm = x_ref.shape[0]
    step = pl.program_id(0) * pl.num_programs(1) + pl.program_id(1)
    n_steps = pl.num_programs(0) * pl.num_programs(1)

    def rows(tile, slot, go):
        for r in range(tm):
            base = (tile * tm + r) * 2
            go(_row_copy(ys_hbm, buf.at[slot, 0], sem.at[slot], dest_ref[base], r))
            go(_row_copy(ys_hbm, buf.at[slot, 1], sem.at[slot], dest_ref[base + 1], r))

    slot = step % 2

    @pl.when(step == 0)
    def _():
        rows(0, 0, lambda cp: cp.start())

    rows(step, slot, lambda cp: cp.wait())
    nxt = jnp.minimum(step + 1, n_steps - 1)
    rows(nxt, 1 - slot, lambda cp: cp.start())
    wt = wt_ref[...]
    w0, w1 = wt[:, 0:1], wt[:, 1:2]
    for c in range(ROW_TILES):
        cs = slice(c * LANE, (c + 1) * LANE)
        moe = w0 * _from_row_tiles(buf.at[slot, 0], tm, c) + w1 * _from_row_tiles(buf.at[slot, 1], tm, c)
        o_ref[:, cs] = x_ref[:, cs] + g_ref[:, cs] * moe

    @pl.when(step == n_steps - 1)
    def _():
        rows(nxt, 1 - slot, lambda cp: cp.wait())


def combine(ys, dest, wts, x, gate):
    b, t, d = x.shape
    tm = _row_tile(t, 128)
    nt = t // tm
    grid_spec = pltpu.PrefetchScalarGridSpec(
        num_scalar_prefetch=1,
        grid=(b, nt),
        in_specs=[pl.BlockSpec(memory_space=pl.ANY),
                  pl.BlockSpec((None, tm, LANE), lambda i, j, ds: (i, j, 0)),
                  pl.BlockSpec((None, tm, d), lambda i, j, ds: (i, j, 0)),
                  pl.BlockSpec((None, 1, d), lambda i, j, ds: (i, 0, 0))],
        out_specs=pl.BlockSpec((None, tm, d), lambda i, j, ds: (i, j, 0)),
        scratch_shapes=[pltpu.VMEM((2, 2, tm * ROW_TILES, LANE), F32), pltpu.SemaphoreType.DMA((2,))],
    )
    return pl.pallas_call(
        _combine_kernel,
        out_shape=jax.ShapeDtypeStruct((b, t, d), F32),
        grid_spec=grid_spec,
        compiler_params=_cparams(("arbitrary", "arbitrary")),
        name="moe_combine",
    )(dest.reshape(-1), ys, wts.reshape(b, t, LANE), x, gate.reshape(b, 1, d))


def _dispatch_tables(eid, bm):
    n = eid.shape[0]
    nk = n * 2
    n_blocks = (nk + N_EXPERTS * (bm - 1) + bm - 1) // bm
    flat_e = eid.reshape(-1)
    onehot = (flat_e[:, None] == jnp.arange(N_EXPERTS, dtype=I32)[None, :]).astype(I32)
    csum = jnp.cumsum(onehot, axis=0)
    counts = csum[-1]
    rank = jnp.sum(onehot * (csum - 1), axis=1)
    padded = (counts + bm - 1) // bm * bm
    pad_end = jnp.cumsum(padded)
    pad_start = pad_end - padded
    dest = (pad_start[flat_e] + rank).astype(I32)
    row_tok = jnp.zeros((n_blocks * bm,), I32).at[dest].set(jnp.arange(nk, dtype=I32) // 2)
    blk_start = jnp.arange(n_blocks, dtype=I32) * bm
    blk_e = jnp.minimum(jnp.sum((pad_end[None, :] <= blk_start[:, None]).astype(I32), axis=1), N_EXPERTS - 1)
    n_used = (pad_end[-1] // bm).astype(I32).reshape(1)
    return blk_e, n_used, row_tok, dest.reshape(n, 2)


def _final_norm_kernel(x_ref, g_ref, o_ref):
    x = x_ref[...]
    o_ref[...] = x * lax.rsqrt(jnp.mean(x * x, axis=-1, keepdims=True) + RMS_EPS) * g_ref[...]


def final_norm(x, g):
    b, t, d = x.shape
    tr = min(t, 256)
    return pl.pallas_call(
        _final_norm_kernel,
        out_shape=jax.ShapeDtypeStruct((b, t, d), F32),
        grid=(b, t // tr),
        in_specs=[pl.BlockSpec((None, tr, d), lambda i, j: (i, j, 0)), pl.BlockSpec((1, d), lambda i, j: (0, 0))],
        out_specs=pl.BlockSpec((None, tr, d), lambda i, j: (i, j, 0)),
        compiler_params=_cparams(("parallel", "parallel")),
        name="final_norm",
    )(x, g.reshape(1, d))


def _align_in_cols(w, axis):
    def take(a, b_):
        return lax.slice_in_dim(w, a, b_, axis=axis)

    def zeros(nz):
        shp = list(w.shape)
        shp[axis] = nz
        return jnp.zeros(shp, w.dtype)

    gate_parts = []
    for g in range(N_KV):
        gate_parts += [take(OFF_GATE_A + g * 3 * GQA_R, OFF_GATE_A + (g + 1) * 3 * GQA_R), zeros(LANE - 3 * GQA_R)]
    parts = [take(0, OFF_GATE_A)] + gate_parts + [take(OFF_RWKV + 3 * D_B, OFF_MERGE), zeros(RW_W - SHIFT_W),
                                                   take(OFF_RWKV, OFF_RWKV + 3 * D_B),
                                                   take(OFF_MERGE, OFF_MERGE + 2 * D_MODEL)]
    return jnp.concatenate(parts, axis=axis)


def _rwkv_cols(p):
    return jnp.concatenate([p[..., A_RKV:A_RKV + 3 * D_B], p[..., A_LO:A_LO + SHIFT_W - 3 * D_B]], axis=-1)


def _pad_lanes(v, width):
    return jnp.pad(v, [(0, 0)] * (v.ndim - 1) + [(0, width - v.shape[-1])])


def kernel(x_prompt, x_sample, cache_cmp, cache_slc, cache_win, state_shift, state_wkv, page_table, c_prompt, c_sample, ln1_g, ln2_g, ada_w, ada_b, w_in, cmp_w1, cmp_b1, cmp_w2, rwkv_mu, rwkv_w0, rwkv_w2, rwkv_a0, rwkv_a2, rwkv_g2, rwkv_kk, rwkv_ka, rwkv_rk, rwkv_gn_g, rwkv_gn_b, w_branch, w_out, router_g_w, router_g_b, router_e_w, router_e_b, exp_w1, exp_w3, exp_w2, final_g):
    depth = w_in.shape[0]
    bp, t, d = x_prompt.shape
    bd, ts, _ = x_sample.shape
    assert ts == 1 and t % KC == 0 and t >= WINDOW + QB
    n_pages = page_table.shape[1]
    past = n_pages * PAGE
    wb = cache_win.shape[2]
    n_phys = cache_cmp.shape[1]
    n_p = bp * t

    slopes = jnp.exp2(-8.0 * (jnp.arange(N_HEADS_A, dtype=F32) + 1.0) / N_HEADS_A)
    c_rows = bp + bd
    c16 = jnp.zeros(((c_rows + SUBLANE - 1) // SUBLANE * SUBLANE, d), F32).at[:bp].set(c_prompt).at[bp:c_rows].set(c_sample)
    mod = adaln(c16, ada_w, ada_b).reshape(depth, c16.shape[0], 6, d)

    xp, xs = x_prompt, x_sample
    outs = {k: [] for k in ("cmp_p", "slc_p", "win_p", "shf_p", "wkv_p", "cmp_s", "slc_s", "win_s", "shf_s", "wkv_s")}
    pool_cmp2d = cache_cmp.reshape(-1, HD_A)
    pool_slc2d = cache_slc.reshape(-1, HD_A)
    win2d = cache_win.reshape(-1, HD_A)

    for l in range(depth):
        mp, ms = mod[l, :bp], mod[l, bp:c_rows]
        w_in_al = _align_in_cols(w_in[l], 1).astype(BF16)
        w1b = cmp_w1[l].astype(BF16)
        w_out_b = w_out[l].astype(BF16)
        prm = {
            "mu": _pad_lanes(rwkv_mu[l][None, :], RW_W),
            "w0": rwkv_w0[l][None, :], "w2": rwkv_w2[l], "a0": rwkv_a0[l][None, :], "a2": rwkv_a2[l],
            "g2": rwkv_g2[l], "kk": rwkv_kk[l][None, :], "ka": rwkv_ka[l][None, :],
            "rk": rwkv_rk[l].reshape(1, D_B), "gn_g": rwkv_gn_g[l][None, :], "gn_b": rwkv_gn_b[l][None, :],
        }
        wr = _pad_lanes(jnp.concatenate([router_g_w[l], router_e_w[l]], axis=1), LANE)
        br = _pad_lanes(jnp.concatenate([router_g_b[l], router_e_b[l]])[None, :], LANE)

        hp_ = norm_mod(xp, ln1_g[l], mp[:, 0], mp[:, 1], BF16).reshape(n_p, d)
        pp, ppb, kv_p = in_proj(hp_, w_in_al)
        cmp_kv = compress(pp, A_KV, 1, w1b, cmp_b1[l], cmp_w2[l])
        o_a = nsa_prompt(slopes, ppb, pp, cmp_kv, bp, t)
        o_b, s_fin = rwkv7(pp, jnp.zeros((bp, 1, RW_W), F32), jnp.zeros((bp, N_HEADS_B, HD_B, HD_B), F32),
                           prm, bp, t, 64, 64)
        merged = branch_merge(o_a, o_b, w_branch, l, pp)
        xp = out_proj_residual(merged.reshape(bp, t, d), w_out_b, xp, mp[:, 2])
        kv_p = kv_p.reshape(3, bp, t, 2, N_KV, HD_A)
        outs["cmp_p"].append(kv_p[0])
        outs["slc_p"].append(kv_p[1])
        outs["win_p"].append(kv_p[2, :, t - min(WINDOW, t):])
        outs["shf_p"].append(_rwkv_cols(pp.reshape(bp, t, IN_AL)[:, -1]))
        outs["wkv_p"].append(s_fin)

        hs_ = norm_mod(xs, ln1_g[l], ms[:, 0], ms[:, 1], BF16).reshape(bd, d)
        ps, psb, kv_s = in_proj(hs_, w_in_al)
        cmp_past = compress_paged(page_table, pool_cmp2d, l * n_phys, w1b, cmp_b1[l], cmp_w2[l])
        new_rows = jnp.zeros((bd, BLK, 2 * KV_W), F32).at[:, 0].set(ps[:, A_KV:A_KV + 2 * KV_W])
        cmp_new = compress(new_rows.reshape(bd * BLK, 2 * KV_W), 0, 1, w1b, cmp_b1[l], cmp_w2[l])
        nb_past = past // BLK
        nb_real = nb_past + 1
        nbp = (nb_real + LANE - 1) // LANE * LANE
        cmp_s = jnp.concatenate([cmp_past.reshape(2, N_KV, bd, nb_past, HD_A), cmp_new[:, :, :, None, :],
                                 jnp.zeros((2, N_KV, bd, nbp - nb_real, HD_A), F32)], axis=3)
        ps3 = ps.reshape(bd, 1, IN_AL)
        psb3 = psb.reshape(bd, 1, A_GATE)
        o_cmp, idx_full = nsa_sample_cmp(slopes, psb3, cmp_s, past, nb_real)
        idx_flat = idx_full[:, :, :N_SEL, 0].reshape(-1)
        o_a_s = nsa_sample_sel(idx_flat, page_table, slopes, psb3, ps3, pool_slc2d, l * n_phys,
                               win2d, l * bd, wb, o_cmp, past).reshape(bd, D_A)
        ps_pad = jnp.zeros((bd, SUBLANE, IN_AL), F32).at[:, 0].set(ps).reshape(bd * SUBLANE, IN_AL)
        o_b_s, s_fin_s = rwkv7(ps_pad, _pad_lanes(state_shift[l], RW_W)[:, None, :], state_wkv[l],
                               prm, bd, SUBLANE, SUBLANE, 1)
        o_b_s = o_b_s.reshape(bd, SUBLANE, D_B)[:, 0]
        merged_s = branch_merge(o_a_s, o_b_s, w_branch, l, ps)
        xs = out_proj_residual(merged_s.reshape(bd, 1, d), w_out_b, xs, ms[:, 2])
        kv_s = kv_s.reshape(3, bd, 1, 2, N_KV, HD_A)
        outs["cmp_s"].append(kv_s[0])
        outs["slc_s"].append(kv_s[1])
        outs["win_s"].append(jnp.concatenate([cache_win[l, :, 1:], kv_s[2]], axis=1))
        outs["shf_s"].append(_rwkv_cols(ps))
        outs["wkv_s"].append(s_fin_s)

        h2p, h_all = norm_mod(xp, ln2_g[l], mp[:, 3], mp[:, 4], F32, tiles_rows=n_p + bd,
                              tiles_into=jnp.zeros(((n_p + bd) * ROW_TILES, LANE), F32))
        h2s, h_all = norm_mod(xs, ln2_g[l], ms[:, 3], ms[:, 4], F32, tiles_rows=n_p + bd, tiles_into=h_all,
                              tiles_row0=n_p)
        eid_p, wt_p = router(h2p.reshape(n_p, d), wr, br)
        eid_s, wt_s = router(h2s.reshape(bd, d), wr, br)
        eid = jnp.concatenate([eid_p[:, :2], eid_s[:, :2]], axis=0)
        blk_e, n_used, row_tok, dest = _dispatch_tables(eid, MOE_BLOCK)
        ys = experts(h_all, blk_e, n_used, row_tok, exp_w1, exp_w3, exp_w2, l, MOE_BLOCK)
        xp = combine(ys, dest[:n_p], wt_p, xp, mp[:, 5])
        xs = combine(ys, dest[n_p:], wt_s, xs, ms[:, 5])

    y_prompt = final_norm(xp, final_g)
    y_sample = final_norm(xs, final_g)
    st = lambda k: jnp.stack(outs[k])
    return (y_prompt, y_sample, st("cmp_p"), st("slc_p"), st("win_p"), st("shf_p"), st("wkv_p"),
            st("cmp_s"), st("slc_s"), st("win_s"), st("shf_s"), st("wkv_s"))
```

```python
import functools

import jax
import jax.numpy as jnp
from jax import lax
from jax.experimental import pallas as pl
from jax.experimental.pallas import tpu as pltpu

F32 = jnp.float32
BF16 = jnp.bfloat16
I32 = jnp.int32

LANE = 128
SUBLANE = 8
VMEM_LIMIT = 56 * 1024 * 1024

D_MODEL = 2048
HD_A = 128
N_HEADS_A = 8
N_KV = 2
GQA_R = 4
BLK = 64
N_SEL = 16
WINDOW = 512
CMP_HID = 256
SCALE_A = HD_A ** -0.5
D_A = 1024
D_B = 1024
HD_B = 64
N_HEADS_B = 16
LORA_W, LORA_A, LORA_G = 64, 64, 32
GN_EPS = HD_B * 1e-5
N_GROUPS = 4
E_PER_GROUP = 8
N_EXPERTS = 32
D_EXPERT = 512
MOE_BLOCK = 256
RMS_EPS = 1e-6
PAGE = 128

Q_W = N_HEADS_A * HD_A
KV_W = N_KV * HD_A
OFF_KV = Q_W
OFF_GATE_A = OFF_KV + 6 * KV_W
OFF_RWKV = OFF_GATE_A + 3 * N_HEADS_A
SHIFT_W = 3 * D_B + LORA_W + LORA_A + LORA_G
OFF_MERGE = OFF_RWKV + SHIFT_W

A_KV = Q_W
A_GATE = A_KV + 6 * KV_W
A_LO = A_GATE + N_KV * LANE
A_RKV = A_LO + 2 * LANE
RW_W = 3 * D_B + 2 * LANE
A_MERGE = A_RKV + 3 * D_B
IN_AL = A_MERGE + 2 * D_MODEL
NEG = -1e30


def _cparams(sem):
    return pltpu.CompilerParams(dimension_semantics=sem, vmem_limit_bytes=VMEM_LIMIT)


def _dot(a, b, dims=(((1,), (0,)), ((), ()))):
    return lax.dot_general(a.astype(BF16), b.astype(BF16), dims, preferred_element_type=F32)


def _dot_nt(a, b):
    return _dot(a, b, (((1,), (1,)), ((), ())))


def _dot_tn(a, b):
    return _dot(a, b, (((0,), (0,)), ((), ())))


def _split3(x):
    h = x.astype(BF16)
    r1 = x - h.astype(F32)
    m = r1.astype(BF16)
    lo = (r1 - m.astype(F32)).astype(BF16)
    return h, m, lo


def _dot3(a, b, dims=(((1,), (0,)), ((), ()))):
    ah, am, al = _split3(a)
    bh, bm, bl = _split3(b)
    d = lambda x, y: lax.dot_general(x, y, dims, preferred_element_type=F32)
    return (d(ah, bh) + (d(ah, bm) + d(am, bh))) + ((d(am, bm) + d(ah, bl)) + d(al, bh))


def _sigmoid(x):
    return 1.0 / (1.0 + jnp.exp(-x))


def _softplus(x):
    return jnp.maximum(x, 0.0) + jnp.log(1.0 + jnp.exp(-jnp.abs(x)))


def _adaln_kernel(c_ref, w_ref, b_ref, o_ref):
    c = c_ref[...]
    h = c * _sigmoid(c)
    o_ref[...] = _dot(h, w_ref[...]) + b_ref[...]


def adaln(c16, ada_w, ada_b):
    depth, d, n = ada_w.shape
    tn = 1024
    return pl.pallas_call(
        _adaln_kernel,
        out_shape=jax.ShapeDtypeStruct((depth, c16.shape[0], n), F32),
        grid=(depth, n // tn),
        in_specs=[pl.BlockSpec(c16.shape, lambda l, j: (0, 0)),
                  pl.BlockSpec((None, d, tn), lambda l, j: (l, 0, j)),
                  pl.BlockSpec((None, 1, tn), lambda l, j: (l, 0, j))],
        out_specs=pl.BlockSpec((None, c16.shape[0], tn), lambda l, j: (l, 0, j)),
        compiler_params=_cparams(("parallel", "parallel")),
        name="adaln",
    )(c16, ada_w, ada_b.reshape(depth, 1, n))


ROW_TILES = D_MODEL // LANE


def _to_row_tiles(ref, val):
    rows = val.shape[0]
    for c in range(ROW_TILES):
        ref[pl.ds(c, rows, stride=ROW_TILES), :] = val[:, c * LANE:(c + 1) * LANE]


def _from_row_tiles(ref, rows, c):
    return ref[pl.ds(c, rows, stride=ROW_TILES), :]


def _norm_kernel(x_ref, g_ref, sh_ref, sc_ref, o_ref):
    x = x_ref[...]
    y = x * lax.rsqrt(jnp.mean(x * x, axis=-1, keepdims=True) + RMS_EPS)
    y = y * g_ref[...]
    o_ref[...] = (y * (1.0 + sc_ref[...]) + sh_ref[...]).astype(o_ref.dtype)


def _norm_tiles_kernel(x_ref, g_ref, sh_ref, sc_ref, *refs):
    o_ref, tiles_ref = refs[-2:]
    _norm_kernel(x_ref, g_ref, sh_ref, sc_ref, o_ref)
    _to_row_tiles(tiles_ref, o_ref[...])


def norm_mod(x, g, shift, scale, out_dtype, tiles_rows=0, tiles_into=None, tiles_row0=0):
    b, t, d = x.shape
    tr = min(t, 256)
    nt = t // tr
    in_specs = [pl.BlockSpec((None, tr, d), lambda i, j: (i, j, 0)),
                pl.BlockSpec((1, d), lambda i, j: (0, 0)),
                pl.BlockSpec((None, 1, d), lambda i, j: (i, 0, 0)),
                pl.BlockSpec((None, 1, d), lambda i, j: (i, 0, 0))]
    args = [x, g.reshape(1, d), shift.reshape(b, 1, d), scale.reshape(b, 1, d)]
    row_spec = pl.BlockSpec((None, tr, d), lambda i, j: (i, j, 0))
    if not tiles_rows:
        return pl.pallas_call(
            _norm_kernel, out_shape=jax.ShapeDtypeStruct((b, t, d), out_dtype), grid=(b, nt),
            in_specs=in_specs, out_specs=row_spec,
            compiler_params=_cparams(("parallel", "parallel")), name="norm_mod")(*args)
    assert out_dtype == F32 and tiles_row0 % tr == 0
    blk0 = tiles_row0 // tr
    aliases = {}
    if tiles_into is not None:
        in_specs.append(pl.BlockSpec(memory_space=pl.ANY))
        args.append(tiles_into)
        aliases = {len(args) - 1: 1}
    return pl.pallas_call(
        _norm_tiles_kernel,
        out_shape=[jax.ShapeDtypeStruct((b, t, d), F32),
                   jax.ShapeDtypeStruct((tiles_rows * ROW_TILES, LANE), F32)],
        grid=(b, nt),
        in_specs=in_specs,
        out_specs=[row_spec, pl.BlockSpec((tr * ROW_TILES, LANE), lambda i, j: (blk0 + i * nt + j, 0))],
        input_output_aliases=aliases,
        compiler_params=_cparams(("parallel", "parallel")),
        name="norm_tiles",
    )(*args)


IN_TN = 2 * KV_W
N_BF16_TILES = A_GATE // IN_TN
KV_TILE0 = A_KV // IN_TN
KV_ROWS = 2 * N_KV


def _in_proj_kernel(a_ref, w_ref, o_ref, ob_ref, kv_ref):
    j = pl.program_id(1)
    acc = _dot(a_ref[...], w_ref[...])
    o_ref[...] = acc

    @pl.when(j < N_BF16_TILES)
    def _():
        ob_ref[...] = acc.astype(BF16)

    @pl.when((j >= KV_TILE0) & (j < N_BF16_TILES))
    def _():
        tm = acc.shape[0]
        for c4 in range(KV_ROWS):
            kv_ref[pl.ds(c4, tm, stride=KV_ROWS), :] = acc[:, c4 * HD_A:(c4 + 1) * HD_A]


def _row_tile(m, cap):
    return m if m <= cap else cap


def in_proj(a, w):
    m, k = a.shape
    tm = _row_tile(m, 2048)
    tn = IN_TN
    assert m % tm == 0
    return pl.pallas_call(
        _in_proj_kernel,
        out_shape=[jax.ShapeDtypeStruct((m, IN_AL), F32), jax.ShapeDtypeStruct((m, A_GATE), BF16),
                   jax.ShapeDtypeStruct((3, m * KV_ROWS, HD_A), F32)],
        grid=(m // tm, IN_AL // tn),
        in_specs=[pl.BlockSpec((tm, k), lambda i, j: (i, 0)),
                  pl.BlockSpec((k, tn), lambda i, j: (0, j))],
        out_specs=[pl.BlockSpec((tm, tn), lambda i, j: (i, j)),
                   pl.BlockSpec((tm, tn), lambda i, j: (i, jnp.minimum(j, N_BF16_TILES - 1))),
                   pl.BlockSpec((None, tm * KV_ROWS, HD_A),
                                lambda i, j: (jnp.clip(j - KV_TILE0, 0, 2), i, 0))],
        compiler_params=_cparams(("parallel", "arbitrary")),
        name="proj_in",
    )(a, w)


def _branch_kernel(oa_ref, ob_ref, wa_ref, wb_ref, ga_ref, gb_ref, o_ref):
    ya = _dot(oa_ref[...], wa_ref[...])
    yb = _dot(ob_ref[...], wb_ref[...])
    o_ref[...] = (_sigmoid(ga_ref[...]) * ya + _sigmoid(gb_ref[...]) * yb).astype(o_ref.dtype)


def branch_merge(o_a, o_b, w_branch, l, pp):
    m = o_a.shape[0]
    tm = _row_tile(m, 1024)
    tn = 512
    gcol = A_MERGE // tn
    return pl.pallas_call(
        _branch_kernel,
        out_shape=jax.ShapeDtypeStruct((m, D_MODEL), BF16),
        grid=(m // tm, D_MODEL // tn),
        in_specs=[pl.BlockSpec((tm, D_A), lambda i, j: (i, 0)),
                  pl.BlockSpec((tm, D_B), lambda i, j: (i, 0)),
                  pl.BlockSpec((None, D_A, tn), lambda i, j: (l, 0, j)),
                  pl.BlockSpec((None, D_B, tn), lambda i, j: (l, D_A // D_B, j)),
                  pl.BlockSpec((tm, tn), lambda i, j: (i, gcol + j)),
                  pl.BlockSpec((tm, tn), lambda i, j: (i, gcol + D_MODEL // tn + j))],
        out_specs=pl.BlockSpec((tm, tn), lambda i, j: (i, j)),
        compiler_params=_cparams(("parallel", "parallel")),
        name="branch_merge",
    )(o_a, o_b, w_branch, w_branch, pp, pp)


def _resid_kernel(m_ref, w_ref, x_ref, g_ref, o_ref):
    o_ref[...] = x_ref[...] + g_ref[...] * _dot(m_ref[...], w_ref[...])


def out_proj_residual(merged, w_out_l, x, gate):
    b, t, d = x.shape
    tm = _row_tile(t, 2048)
    tn = 512
    nt = t // tm
    return pl.pallas_call(
        _resid_kernel,
        out_shape=jax.ShapeDtypeStruct((b, t, d), F32),
        grid=(b, nt, d // tn),
        in_specs=[pl.BlockSpec((None, tm, d), lambda i, r, j: (i, r, 0)),
                  pl.BlockSpec((d, tn), lambda i, r, j: (0, j)),
                  pl.BlockSpec((None, tm, tn), lambda i, r, j: (i, r, j)),
                  pl.BlockSpec((None, 1, tn), lambda i, r, j: (i, 0, j))],
        out_specs=pl.BlockSpec((None, tm, tn), lambda i, r, j: (i, r, j)),
        compiler_params=_cparams(("parallel", "parallel", "parallel")),
        name="out_proj",
    )(merged, w_out_l, x, gate.reshape(b, 1, d))


T_PER_STEP = 8


def _gelu_tanh(x):
    return 0.5 * x * (1.0 + jnp.tanh(0.7978845608028654 * (x + 0.044715 * x * x * x)))


def _compress_kernel(x_ref, w1_ref, b1_ref, w2_ref, o_ref, acc_ref):
    tc = pl.program_id(3)
    nblk = o_ref.shape[0]

    @pl.when(tc == 0)
    def _():
        acc_ref[...] = jnp.zeros_like(acc_ref)

    acc = acc_ref[...]
    for tl in range(T_PER_STEP):
        t = tc * T_PER_STEP + tl
        xt = x_ref[pl.ds(t, nblk, stride=BLK), :]
        acc = acc + _dot(xt, w1_ref[tl * HD_A:(tl + 1) * HD_A, :])
    acc_ref[...] = acc

    @pl.when(tc == pl.num_programs(3) - 1)
    def _():
        h = _gelu_tanh(acc + b1_ref[...])
        o_ref[...] = _dot(h, w2_ref[...])


def compress(rows, col0, n_slabs, w1b, b1, w2):
    r = rows.shape[0] // n_slabs
    nblk = r // BLK
    cb = col0 // HD_A
    return pl.pallas_call(
        _compress_kernel,
        out_shape=jax.ShapeDtypeStruct((2, N_KV, n_slabs * nblk, HD_A), F32),
        grid=(2, N_KV, n_slabs, BLK // T_PER_STEP),
        in_specs=[pl.BlockSpec((r, HD_A), lambda kv, g, s, tc: (s, cb + kv * N_KV + g)),
                  pl.BlockSpec((None, T_PER_STEP * HD_A, CMP_HID), lambda kv, g, s, tc: (kv, tc, 0)),
                  pl.BlockSpec((None, 1, CMP_HID), lambda kv, g, s, tc: (kv, 0, 0)),
                  pl.BlockSpec((None, CMP_HID, HD_A), lambda kv, g, s, tc: (kv, 0, 0))],
        out_specs=pl.BlockSpec((None, None, nblk, HD_A), lambda kv, g, s, tc: (kv, g, s, 0)),
        scratch_shapes=[pltpu.VMEM((nblk, CMP_HID), F32)],
        compiler_params=_cparams(("parallel", "parallel", "parallel", "arbitrary")),
        name="compress",
    )(rows, w1b, b1.reshape(2, 1, CMP_HID), w2)


QB = 128
KC = 512
HEAD_GROUP = 4


def _softmax_cols(s, valid):
    sm = jnp.where(valid, s, NEG)
    m = jnp.max(sm, axis=0, keepdims=True)
    e = jnp.where(valid, jnp.exp(sm - m), 0.0)
    return e, jnp.sum(e, axis=0, keepdims=True)


def _safe(den):
    return jnp.where(den > 0, den, 1.0)


def _select_mask_t(imp, jblk, cur):
    nb = imp.shape[0]
    score = jnp.where(jblk < cur, imp, jnp.where(jblk == cur, GQA_R + 1.0, -1.0))
    rank = jnp.zeros(imp.shape, I32)
    for i in range(nb):
        row = score[i:i + 1, :]
        beats = (row > score) | ((row == score) & (jblk > i))
        rank = rank + beats.astype(I32)
    return (rank < N_SEL) & (jblk <= cur)


def _nsa_prompt_kernel(slopes_ref, q_ref, kc_ref, vc_ref, ks_ref, vs_ref, kw_ref, vw_ref, gate_ref,
                       o_ref, bias_ref, acc_ref):
    g = pl.program_id(1)
    i = pl.program_id(2)
    t_len = ks_ref.shape[0]
    nb = t_len // BLK
    q0 = i * QB
    span = WINDOW + QB

    jblk = lax.broadcasted_iota(I32, (nb, QB), 0)
    qpos_l = q0 + lax.broadcasted_iota(I32, (nb, QB), 1)
    dist_c = qpos_l - ((jblk + 1) * BLK - 1)
    valid_c = dist_c >= 0
    dist_cf = dist_c.astype(F32)
    kc = kc_ref[...]
    vc = vc_ref[...]

    qs = [q_ref[:, r * HD_A:(r + 1) * HD_A] for r in range(GQA_R)]
    slopes = [slopes_ref[g * GQA_R + r] for r in range(GQA_R)]

    imp = jnp.zeros((nb, QB), F32)
    o_cmp = []
    for r in range(GQA_R):
        s = _dot_nt(kc, qs[r]) * SCALE_A - slopes[r] * dist_cf
        e, den = _softmax_cols(s, valid_c)
        p = e / _safe(den)
        imp = imp + p
        o_cmp.append(_dot_tn(p, vc))

    sel = _select_mask_t(imp, jblk, qpos_l // BLK)
    expand = (lax.broadcasted_iota(I32, (nb, t_len), 1) // BLK
              == lax.broadcasted_iota(I32, (nb, t_len), 0))
    key_sel = _dot_tn(sel.astype(F32), expand.astype(F32))
    row_q = q0 + lax.broadcasted_iota(I32, (QB, KC), 0)
    col_k = lax.broadcasted_iota(I32, (QB, KC), 1)
    for cc in range(t_len // KC):
        @pl.when(cc * KC < q0 + QB)
        def _(cc=cc):
            ok = (key_sel[:, cc * KC:(cc + 1) * KC] > 0.5) & (row_q >= cc * KC + col_k)
            bias_ref[cc] = jnp.where(ok, 0.0, NEG)

    n_chunks = (q0 + QB + KC - 1) // KC
    kpos0 = lax.broadcasted_iota(I32, (1, KC), 1).astype(F32)
    acc_ref[...] = jnp.zeros_like(acc_ref)

    def body(c, carry):
        ms, ls = carry
        k0 = pl.multiple_of(c * KC, KC)
        kk = ks_ref[pl.ds(k0, KC), :]
        vv = vs_ref[pl.ds(k0, KC), :]
        bias = bias_ref[c]
        kpos = kpos0 + k0.astype(F32)
        m_new, l_new = [None] * GQA_R, [None] * GQA_R
        for r0 in range(0, GQA_R, HEAD_GROUP):
            grp = range(r0, r0 + HEAD_GROUP)
            ss = {r: _dot_nt(qs[r], kk) * SCALE_A + (bias + slopes[r] * kpos) for r in grp}
            for r in grp:
                m_new[r] = jnp.maximum(ms[r], jnp.max(ss[r], axis=-1, keepdims=True))
            alpha = {r: jnp.exp(ms[r] - m_new[r]) for r in grp}
            es = {r: jnp.exp(ss[r] - m_new[r]) for r in grp}
            for r in grp:
                l_new[r] = alpha[r] * ls[r] + jnp.sum(es[r], axis=-1, keepdims=True)
            pv = {r: _dot(es[r], vv) for r in grp}
            for r in grp:
                acc_ref[r] = alpha[r] * acc_ref[r] + pv[r]
        return tuple(m_new), tuple(l_new)

    init = (tuple(jnp.full((QB, 1), NEG, F32) for _ in range(GQA_R)),
            tuple(jnp.zeros((QB, 1), F32) for _ in range(GQA_R)))
    _, l_sel = lax.fori_loop(0, n_chunks, body, init)

    start = pl.multiple_of(jnp.maximum(i - WINDOW // QB, 0) * QB, QB)
    kw = kw_ref[pl.ds(start, span), :]
    vw = vw_ref[pl.ds(start, span), :]
    dist_w = (q0 + lax.broadcasted_iota(I32, (QB, span), 0)) - (start + lax.broadcasted_iota(I32, (QB, span), 1))
    bias_w = jnp.where((dist_w >= 0) & (dist_w < WINDOW), 0.0, NEG)
    kpos_w = (start + lax.broadcasted_iota(I32, (1, span), 1)).astype(F32)
    gates = _sigmoid(gate_ref[...])
    sw = [_dot_nt(qs[r], kw) * SCALE_A + (bias_w + slopes[r] * kpos_w) for r in range(GQA_R)]
    ew = [jnp.exp(sw[r] - jnp.max(sw[r], axis=-1, keepdims=True)) for r in range(GQA_R)]
    den_w = [jnp.sum(ew[r], axis=-1, keepdims=True) for r in range(GQA_R)]
    pv_w = [_dot(ew[r], vw) for r in range(GQA_R)]
    for r in range(GQA_R):
        o = (gates[:, 3 * r:3 * r + 1] * o_cmp[r] + gates[:, 3 * r + 1:3 * r + 2] * (acc_ref[r] / l_sel[r])
             + gates[:, 3 * r + 2:3 * r + 3] * (pv_w[r] / den_w[r]))
        o_ref[:, r * HD_A:(r + 1) * HD_A] = o.astype(o_ref.dtype)


def nsa_prompt(slopes, ppb, pp, cmp_kv, b, t):
    nq = t // QB
    nb = t // BLK
    kvb = A_KV // HD_A

    def kv_spec(branch, kv):
        return pl.BlockSpec((t, HD_A), lambda bi, g, i, s: (bi, kvb + branch * 4 + kv * 2 + g))

    grid_spec = pltpu.PrefetchScalarGridSpec(
        num_scalar_prefetch=1,
        grid=(b, N_KV, nq),
        in_specs=[pl.BlockSpec((QB, GQA_R * HD_A), lambda bi, g, i, s: (bi * nq + i, g)),
                  pl.BlockSpec((None, None, nb, HD_A), lambda bi, g, i, s: (0, g, bi, 0)),
                  pl.BlockSpec((None, None, nb, HD_A), lambda bi, g, i, s: (1, g, bi, 0)),
                  kv_spec(1, 0), kv_spec(1, 1), kv_spec(2, 0), kv_spec(2, 1),
                  pl.BlockSpec((QB, LANE), lambda bi, g, i, s: (bi * nq + i, A_GATE // LANE + g))],
        out_specs=pl.BlockSpec((QB, GQA_R * HD_A), lambda bi, g, i, s: (bi * nq + i, g)),
        scratch_shapes=[pltpu.VMEM((t // KC, QB, KC), F32), pltpu.VMEM((GQA_R, QB, HD_A), F32)],
    )
    return pl.pallas_call(
        _nsa_prompt_kernel,
        out_shape=jax.ShapeDtypeStruct((b * t, D_A), BF16),
        grid_spec=grid_spec,
        compiler_params=_cparams(("parallel", "parallel", "arbitrary")),
        name="nsa_prompt",
    )(slopes, ppb, cmp_kv, cmp_kv, ppb, ppb, ppb, ppb, pp)


PAGE_ROWS = PAGE * KV_ROWS
SLAB_PAGES = 64


def _compress_paged_kernel(pt_ref, pool_hbm, w1_ref, b1_ref, w2_ref, o_ref, slab_ref, acc_ref, sem, *, page0):
    s = pl.program_id(0)
    tc = pl.program_id(1)
    n_slabs = pl.num_programs(0)
    nblk = SLAB_PAGES * (PAGE // BLK)

    def page_copy(slab, p, slot):
        phys = pt_ref[slab * SLAB_PAGES + p]
        return pltpu.make_async_copy(pool_hbm.at[pl.ds((page0 + phys) * PAGE_ROWS, PAGE_ROWS), :],
                                     slab_ref.at[slot, pl.ds(p * PAGE_ROWS, PAGE_ROWS), :], sem.at[slot])

    def start_slab(slab, slot):
        lax.fori_loop(0, SLAB_PAGES, lambda p, c: (page_copy(slab, p, slot).start(), c)[1], 0)

    def wait_slab(slab, slot):
        lax.fori_loop(0, SLAB_PAGES, lambda p, c: (page_copy(slab, p, slot).wait(), c)[1], 0)

    slot = s % 2

    @pl.when(tc == 0)
    def _():
        @pl.when(s == 0)
        def _():
            start_slab(0, 0)

        wait_slab(s, slot)

        @pl.when(s + 1 < n_slabs)
        def _():
            start_slab(s + 1, 1 - slot)

        acc_ref[...] = jnp.zeros_like(acc_ref)

    for tl in range(T_PER_STEP):
        t = tc * T_PER_STEP + tl
        for kv in range(2):
            xt = jnp.concatenate(
                [slab_ref[slot, pl.ds(t * KV_ROWS + kv * N_KV + g, nblk, stride=BLK * KV_ROWS), :]
                 for g in range(N_KV)], axis=0)
            acc_ref[kv] += _dot(xt, w1_ref[kv, tl * HD_A:(tl + 1) * HD_A, :])

    @pl.when(tc == pl.num_programs(1) - 1)
    def _():
        for kv in range(2):
            o = _dot(_gelu_tanh(acc_ref[kv] + b1_ref[kv]), w2_ref[kv])
            for g in range(N_KV):
                o_ref[kv, g] = o[g * nblk:(g + 1) * nblk]


def compress_paged(page_table, pool2d, page0, w1b, b1, w2):
    bd, n_pages = page_table.shape
    assert n_pages % SLAB_PAGES == 0
    n_slabs = bd * n_pages // SLAB_PAGES
    nblk = SLAB_PAGES * (PAGE // BLK)
    grid_spec = pltpu.PrefetchScalarGridSpec(
        num_scalar_prefetch=1,
        grid=(n_slabs, BLK // T_PER_STEP),
        in_specs=[pl.BlockSpec(memory_space=pl.ANY),
                  pl.BlockSpec((2, T_PER_STEP * HD_A, CMP_HID), lambda s, tc, pt: (0, tc, 0)),
                  pl.BlockSpec((2, 1, CMP_HID), lambda s, tc, pt: (0, 0, 0)),
                  pl.BlockSpec((2, CMP_HID, HD_A), lambda s, tc, pt: (0, 0, 0))],
        out_specs=pl.BlockSpec((2, N_KV, nblk, HD_A), lambda s, tc, pt: (0, 0, s, 0)),
        scratch_shapes=[pltpu.VMEM((2, SLAB_PAGES * PAGE_ROWS, HD_A), F32),
                        pltpu.VMEM((2, N_KV * nblk, CMP_HID), F32),
                        pltpu.SemaphoreType.DMA((2,))],
    )
    return pl.pallas_call(
        functools.partial(_compress_paged_kernel, page0=page0),
        out_shape=jax.ShapeDtypeStruct((2, N_KV, n_slabs * nblk, HD_A), F32),
        grid_spec=grid_spec,
        compiler_params=_cparams(("arbitrary", "arbitrary")),
        name="compress_paged",
    )(page_table.reshape(-1), pool2d, w1b, b1.reshape(2, 1, CMP_HID), w2)


SEL_PAD = 128


def _heads_on_sublanes(q_ref):
    q = q_ref[...].astype(F32)
    row = lax.broadcasted_iota(I32, (SUBLANE, HD_A), 0)
    q8 = jnp.zeros((SUBLANE, HD_A), F32)
    for r in range(GQA_R):
        q8 = jnp.where(row == r, jnp.broadcast_to(q[:, r * HD_A:(r + 1) * HD_A], (SUBLANE, HD_A)), q8)
    return q8


def _slopes_on_sublanes(slopes_ref, g):
    row = lax.broadcasted_iota(I32, (SUBLANE, 1), 0)
    slope = jnp.zeros((SUBLANE, 1), F32)
    for r in range(GQA_R):
        slope = jnp.where(row == r, slopes_ref[g * GQA_R + r], slope)
    return slope


def _softmax_rows(s, valid):
    sm = jnp.where(valid, s, NEG)
    m = jnp.max(sm, axis=-1, keepdims=True)
    e = jnp.where(valid, jnp.exp(sm - m), 0.0)
    return e, jnp.sum(e, axis=-1, keepdims=True)


def _nsa_sample_cmp_kernel(slopes_ref, q_ref, kc_ref, vc_ref, o_ref, idx_ref, *, past, nb_real):
    g = pl.program_id(1)
    nbp = kc_ref.shape[0]
    kc = kc_ref[...]
    vc = vc_ref[...]
    q8 = _heads_on_sublanes(q_ref)
    slope = _slopes_on_sublanes(slopes_ref, g)
    head = lax.broadcasted_iota(I32, (SUBLANE, 1), 0) < GQA_R
    jrow = lax.broadcasted_iota(I32, (1, nbp), 1)
    dist = past - ((jrow + 1) * BLK - 1)
    valid = (dist >= 0) & (jrow < nb_real)
    s = _dot_nt(q8, kc) * SCALE_A - slope * dist.astype(F32)
    e, den = _softmax_rows(s, valid)
    p = e / _safe(den)
    o = _dot(p, vc)
    for r in range(GQA_R):
        o_ref[:, r * HD_A:(r + 1) * HD_A] = o[r:r + 1, :]

    imp = jnp.sum(jnp.where(head, p, 0.0), axis=0, keepdims=True)
    cur = past // BLK
    score_r = jnp.where(jrow < cur, imp, jnp.where(jrow == cur, GQA_R + 1.0, -1.0))
    score_r = jnp.where(jrow < nb_real, score_r, -2.0)
    ii = lax.broadcasted_iota(I32, (nbp, nbp), 0)
    jj = lax.broadcasted_iota(I32, (nbp, nbp), 1)
    sc_b = jnp.broadcast_to(score_r, (nbp, nbp))
    score_c = jnp.sum(jnp.where(ii == jj, sc_b, 0.0), axis=1, keepdims=True)
    beats = (score_c > sc_b) | ((score_c == sc_b) & (ii < jj))
    rank_r = jnp.sum(beats.astype(F32), axis=0, keepdims=True)
    nn = lax.broadcasted_iota(I32, (SEL_PAD, nbp), 0).astype(F32)
    jn = lax.broadcasted_iota(I32, (SEL_PAD, nbp), 1).astype(F32)
    hit = jnp.broadcast_to(rank_r, (SEL_PAD, nbp)) == nn
    idx_c = jnp.sum(jnp.where(hit, jn, 0.0), axis=1, keepdims=True)
    idx_ref[...] = jnp.broadcast_to(idx_c, (SEL_PAD, LANE)).astype(I32)


def nsa_sample_cmp(slopes, ppb3, cmp_kv_s, past, nb_real):
    bd = ppb3.shape[0]
    nbp = cmp_kv_s.shape[3]
    grid_spec = pltpu.PrefetchScalarGridSpec(
        num_scalar_prefetch=1,
        grid=(bd, N_KV),
        in_specs=[pl.BlockSpec((None, 1, GQA_R * HD_A), lambda bi, g, s: (bi, 0, g)),
                  pl.BlockSpec((None, None, None, nbp, HD_A), lambda bi, g, s: (0, g, bi, 0, 0)),
                  pl.BlockSpec((None, None, None, nbp, HD_A), lambda bi, g, s: (1, g, bi, 0, 0))],
        out_specs=[pl.BlockSpec((None, 1, GQA_R * HD_A), lambda bi, g, s: (bi, 0, g)),
                   pl.BlockSpec((None, None, SEL_PAD, LANE), lambda bi, g, s: (bi, g, 0, 0))],
    )
    return pl.pallas_call(
        functools.partial(_nsa_sample_cmp_kernel, past=past, nb_real=nb_real),
        out_shape=[jax.ShapeDtypeStruct((bd, 1, D_A), F32),
                   jax.ShapeDtypeStruct((bd, N_KV, SEL_PAD, LANE), I32)],
        grid_spec=grid_spec,
        compiler_params=_cparams(("parallel", "parallel")),
        name="nsa_sample_cmp",
    )(slopes, ppb3, cmp_kv_s, cmp_kv_s)


W_PAD = LANE


def _nsa_sample_sel_kernel(idx_ref, pt_ref, slopes_ref, q_ref, pool_hbm, kn_ref, vn_ref,
                           wbuf_ref, kwn_ref, vwn_ref, gate_ref, oc_ref, o_ref,
                           blk_ref, ksel_ref, vsel_ref, kwin_ref, vwin_ref, sem, *, past, page0, n_pages):
    bi = pl.program_id(0)
    g = pl.program_id(1)
    nb_past = past // BLK
    bpp = PAGE // BLK
    blk_rows = BLK * KV_ROWS
    base = (bi * N_KV + g) * N_SEL

    def block_copy(n):
        ip = jnp.clip(idx_ref[base + n], 0, nb_past - 1)
        phys = pt_ref[bi * n_pages + ip // bpp]
        row0 = ((page0 + phys) * bpp + ip % bpp) * blk_rows
        return pltpu.make_async_copy(pool_hbm.at[pl.ds(row0, blk_rows), :], blk_ref.at[n], sem)

    for n in range(N_SEL):
        block_copy(n).start()
    for n in range(N_SEL):
        block_copy(n).wait()
    first = lax.broadcasted_iota(I32, (BLK, HD_A), 0) == 0
    k_new = jnp.where(first, kn_ref[...], 0.0)
    v_new = jnp.where(first, vn_ref[...], 0.0)
    for n in range(N_SEL):
        is_past = idx_ref[base + n] < nb_past
        blk = blk_ref.at[n]
        ksel_ref[n * BLK:(n + 1) * BLK, :] = jnp.where(is_past, blk[pl.ds(g, BLK, stride=KV_ROWS), :], k_new)
        vsel_ref[n * BLK:(n + 1) * BLK, :] = jnp.where(is_past, blk[pl.ds(N_KV + g, BLK, stride=KV_ROWS), :], v_new)

    q8 = _heads_on_sublanes(q_ref)
    slope = _slopes_on_sublanes(slopes_ref, g)
    cur = past // BLK
    lane = lax.broadcasted_iota(I32, (1, N_SEL * BLK), 1)
    slot = lane // BLK
    idx_row = jnp.zeros((1, N_SEL * BLK), I32)
    for m in range(N_SEL):
        idx_row = jnp.where(slot == m, idx_ref[base + m], idx_row)
    dist = past - (idx_row * BLK + lane % BLK)
    valid = (idx_row <= cur) & (dist >= 0)
    s = _dot_nt(q8, ksel_ref[...]) * SCALE_A - slope * dist.astype(F32)
    e, den = _softmax_rows(s, valid)
    o_sel = _dot(e / _safe(den), vsel_ref[...])
    wb = wbuf_ref.shape[0] // KV_ROWS
    kwin_ref[0:wb, :] = wbuf_ref[pl.ds(g, wb, stride=KV_ROWS), :]
    vwin_ref[0:wb, :] = wbuf_ref[pl.ds(N_KV + g, wb, stride=KV_ROWS), :]
    first_w = lax.broadcasted_iota(I32, (W_PAD, HD_A), 0) == 0
    kwin_ref[wb:wb + W_PAD, :] = jnp.where(first_w, kwn_ref[...], 0.0)
    vwin_ref[wb:wb + W_PAD, :] = jnp.where(first_w, vwn_ref[...], 0.0)
    lane_w = lax.broadcasted_iota(I32, (1, wb + W_PAD), 1)
    dist_w = wb - lane_w
    valid_w = (dist_w >= 0) & (dist_w < WINDOW)
    s = _dot_nt(q8, kwin_ref[...]) * SCALE_A - slope * dist_w.astype(F32)
    e, den = _softmax_rows(s, valid_w)
    o_win = _dot(e / _safe(den), vwin_ref[...])
    gates = _sigmoid(gate_ref[...])
    for r in range(GQA_R):
        o = (gates[:, 3 * r:3 * r + 1] * oc_ref[:, r * HD_A:(r + 1) * HD_A]
             + gates[:, 3 * r + 1:3 * r + 2] * o_sel[r:r + 1, :]
             + gates[:, 3 * r + 2:3 * r + 3] * o_win[r:r + 1, :])
        o_ref[:, r * HD_A:(r + 1) * HD_A] = o.astype(o_ref.dtype)


def nsa_sample_sel(idx_flat, page_table, slopes, ppb3, pp3, pool2d, page0, win2d, seq0, wb, o_cmp, past):
    bd = ppb3.shape[0]
    n_pages = page_table.shape[1]
    kvb = A_KV // HD_A

    def new_spec(branch, kv):
        return pl.BlockSpec((None, 1, HD_A), lambda bi, g, idx, pt, s: (bi, 0, kvb + branch * 4 + kv * 2 + g))

    grid_spec = pltpu.PrefetchScalarGridSpec(
        num_scalar_prefetch=3,
        grid=(bd, N_KV),
        in_specs=[pl.BlockSpec((None, 1, GQA_R * HD_A), lambda bi, g, idx, pt, s: (bi, 0, g)),
                  pl.BlockSpec(memory_space=pl.ANY), new_spec(1, 0), new_spec(1, 1),
                  pl.BlockSpec((wb * KV_ROWS, HD_A), lambda bi, g, idx, pt, s: (seq0 + bi, 0)),
                  new_spec(2, 0), new_spec(2, 1),
                  pl.BlockSpec((None, 1, LANE), lambda bi, g, idx, pt, s: (bi, 0, A_GATE // LANE + g)),
                  pl.BlockSpec((None, 1, GQA_R * HD_A), lambda bi, g, idx, pt, s: (bi, 0, g))],
        out_specs=pl.BlockSpec((None, 1, GQA_R * HD_A), lambda bi, g, idx, pt, s: (bi, 0, g)),
        scratch_shapes=[pltpu.VMEM((N_SEL, BLK * KV_ROWS, HD_A), F32),
                        pltpu.VMEM((N_SEL * BLK, HD_A), F32), pltpu.VMEM((N_SEL * BLK, HD_A), F32),
                        pltpu.VMEM((wb + W_PAD, HD_A), F32), pltpu.VMEM((wb + W_PAD, HD_A), F32),
                        pltpu.SemaphoreType.DMA(())],
    )
    return pl.pallas_call(
        functools.partial(_nsa_sample_sel_kernel, past=past, page0=page0, n_pages=n_pages),
        out_shape=jax.ShapeDtypeStruct((bd, 1, D_A), BF16),
        grid_spec=grid_spec,
        compiler_params=_cparams(("arbitrary", "arbitrary")),
        name="nsa_sample_sel",
    )(idx_flat, page_table.reshape(-1), slopes, ppb3, pool2d, pp3, pp3, win2d, pp3, pp3, pp3, o_cmp)


def _cumsum_rows(x):
    c = x.shape[0]
    row = lax.broadcasted_iota(I32, x.shape, 0)
    sh = 1
    while sh < c:
        x = x + jnp.where(row >= sh, pltpu.roll(x, sh, 0), 0.0)
        sh *= 2
    return x


def _split2(x):
    hi = x.astype(BF16)
    return hi, (x - hi.astype(F32)).astype(BF16)


def _dot2(a, b, dims=(((1,), (0,)), ((), ()))):
    ah, al = _split2(a)
    bh, bl = _split2(b)
    d = lambda x, y: lax.dot_general(x, y, dims, preferred_element_type=F32)
    return d(ah, bh) + (d(ah, bl) + d(al, bh))


_dot_t = _dot2
NT = (((1,), (1,)), ((), ()))
TN = (((0,), (0,)), ((), ()))
CHUNK_LB = 8
STATE_GROUP = 4


def _rwkv_chunk_kernel(pr_ref, pk_ref, pv_ref, pl_ref, qr_ref, qk_ref, qv_ref, ql_ref,
                       sr_ref, sk_ref, sv_ref, sl_ref,
                       mur_ref, muk_ref, muv_ref, mul_ref, w0_ref, w2_ref, a0_ref, a2_ref, g2_ref,
                       kkp_ref, kap_ref, rkp_ref, gnb_ref,
                       y0_ref, rt_ref, bonus_ref, gate_ref, mm_ref, g0_ref, *, n_valid):
    ci = pl.program_id(2)
    c = pr_ref.shape[0]
    row = lax.broadcasted_iota(I32, (c, 1), 0)
    first_chunk = ci == 0

    def mix(p_ref, q_ref, s_ref, mu_ref):
        p = p_ref[...]
        prev = jnp.where(first_chunk, s_ref[...], q_ref[SUBLANE - 1:SUBLANE, :])
        shifted = jnp.where(row == 0, prev, pltpu.roll(p, 1, 0))
        return p + mu_ref[...] * (shifted - p)

    xr = mix(pr_ref, qr_ref, sr_ref, mur_ref)
    xk = mix(pk_ref, qk_ref, sk_ref, muk_ref)
    xv = mix(pv_ref, qv_ref, sv_ref, muv_ref)
    xl = mix(pl_ref, ql_ref, sl_ref, mul_ref)

    dw = xl[:, 0:LORA_W]
    da = xl[:, LORA_W:LORA_W + LORA_A]
    dg = xl[:, LORA_W + LORA_A:LORA_W + LORA_A + LORA_G]
    wlog = -_softplus(-(w0_ref[...] + _dot(jnp.tanh(dw), w2_ref[...]))) - 0.5
    logdec = -jnp.exp(wlog)
    a = _sigmoid(a0_ref[...] + _dot(da, a2_ref[...]))
    gate_ref[...] = _dot(_sigmoid(dg), g2_ref[...])
    kkv = xk * kkp_ref[...]
    kmod = xk * (1.0 + (a - 1.0) * kap_ref[...])
    if n_valid < c:
        live = row < n_valid
        logdec = jnp.where(live, logdec, 0.0)
        kmod = jnp.where(live, kmod, 0.0)
        a = jnp.where(live, a, 0.0)
        xv = jnp.where(live, xv, 0.0)
    cum = _cumsum_rows(logdec)
    cum_ex = cum - logdec
    cum_end = cum[c - 1:c, :]

    ti = lax.broadcasted_iota(I32, (c, c), 0)
    si = lax.broadcasted_iota(I32, (c, c), 1)
    lower_strict = ti > si
    lower_incl = ti >= si
    eye_c = (ti == si).astype(F32)
    eye_k = (lax.broadcasted_iota(I32, (HD_B, HD_B), 0) == lax.broadcasted_iota(I32, (HD_B, HD_B), 1)).astype(F32)
    rk_all = rkp_ref[...]
    gnb = gnb_ref[...]

    heads = range(pr_ref.shape[1] // HD_B)
    sls = [slice(h * HD_B, (h + 1) * HD_B) for h in heads]
    r_ = [xr[:, s] for s in sls]
    k_ = [kmod[:, s] for s in sls]
    v_ = [xv[:, s] for s in sls]
    kk_ = [kkv[:, s] for s in sls]
    kk_ = [x / jnp.maximum(jnp.sqrt(jnp.sum(x * x, axis=-1, keepdims=True)), 1e-12) for x in kk_]
    b_ = [kk_[h] * a[:, sls[h]] for h in heads]
    kap_ = [kk_[h] * jnp.exp(cum_ex[:, sls[h]]) for h in heads]
    rt_ = [r_[h] * jnp.exp(cum[:, sls[h]]) for h in heads]
    inv_ = [jnp.exp(-cum[:, sls[h]]) for h in heads]
    paired = 2 * c == LANE and len(heads) % 2 == 0
    kb_first = [paired and h % 2 == 0 for h in heads]
    amat = [_dot2(jnp.concatenate([kap_[h], rt_[h]], axis=0),
                  jnp.concatenate([b_[h] * inv_[h], k_[h] * inv_[h]] if kb_first[h]
                                  else [k_[h] * inv_[h], b_[h] * inv_[h]], axis=0), NT) for h in heads]
    kcol = [slice(c, 2 * c) if kb_first[h] else slice(0, c) for h in heads]
    bcol = [slice(0, c) if kb_first[h] else slice(c, 2 * c) for h in heads]
    a_kk = [jnp.where(lower_strict, amat[h][0:c, kcol[h]], 0.0) for h in heads]
    a_rk = [jnp.where(lower_incl, amat[h][c:2 * c, kcol[h]], 0.0) for h in heads]
    a_rb = [jnp.where(lower_incl, amat[h][c:2 * c, bcol[h]], 0.0) for h in heads]
    av = [_dot2(jnp.concatenate([a_kk[h], a_rk[h]], axis=0), v_[h]) for h in heads]

    def level_mask(rows_i, cols_i, w):
        return (rows_i // (2 * w) == cols_i // (2 * w)) & ((rows_i // w) % 2 == 1) & ((cols_i // w) % 2 == 0)

    if paired:
        tp = lax.broadcasted_iota(I32, (c, LANE), 0)
        lane_p = lax.broadcasted_iota(I32, (c, LANE), 1)
        sp = lane_p % c
        left = lane_p < c

        left_b = left.astype(BF16)
        right_b = 1.0 - left_b

        def blockdiag(x):
            return jnp.concatenate([x * left_b, x * right_b], axis=0)

        def dot3p(x_hi, x_lo, y_hi, y_lo):
            d = lambda a_, b_: lax.dot_general(a_, b_, (((1,), (0,)), ((), ())), preferred_element_type=F32)
            return d(x_hi, y_hi) + (d(x_hi, y_lo) + d(x_lo, y_hi))

        pairs = range(len(heads) // 2)
        l_pair = [jnp.where(tp > sp, jnp.where(left, amat[2 * p][0:c, :], amat[2 * p + 1][0:c, :]), 0.0)
                  for p in pairs]
        l_split = [_split2(m) for m in l_pair]
        tpair = [(tp == sp).astype(F32) - jnp.where(((tp % 2) == 1) & (sp == tp - 1), m, 0.0) for m in l_pair]
        w = 2
        while w < c:
            off_b = level_mask(tp, sp, w).astype(BF16)
            t_split = [_split2(m) for m in tpair]
            ld = [dot3p(l_split[p][0] * off_b, l_split[p][1] * off_b,
                        blockdiag(t_split[p][0]), blockdiag(t_split[p][1])) for p in pairs]
            ld_split = [_split2(m) for m in ld]
            tpair = [tpair[p] - dot3p(t_split[p][0], t_split[p][1],
                                      blockdiag(ld_split[p][0]), blockdiag(ld_split[p][1])) for p in pairs]
            w *= 2
        tinv = [tpair[h // 2][:, (h % 2) * c:(h % 2 + 1) * c] for h in heads]
    else:
        a_kb = [jnp.where(lower_strict, amat[h][0:c, bcol[h]], 0.0) for h in heads]
        tinv = [eye_c - jnp.where(((ti % 2) == 1) & (si == ti - 1), m, 0.0) for m in a_kb]
        w = 2
        while w < c:
            off = level_mask(ti, si, w)
            ld = [_dot_t(jnp.where(off, a_kb[h], 0.0), tinv[h]) for h in heads]
            tinv = [tinv[h] - _dot_t(tinv[h], ld[h]) for h in heads]
            w *= 2

    tx = [_dot2(tinv[h], jnp.concatenate([av[h][0:c], kap_[h]], axis=1)) for h in heads]
    arb_tx = [_dot2(a_rb[h], tx[h]) for h in heads]
    dec_end = [jnp.exp(cum_end[:, sls[h]] - cum[:, sls[h]]) for h in heads]
    k_end = [k_[h] * dec_end[h] for h in heads]
    b_end = [b_[h] * dec_end[h] for h in heads]
    g0 = [_dot2(jnp.concatenate([v_[h], -tx[h][:, 0:HD_B]], axis=0),
                jnp.concatenate([k_end[h], b_end[h]], axis=0), TN) for h in heads]
    ktb = [_dot2(tx[h][:, HD_B:2 * HD_B], b_end[h], TN) for h in heads]
    for h in heads:
        s = sls[h]
        y0_ref[:, s] = av[h][c:2 * c] - arb_tx[h][:, 0:HD_B]
        rt_ref[:, s] = rt_[h] - arb_tx[h][:, HD_B:2 * HD_B]
        bonus_ref[:, s] = jnp.sum(r_[h] * k_[h] * rk_all[:, s], axis=-1, keepdims=True) * v_[h] + gnb[:, s]
        mm_ref[h] = eye_k * jnp.exp(cum_end[:, s]) - ktb[h]
        g0_ref[h] = g0[h]


def _rwkv_state_kernel(y0_ref, rt_ref, bonus_ref, gate_ref, mm_ref, g0_ref, s0_ref, gng_ref,
                       o_ref, sT_ref, st_ref):
    @pl.when(pl.program_id(1) == 0)
    def _():
        st_ref[...] = s0_ref[...]

    gng = gng_ref[...]
    for h0 in range(0, N_HEADS_B, STATE_GROUP):
        heads = range(h0, h0 + STATE_GROUP)
        sls = {h: slice(h * HD_B, (h + 1) * HD_B) for h in heads}
        s0 = {h: st_ref[h] for h in heads}
        s_new = {h: _dot2(s0[h], mm_ref[h]) + g0_ref[h] for h in heads}
        y = {h: y0_ref[:, sls[h]] + _dot2(rt_ref[:, sls[h]], s0[h], NT) for h in heads}
        for h in heads:
            st_ref[h] = s_new[h]
            sT_ref[h] = s_new[h]
        mean = {h: jnp.mean(y[h], axis=-1, keepdims=True) for h in heads}
        dev = {h: y[h] - mean[h] for h in heads}
        var = {h: jnp.mean(jnp.square(dev[h]), axis=-1, keepdims=True) for h in heads}
        for h in heads:
            yn = dev[h] * lax.rsqrt(var[h] + GN_EPS) * gng[:, sls[h]]
            o_ref[:, sls[h]] = ((yn + bonus_ref[:, sls[h]]) * gate_ref[:, sls[h]]).astype(o_ref.dtype)


def rwkv7(pp, shift0, wkv0, prm, bsz, t, chunk, n_valid):
    nc = t // chunk
    lb = CHUNK_LB * LANE
    ngrp = D_B // lb
    hs = lb // HD_B
    assert A_RKV % lb == 0
    rb = A_RKV // lb
    per = D_B // lb
    lo_blk = A_LO // (2 * LANE)
    lo3 = 3 * D_B // (2 * LANE)
    sub = chunk // SUBLANE

    def rows(off):
        return pl.BlockSpec((chunk, lb), lambda b, hg, ci: (b * nc + ci, rb + off * per + hg))

    def prev_rows(off):
        return pl.BlockSpec((SUBLANE, lb),
                            lambda b, hg, ci: (jnp.maximum((b * nc + ci) * sub - 1, 0), rb + off * per + hg))

    def srow(off):
        return pl.BlockSpec((None, 1, lb), lambda b, hg, ci: (b, 0, off * per + hg))

    def prow(off):
        return pl.BlockSpec((1, lb), lambda b, hg, ci: (0, off * per + hg))

    def per_head(shape0):
        return pl.BlockSpec((shape0, lb), lambda b, hg, ci: (0, hg))

    lo_spec = lambda rws, imap: pl.BlockSpec((rws, 2 * LANE), imap)
    in_specs = [rows(0), rows(1), rows(2), lo_spec(chunk, lambda b, hg, ci: (b * nc + ci, lo_blk)),
                prev_rows(0), prev_rows(1), prev_rows(2),
                lo_spec(SUBLANE, lambda b, hg, ci: (jnp.maximum((b * nc + ci) * sub - 1, 0), lo_blk)),
                srow(0), srow(1), srow(2), pl.BlockSpec((None, 1, 2 * LANE), lambda b, hg, ci: (b, 0, lo3)),
                prow(0), prow(1), prow(2), lo_spec(1, lambda b, hg, ci: (0, lo3)),
                per_head(1), per_head(LORA_W), per_head(1), per_head(LORA_A), per_head(LORA_G),
                per_head(1), per_head(1), per_head(1), per_head(1)]
    row_out = pl.BlockSpec((chunk, lb), lambda b, hg, ci: (b * nc + ci, hg))
    mat_out = pl.BlockSpec((None, None, hs, HD_B, HD_B), lambda b, hg, ci: (b, ci, hg, 0, 0))
    n = bsz * t
    y0, rt, bonus, gate, mm, g0 = pl.pallas_call(
        functools.partial(_rwkv_chunk_kernel, n_valid=n_valid),
        out_shape=[jax.ShapeDtypeStruct((n, D_B), F32)] * 4
        + [jax.ShapeDtypeStruct((bsz, nc, N_HEADS_B, HD_B, HD_B), F32)] * 2,
        grid=(bsz, ngrp, nc),
        in_specs=in_specs,
        out_specs=[row_out] * 4 + [mat_out] * 2,
        compiler_params=_cparams(("parallel", "parallel", "parallel")),
        name="rwkv_chunk",
    )(pp, pp, pp, pp, pp, pp, pp, pp, shift0, shift0, shift0, shift0,
      prm["mu"], prm["mu"], prm["mu"], prm["mu"], prm["w0"], prm["w2"], prm["a0"], prm["a2"], prm["g2"],
      prm["kk"], prm["ka"], prm["rk"], prm["gn_b"])

    row_in = pl.BlockSpec((chunk, D_B), lambda b, ci: (b * nc + ci, 0))
    mat_in = pl.BlockSpec((None, None, N_HEADS_B, HD_B, HD_B), lambda b, ci: (b, ci, 0, 0, 0))
    state = pl.BlockSpec((None, N_HEADS_B, HD_B, HD_B), lambda b, ci: (b, 0, 0, 0))
    o_b, s_fin = pl.pallas_call(
        _rwkv_state_kernel,
        out_shape=[jax.ShapeDtypeStruct((n, D_B), BF16),
                   jax.ShapeDtypeStruct((bsz, N_HEADS_B, HD_B, HD_B), F32)],
        grid=(bsz, nc),
        in_specs=[row_in, row_in, row_in, row_in, mat_in, mat_in, state,
                  pl.BlockSpec((1, D_B), lambda b, ci: (0, 0))],
        out_specs=[row_in, state],
        scratch_shapes=[pltpu.VMEM((N_HEADS_B, HD_B, HD_B), F32)],
        compiler_params=_cparams(("parallel", "arbitrary")),
        name="rwkv_state",
    )(y0, rt, bonus, gate, mm, g0, wkv0, prm["gn_g"])
    return o_b, s_fin


def _first_lane(cond, lane):
    return jnp.min(jnp.where(cond, lane, 4 * LANE), axis=-1, keepdims=True)


def _router_kernel(h_ref, w_ref, b_ref, eid_ref, wt_ref):
    logits = _dot3(h_ref[...], w_ref[...]) + b_ref[...]
    lane = lax.broadcasted_iota(I32, logits.shape, 1)
    gmask = lane < N_GROUPS
    lg = jnp.where(gmask, logits, NEG)
    eg = jnp.where(gmask, jnp.exp(lg - jnp.max(lg, axis=-1, keepdims=True)), 0.0)
    gp = eg / jnp.sum(eg, axis=-1, keepdims=True)
    g_w = jnp.max(gp, axis=-1, keepdims=True)
    grp = _first_lane(gmask & (gp == g_w), lane)
    lo = N_GROUPS + grp * E_PER_GROUP
    emask = (lane >= lo) & (lane < lo + E_PER_GROUP)
    le = jnp.where(emask, logits, NEG)
    ee = jnp.where(emask, jnp.exp(le - jnp.max(le, axis=-1, keepdims=True)), 0.0)
    ep = ee / jnp.sum(ee, axis=-1, keepdims=True)
    p1 = jnp.max(jnp.where(emask, ep, -1.0), axis=-1, keepdims=True)
    i1 = _first_lane(emask & (ep == p1), lane)
    rest = emask & (lane != i1)
    p2 = jnp.max(jnp.where(rest, ep, -1.0), axis=-1, keepdims=True)
    i2 = _first_lane(rest & (ep == p2), lane)
    tot = p1 + p2
    eid_ref[...] = jnp.where(lane == 0, i1 - N_GROUPS, jnp.where(lane == 1, i2 - N_GROUPS, 0))
    wt_ref[...] = jnp.where(lane == 0, g_w * p1 / tot, jnp.where(lane == 1, g_w * p2 / tot, 0.0))


def router(h2, wr, br):
    n, d = h2.shape
    tm = _row_tile(n, 512)
    return pl.pallas_call(
        _router_kernel,
        out_shape=[jax.ShapeDtypeStruct((n, LANE), I32), jax.ShapeDtypeStruct((n, LANE), F32)],
        grid=(n // tm,),
        in_specs=[pl.BlockSpec((tm, d), lambda i: (i, 0)),
                  pl.BlockSpec((d, LANE), lambda i: (0, 0)),
                  pl.BlockSpec((1, LANE), lambda i: (0, 0))],
        out_specs=[pl.BlockSpec((tm, LANE), lambda i: (i, 0)), pl.BlockSpec((tm, LANE), lambda i: (i, 0))],
        compiler_params=_cparams(("parallel",)),
        name="router",
    )(h2, wr, br)


def _row_copy(src_hbm, dst_vmem, sem, src_row, dst_row):
    return pltpu.make_async_copy(src_hbm.at[pl.ds(src_row * ROW_TILES, ROW_TILES), :],
                                 dst_vmem.at[pl.ds(dst_row * ROW_TILES, ROW_TILES), :], sem)


def _experts_kernel(blk_e_ref, n_used_ref, tok_ref, h_hbm, w1_ref, w3_ref, w2_ref, y_ref,
                    x_even, x_odd, w1b_ref, w3b_ref, w2b_ref, sem):
    i = pl.program_id(0)
    bm = x_even.shape[0] // ROW_TILES
    n_used = n_used_ref[0]
    bufs = (x_even, x_odd)

    def rows(blk, slot, go):
        for r in range(bm):
            go(_row_copy(h_hbm, bufs[slot], sem.at[slot], tok_ref[blk * bm + r], r))

    def block(slot):
        @pl.when(i == 0)
        def _():
            rows(0, 0, lambda cp: cp.start())

        rows(i, slot, lambda cp: cp.wait())

        @pl.when((i == 0) | (blk_e_ref[i] != blk_e_ref[jnp.maximum(i - 1, 0)]))
        def _():
            w1b_ref[...] = w1_ref[...].astype(BF16)
            w3b_ref[...] = w3_ref[...].astype(BF16)
            w2b_ref[...] = w2_ref[...].astype(BF16)

        nxt = jnp.minimum(i + 1, n_used - 1)
        rows(nxt, 1 - slot, lambda cp: cp.start())
        x = jnp.concatenate([_from_row_tiles(bufs[slot], bm, c).astype(BF16) for c in range(ROW_TILES)], axis=1)
        h1 = _dot(x, w1b_ref[...])
        h3 = _dot(x, w3b_ref[...])
        act = h1 * _sigmoid(h1) * h3
        _to_row_tiles(y_ref, _dot(act, w2b_ref[...]))

        @pl.when(i == n_used - 1)
        def _():
            rows(nxt, 1 - slot, lambda cp: cp.wait())

    for slot in range(2):
        pl.when((i < n_used) & (i % 2 == slot))(functools.partial(block, slot))

    @pl.when(i >= n_used)
    def _():
        y_ref[...] = jnp.zeros_like(y_ref)


def experts(h_all, blk_e, n_used, row_tok, w1, w3, w2, l, bm):
    n_blocks = blk_e.shape[0]
    d = D_MODEL
    grid_spec = pltpu.PrefetchScalarGridSpec(
        num_scalar_prefetch=3,
        grid=(n_blocks,),
        in_specs=[pl.BlockSpec(memory_space=pl.ANY),
                  pl.BlockSpec((None, None, d, D_EXPERT), lambda i, be, nu, tk: (l, be[i], 0, 0)),
                  pl.BlockSpec((None, None, d, D_EXPERT), lambda i, be, nu, tk: (l, be[i], 0, 0)),
                  pl.BlockSpec((None, None, D_EXPERT, d), lambda i, be, nu, tk: (l, be[i], 0, 0))],
        out_specs=pl.BlockSpec((bm * ROW_TILES, LANE), lambda i, be, nu, tk: (i, 0)),
        scratch_shapes=[pltpu.VMEM((bm * ROW_TILES, LANE), F32), pltpu.VMEM((bm * ROW_TILES, LANE), F32),
                        pltpu.VMEM((d, D_EXPERT), BF16), pltpu.VMEM((d, D_EXPERT), BF16),
                        pltpu.VMEM((D_EXPERT, d), BF16), pltpu.SemaphoreType.DMA((2,))],
    )
    return pl.pallas_call(
        _experts_kernel,
        out_shape=jax.ShapeDtypeStruct((n_blocks * bm * ROW_TILES, LANE), F32),
        grid_spec=grid_spec,
        compiler_params=_cparams(("arbitrary",)),
        name="experts",
    )(blk_e, n_used, row_tok, h_all, w1, w3, w2)


def _combine_kernel(dest_ref, ys_hbm, wt_ref, x_ref, g_ref, o_ref, buf, sem):
    tm = x_ref.shape[0]
    step = pl.program_id(0) * pl.num_programs(1) + pl.program_id(1)
    n_steps = pl.num_programs(0) * pl.num_programs(1)

    def rows(tile, slot, go):
        for r in range(tm):
            base = (tile * tm + r) * 2
            go(_row_copy(ys_hbm, buf.at[slot, 0], sem.at[slot], dest_ref[base], r))
            go(_row_copy(ys_hbm, buf.at[slot, 1], sem.at[slot], dest_ref[base + 1], r))

    slot = step % 2

    @pl.when(step == 0)
    def _():
        rows(0, 0, lambda cp: cp.start())

    rows(step, slot, lambda cp: cp.wait())
    nxt = jnp.minimum(step + 1, n_steps - 1)
    rows(nxt, 1 - slot, lambda cp: cp.start())
    wt = wt_ref[...]
    w0, w1 = wt[:, 0:1], wt[:, 1:2]
    for c in range(ROW_TILES):
        cs = slice(c * LANE, (c + 1) * LANE)
        moe = w0 * _from_row_tiles(buf.at[slot, 0], tm, c) + w1 * _from_row_tiles(buf.at[slot, 1], tm, c)
        o_ref[:, cs] = x_ref[:, cs] + g_ref[:, cs] * moe

    @pl.when(step == n_steps - 1)
    def _():
        rows(nxt, 1 - slot, lambda cp: cp.wait())


def combine(ys, dest, wts, x, gate):
    b, t, d = x.shape
    tm = _row_tile(t, 128)
    nt = t // tm
    grid_spec = pltpu.PrefetchScalarGridSpec(
        num_scalar_prefetch=1,
        grid=(b, nt),
        in_specs=[pl.BlockSpec(memory_space=pl.ANY),
                  pl.BlockSpec((None, tm, LANE), lambda i, j, ds: (i, j, 0)),
                  pl.BlockSpec((None, tm, d), lambda i, j, ds: (i, j, 0)),
                  pl.BlockSpec((None, 1, d), lambda i, j, ds: (i, 0, 0))],
        out_specs=pl.BlockSpec((None, tm, d), lambda i, j, ds: (i, j, 0)),
        scratch_shapes=[pltpu.VMEM((2, 2, tm * ROW_TILES, LANE), F32), pltpu.SemaphoreType.DMA((2,))],
    )
    return pl.pallas_call(
        _combine_kernel,
        out_shape=jax.ShapeDtypeStruct((b, t, d), F32),
        grid_spec=grid_spec,
        compiler_params=_cparams(("arbitrary", "arbitrary")),
        name="moe_combine",
    )(dest.reshape(-1), ys, wts.reshape(b, t, LANE), x, gate.reshape(b, 1, d))


def _dispatch_tables(eid, bm):
    n = eid.shape[0]
    nk = n * 2
    n_blocks = (nk + N_EXPERTS * (bm - 1) + bm - 1) // bm
    flat_e = eid.reshape(-1)
    onehot = (flat_e[:, None] == jnp.arange(N_EXPERTS, dtype=I32)[None, :]).astype(I32)
    csum = jnp.cumsum(onehot, axis=0)
    counts = csum[-1]
    rank = jnp.sum(onehot * (csum - 1), axis=1)
    padded = (counts + bm - 1) // bm * bm
    pad_end = jnp.cumsum(padded)
    pad_start = pad_end - padded
    dest = (pad_start[flat_e] + rank).astype(I32)
    row_tok = jnp.zeros((n_blocks * bm,), I32).at[dest].set(jnp.arange(nk, dtype=I32) // 2)
    blk_start = jnp.arange(n_blocks, dtype=I32) * bm
    blk_e = jnp.minimum(jnp.sum((pad_end[None, :] <= blk_start[:, None]).astype(I32), axis=1), N_EXPERTS - 1)
    n_used = (pad_end[-1] // bm).astype(I32).reshape(1)
    return blk_e, n_used, row_tok, dest.reshape(n, 2)


def _final_norm_kernel(x_ref, g_ref, o_ref):
    x = x_ref[...]
    o_ref[...] = x * lax.rsqrt(jnp.mean(x * x, axis=-1, keepdims=True) + RMS_EPS) * g_ref[...]


def final_norm(x, g):
    b, t, d = x.shape
    tr = min(t, 256)
    return pl.pallas_call(
        _final_norm_kernel,
        out_shape=jax.ShapeDtypeStruct((b, t, d), F32),
        grid=(b, t // tr),
        in_specs=[pl.BlockSpec((None, tr, d), lambda i, j: (i, j, 0)), pl.BlockSpec((1, d), lambda i, j: (0, 0))],
        out_specs=pl.BlockSpec((None, tr, d), lambda i, j: (i, j, 0)),
        compiler_params=_cparams(("parallel", "parallel")),
        name="final_norm",
    )(x, g.reshape(1, d))


def _align_in_cols(w, axis):
    def take(a, b_):
        return lax.slice_in_dim(w, a, b_, axis=axis)

    def zeros(nz):
        shp = list(w.shape)
        shp[axis] = nz
        return jnp.zeros(shp, w.dtype)

    gate_parts = []
    for g in range(N_KV):
        gate_parts += [take(OFF_GATE_A + g * 3 * GQA_R, OFF_GATE_A + (g + 1) * 3 * GQA_R), zeros(LANE - 3 * GQA_R)]
    parts = [take(0, OFF_GATE_A)] + gate_parts + [take(OFF_RWKV + 3 * D_B, OFF_MERGE), zeros(RW_W - SHIFT_W),
                                                   take(OFF_RWKV, OFF_RWKV + 3 * D_B),
                                                   take(OFF_MERGE, OFF_MERGE + 2 * D_MODEL)]
    return jnp.concatenate(parts, axis=axis)


def _rwkv_cols(p):
    return jnp.concatenate([p[..., A_RKV:A_RKV + 3 * D_B], p[..., A_LO:A_LO + SHIFT_W - 3 * D_B]], axis=-1)


def _pad_lanes(v, width):
    return jnp.pad(v, [(0, 0)] * (v.ndim - 1) + [(0, width - v.shape[-1])])


def kernel(x_prompt, x_sample, cache_cmp, cache_slc, cache_win, state_shift, state_wkv, page_table, c_prompt, c_sample, ln1_g, ln2_g, ada_w, ada_b, w_in, cmp_w1, cmp_b1, cmp_w2, rwkv_mu, rwkv_w0, rwkv_w2, rwkv_a0, rwkv_a2, rwkv_g2, rwkv_kk, rwkv_ka, rwkv_rk, rwkv_gn_g, rwkv_gn_b, w_branch, w_out, router_g_w, router_g_b, router_e_w, router_e_b, exp_w1, exp_w3, exp_w2, final_g):
    depth = w_in.shape[0]
    bp, t, d = x_prompt.shape
    bd, ts, _ = x_sample.shape
    assert ts == 1 and t % KC == 0 and t >= WINDOW + QB
    n_pages = page_table.shape[1]
    past = n_pages * PAGE
    wb = cache_win.shape[2]
    n_phys = cache_cmp.shape[1]
    n_p = bp * t

    slopes = jnp.exp2(-8.0 * (jnp.arange(N_HEADS_A, dtype=F32) + 1.0) / N_HEADS_A)
    c_rows = bp + bd
    c16 = jnp.zeros(((c_rows + SUBLANE - 1) // SUBLANE * SUBLANE, d), F32).at[:bp].set(c_prompt).at[bp:c_rows].set(c_sample)
    mod = adaln(c16, ada_w, ada_b).reshape(depth, c16.shape[0], 6, d)

    xp, xs = x_prompt, x_sample
    outs = {k: [] for k in ("cmp_p", "slc_p", "win_p", "shf_p", "wkv_p", "cmp_s", "slc_s", "win_s", "shf_s", "wkv_s")}
    pool_cmp2d = cache_cmp.reshape(-1, HD_A)
    pool_slc2d = cache_slc.reshape(-1, HD_A)
    win2d = cache_win.reshape(-1, HD_A)

    for l in range(depth):
        mp, ms = mod[l, :bp], mod[l, bp:c_rows]
        w_in_al = _align_in_cols(w_in[l], 1).astype(BF16)
        w1b = cmp_w1[l].astype(BF16)
        w_out_b = w_out[l].astype(BF16)
        prm = {
            "mu": _pad_lanes(rwkv_mu[l][None, :], RW_W),
            "w0": rwkv_w0[l][None, :], "w2": rwkv_w2[l], "a0": rwkv_a0[l][None, :], "a2": rwkv_a2[l],
            "g2": rwkv_g2[l], "kk": rwkv_kk[l][None, :], "ka": rwkv_ka[l][None, :],
            "rk": rwkv_rk[l].reshape(1, D_B), "gn_g": rwkv_gn_g[l][None, :], "gn_b": rwkv_gn_b[l][None, :],
        }
        wr = _pad_lanes(jnp.concatenate([router_g_w[l], router_e_w[l]], axis=1), LANE)
        br = _pad_lanes(jnp.concatenate([router_g_b[l], router_e_b[l]])[None, :], LANE)

        hp_ = norm_mod(xp, ln1_g[l], mp[:, 0], mp[:, 1], BF16).reshape(n_p, d)
        pp, ppb, kv_p = in_proj(hp_, w_in_al)
        cmp_kv = compress(pp, A_KV, 1, w1b, cmp_b1[l], cmp_w2[l])
        o_a = nsa_prompt(slopes, ppb, pp, cmp_kv, bp, t)
        o_b, s_fin = rwkv7(pp, jnp.zeros((bp, 1, RW_W), F32), jnp.zeros((bp, N_HEADS_B, HD_B, HD_B), F32),
                           prm, bp, t, 64, 64)
        merged = branch_merge(o_a, o_b, w_branch, l, pp)
        xp = out_proj_residual(merged.reshape(bp, t, d), w_out_b, xp, mp[:, 2])
        kv_p = kv_p.reshape(3, bp, t, 2, N_KV, HD_A)
        outs["cmp_p"].append(kv_p[0])
        outs["slc_p"].append(kv_p[1])
        outs["win_p"].append(kv_p[2, :, t - min(WINDOW, t):])
        outs["shf_p"].append(_rwkv_cols(pp.reshape(bp, t, IN_AL)[:, -1]))
        outs["wkv_p"].append(s_fin)

        hs_ = norm_mod(xs, ln1_g[l], ms[:, 0], ms[:, 1], BF16).reshape(bd, d)
        ps, psb, kv_s = in_proj(hs_, w_in_al)
        cmp_past = compress_paged(page_table, pool_cmp2d, l * n_phys, w1b, cmp_b1[l], cmp_w2[l])
        new_rows = jnp.zeros((bd, BLK, 2 * KV_W), F32).at[:, 0].set(ps[:, A_KV:A_KV + 2 * KV_W])
        cmp_new = compress(new_rows.reshape(bd * BLK, 2 * KV_W), 0, 1, w1b, cmp_b1[l], cmp_w2[l])
        nb_past = past // BLK
        nb_real = nb_past + 1
        nbp = (nb_real + LANE - 1) // LANE * LANE
        cmp_s = jnp.concatenate([cmp_past.reshape(2, N_KV, bd, nb_past, HD_A), cmp_new[:, :, :, None, :],
                                 jnp.zeros((2, N_KV, bd, nbp - nb_real, HD_A), F32)], axis=3)
        ps3 = ps.reshape(bd, 1, IN_AL)
        psb3 = psb.reshape(bd, 1, A_GATE)
        o_cmp, idx_full = nsa_sample_cmp(slopes, psb3, cmp_s, past, nb_real)
        idx_flat = idx_full[:, :, :N_SEL, 0].reshape(-1)
        o_a_s = nsa_sample_sel(idx_flat, page_table, slopes, psb3, ps3, pool_slc2d, l * n_phys,
                               win2d, l * bd, wb, o_cmp, past).reshape(bd, D_A)
        ps_pad = jnp.zeros((bd, SUBLANE, IN_AL), F32).at[:, 0].set(ps).reshape(bd * SUBLANE, IN_AL)
        o_b_s, s_fin_s = rwkv7(ps_pad, _pad_lanes(state_shift[l], RW_W)[:, None, :], state_wkv[l],
                               prm, bd, SUBLANE, SUBLANE, 1)
        o_b_s = o_b_s.reshape(bd, SUBLANE, D_B)[:, 0]
        merged_s = branch_merge(o_a_s, o_b_s, w_branch, l, ps)
        xs = out_proj_residual(merged_s.reshape(bd, 1, d), w_out_b, xs, ms[:, 2])
        kv_s = kv_s.reshape(3, bd, 1, 2, N_KV, HD_A)
        outs["cmp_s"].append(kv_s[0])
        outs["slc_s"].append(kv_s[1])
        outs["win_s"].append(jnp.concatenate([cache_win[l, :, 1:], kv_s[2]], axis=1))
        outs["shf_s"].append(_rwkv_cols(ps))
        outs["wkv_s"].append(s_fin_s)

        h2p, h_all = norm_mod(xp, ln2_g[l], mp[:, 3], mp[:, 4], F32, tiles_rows=n_p + bd,
                              tiles_into=jnp.zeros(((n_p + bd) * ROW_TILES, LANE), F32))
        h2s, h_all = norm_mod(xs, ln2_g[l], ms[:, 3], ms[:, 4], F32, tiles_rows=n_p + bd, tiles_into=h_all,
                              tiles_row0=n_p)
        eid_p, wt_p = router(h2p.reshape(n_p, d), wr, br)
        eid_s, wt_s = router(h2s.reshape(bd, d), wr, br)
        eid = jnp.concatenate([eid_p[:, :2], eid_s[:, :2]], axis=0)
        blk_e, n_used, row_tok, dest = _dispatch_tables(eid, MOE_BLOCK)
        ys = experts(h_all, blk_e, n_used, row_tok, exp_w1, exp_w3, exp_w2, l, MOE_BLOCK)
        xp = combine(ys, dest[:n_p], wt_p, xp, mp[:, 5])
        xs = combine(ys, dest[n_p:], wt_s, xs, ms[:, 5])

    y_prompt = final_norm(xp, final_g)
    y_sample = final_norm(xs, final_g)
    st = lambda k: jnp.stack(outs[k])
    return (y_prompt, y_sample, st("cmp_p"), st("slc_p"), st("win_p"), st("shf_p"), st("wkv_p"),
            st("cmp_s"), st("slc_s"), st("win_s"), st("shf_s"), st("wkv_s"))
```

```python
import functools

import jax
import jax.numpy as jnp
from jax import lax
from jax.experimental import pallas as pl
from jax.experimental.pallas import tpu as pltpu

F32 = jnp.float32
BF16 = jnp.bfloat16
I32 = jnp.int32

LANE = 128
SUBLANE = 8
VMEM_LIMIT = 56 * 1024 * 1024

D_MODEL = 2048
HD_A = 128
N_HEADS_A = 8
N_KV = 2
GQA_R = 4
BLK = 64
N_SEL = 16
WINDOW = 512
CMP_HID = 256
SCALE_A = HD_A ** -0.5
D_A = 1024
D_B = 1024
HD_B = 64
N_HEADS_B = 16
LORA_W, LORA_A, LORA_G = 64, 64, 32
GN_EPS = HD_B * 1e-5
N_GROUPS = 4
E_PER_GROUP = 8
N_EXPERTS = 32
D_EXPERT = 512
MOE_BLOCK = 128
RMS_EPS = 1e-6
PAGE = 128

Q_W = N_HEADS_A * HD_A
KV_W = N_KV * HD_A
OFF_KV = Q_W
OFF_GATE_A = OFF_KV + 6 * KV_W
OFF_RWKV = OFF_GATE_A + 3 * N_HEADS_A
SHIFT_W = 3 * D_B + LORA_W + LORA_A + LORA_G
OFF_MERGE = OFF_RWKV + SHIFT_W

A_KV = Q_W
A_GATE = A_KV + 6 * KV_W
A_LO = A_GATE + N_KV * LANE
A_RKV = A_LO + 2 * LANE
RW_W = 3 * D_B + 2 * LANE
A_MERGE = A_RKV + 3 * D_B
IN_AL = A_MERGE + 2 * D_MODEL
NEG = -1e30


def _cparams(sem):
    return pltpu.CompilerParams(dimension_semantics=sem, vmem_limit_bytes=VMEM_LIMIT)


def _dot(a, b, dims=(((1,), (0,)), ((), ()))):
    return lax.dot_general(a.astype(BF16), b.astype(BF16), dims, preferred_element_type=F32)


def _dot_nt(a, b):
    return _dot(a, b, (((1,), (1,)), ((), ())))


TN_DIMS = (((0,), (0,)), ((), ()))


def _dot_tn(a, b):
    return _dot(a, b, TN_DIMS)


def _split3(x):
    h = x.astype(BF16)
    r1 = x - h.astype(F32)
    m = r1.astype(BF16)
    lo = (r1 - m.astype(F32)).astype(BF16)
    return h, m, lo


def _dot3(a, b, dims=(((1,), (0,)), ((), ()))):
    ah, am, al = _split3(a)
    bh, bm, bl = _split3(b)
    d = lambda x, y: lax.dot_general(x, y, dims, preferred_element_type=F32)
    return (d(ah, bh) + (d(ah, bm) + d(am, bh))) + ((d(am, bm) + d(ah, bl)) + d(al, bh))


def _sigmoid(x):
    return 1.0 / (1.0 + jnp.exp(-x))


def _softplus(x):
    return jnp.maximum(x, 0.0) + jnp.log(1.0 + jnp.exp(-jnp.abs(x)))


def _adaln_kernel(c_ref, w_ref, b_ref, o_ref):
    c = c_ref[...]
    h = c * _sigmoid(c)
    o_ref[...] = _dot(h, w_ref[...]) + b_ref[...]


def adaln(c16, ada_w, ada_b):
    depth, d, n = ada_w.shape
    tn = 1024
    return pl.pallas_call(
        _adaln_kernel,
        out_shape=jax.ShapeDtypeStruct((depth, c16.shape[0], n), F32),
        grid=(depth, n // tn),
        in_specs=[pl.BlockSpec(c16.shape, lambda l, j: (0, 0)),
                  pl.BlockSpec((None, d, tn), lambda l, j: (l, 0, j)),
                  pl.BlockSpec((None, 1, tn), lambda l, j: (l, 0, j))],
        out_specs=pl.BlockSpec((None, c16.shape[0], tn), lambda l, j: (l, 0, j)),
        compiler_params=_cparams(("parallel", "parallel")),
        name="adaln",
    )(c16, ada_w, ada_b.reshape(depth, 1, n))


ROW_W = D_MODEL
ROW_TILES = D_MODEL // ROW_W


def _to_row_tiles(ref, val):
    rows = val.shape[0]
    if ROW_TILES == 1:
        ref[...] = val
        return
    for c in range(ROW_TILES):
        ref[pl.ds(c, rows, stride=ROW_TILES), :] = val[:, c * ROW_W:(c + 1) * ROW_W]


def _from_row_tiles(ref, rows, c):
    if ROW_TILES == 1:
        return ref[...]
    return ref[pl.ds(c, rows, stride=ROW_TILES), :]


def _norm_kernel(x_ref, g_ref, sh_ref, sc_ref, o_ref):
    x = x_ref[...]
    y = x * lax.rsqrt(jnp.mean(x * x, axis=-1, keepdims=True) + RMS_EPS)
    y = y * g_ref[...]
    o_ref[...] = (y * (1.0 + sc_ref[...]) + sh_ref[...]).astype(o_ref.dtype)


def _norm_tiles_kernel(x_ref, g_ref, sh_ref, sc_ref, *refs):
    o_ref, tiles_ref = refs[-2:]
    _norm_kernel(x_ref, g_ref, sh_ref, sc_ref, o_ref)
    rows = o_ref.shape[0] * ROW_TILES
    if tiles_ref.shape[0] == rows:
        _to_row_tiles(tiles_ref, o_ref[...])
    else:
        tiles_ref[pl.ds((pl.program_id(0) % SUBLANE) * rows, rows), :] = o_ref[...]


def norm_mod(x, g, shift, scale, out_dtype, tiles_rows=0, tiles_into=None, tiles_row0=0):
    b, t, d = x.shape
    tr = min(t, 256)
    nt = t // tr
    in_specs = [pl.BlockSpec((None, tr, d), lambda i, j: (i, j, 0)),
                pl.BlockSpec((1, d), lambda i, j: (0, 0)),
                pl.BlockSpec((None, 1, d), lambda i, j: (i, 0, 0)),
                pl.BlockSpec((None, 1, d), lambda i, j: (i, 0, 0))]
    args = [x, g.reshape(1, d), shift.reshape(b, 1, d), scale.reshape(b, 1, d)]
    row_spec = pl.BlockSpec((None, tr, d), lambda i, j: (i, j, 0))
    if not tiles_rows:
        return pl.pallas_call(
            _norm_kernel, out_shape=jax.ShapeDtypeStruct((b, t, d), out_dtype), grid=(b, nt),
            in_specs=in_specs, out_specs=row_spec,
            compiler_params=_cparams(("parallel", "parallel")), name="norm_mod")(*args)
    assert out_dtype == F32
    if tr * ROW_TILES % SUBLANE == 0:
        assert tiles_row0 % tr == 0
        blk0 = tiles_row0 // tr
        tiles_spec = pl.BlockSpec((tr * ROW_TILES, ROW_W), lambda i, j: (blk0 + i * nt + j, 0))
        sem = ("parallel", "parallel")
    else:
        assert t == 1 and ROW_TILES == 1 and b % SUBLANE == 0 and tiles_row0 % SUBLANE == 0
        blk0 = tiles_row0 // SUBLANE
        tiles_spec = pl.BlockSpec((SUBLANE, ROW_W), lambda i, j: (blk0 + i // SUBLANE, 0))
        sem = ("arbitrary", "arbitrary")
    aliases = {}
    if tiles_into is not None:
        in_specs.append(pl.BlockSpec(memory_space=pl.ANY))
        args.append(tiles_into)
        aliases = {len(args) - 1: 1}
    return pl.pallas_call(
        _norm_tiles_kernel,
        out_shape=[jax.ShapeDtypeStruct((b, t, d), F32),
                   jax.ShapeDtypeStruct((tiles_rows * ROW_TILES, ROW_W), F32)],
        grid=(b, nt),
        in_specs=in_specs,
        out_specs=[row_spec, tiles_spec],
        input_output_aliases=aliases,
        compiler_params=_cparams(sem),
        name="norm_tiles",
    )(*args)


IN_TN = 2 * KV_W
N_BF16_TILES = A_GATE // IN_TN
KV_TILE0 = A_KV // IN_TN
KV_ROWS = 2 * N_KV


def _in_proj_kernel(a_ref, w_ref, o_ref, ob_ref, kv_ref):
    j = pl.program_id(1)
    acc = _dot(a_ref[...], w_ref[...])
    o_ref[...] = acc

    @pl.when(j < N_BF16_TILES)
    def _():
        ob_ref[...] = acc.astype(BF16)

    @pl.when((j >= KV_TILE0) & (j < N_BF16_TILES))
    def _():
        tm = acc.shape[0]
        for c4 in range(KV_ROWS):
            kv_ref[pl.ds(c4, tm, stride=KV_ROWS), :] = acc[:, c4 * HD_A:(c4 + 1) * HD_A]


def _row_tile(m, cap):
    return m if m <= cap else cap


def in_proj(a, w):
    m, k = a.shape
    tm = _row_tile(m, 2048)
    tn = IN_TN
    assert m % tm == 0
    return pl.pallas_call(
        _in_proj_kernel,
        out_shape=[jax.ShapeDtypeStruct((m, IN_AL), F32), jax.ShapeDtypeStruct((m, A_GATE), BF16),
                   jax.ShapeDtypeStruct((3, m * KV_ROWS, HD_A), F32)],
        grid=(m // tm, IN_AL // tn),
        in_specs=[pl.BlockSpec((tm, k), lambda i, j: (i, 0)),
                  pl.BlockSpec((k, tn), lambda i, j: (0, j))],
        out_specs=[pl.BlockSpec((tm, tn), lambda i, j: (i, j)),
                   pl.BlockSpec((tm, tn), lambda i, j: (i, jnp.minimum(j, N_BF16_TILES - 1))),
                   pl.BlockSpec((None, tm * KV_ROWS, HD_A),
                                lambda i, j: (jnp.clip(j - KV_TILE0, 0, 2), i, 0))],
        compiler_params=_cparams(("parallel", "arbitrary")),
        name="proj_in",
    )(a, w)


def _branch_kernel(oa_ref, ob_ref, wa_ref, wb_ref, ga_ref, gb_ref, o_ref):
    ya = _dot(oa_ref[...], wa_ref[...])
    yb = _dot(ob_ref[...], wb_ref[...])
    o_ref[...] = (_sigmoid(ga_ref[...]) * ya + _sigmoid(gb_ref[...]) * yb).astype(o_ref.dtype)


def branch_merge(o_a, o_b, w_branch, l, pp):
    m = o_a.shape[0]
    tm = _row_tile(m, 1024)
    tn = 512
    gcol = A_MERGE // tn
    return pl.pallas_call(
        _branch_kernel,
        out_shape=jax.ShapeDtypeStruct((m, D_MODEL), BF16),
        grid=(m // tm, D_MODEL // tn),
        in_specs=[pl.BlockSpec((tm, D_A), lambda i, j: (i, 0)),
                  pl.BlockSpec((tm, D_B), lambda i, j: (i, 0)),
                  pl.BlockSpec((None, D_A, tn), lambda i, j: (l, 0, j)),
                  pl.BlockSpec((None, D_B, tn), lambda i, j: (l, D_A // D_B, j)),
                  pl.BlockSpec((tm, tn), lambda i, j: (i, gcol + j)),
                  pl.BlockSpec((tm, tn), lambda i, j: (i, gcol + D_MODEL // tn + j))],
        out_specs=pl.BlockSpec((tm, tn), lambda i, j: (i, j)),
        compiler_params=_cparams(("parallel", "parallel")),
        name="branch_merge",
    )(o_a, o_b, w_branch, w_branch, pp, pp)


def _resid_kernel(m_ref, w_ref, x_ref, g_ref, o_ref):
    o_ref[...] = x_ref[...] + g_ref[...] * _dot(m_ref[...], w_ref[...])


def out_proj_residual(merged, w_out_l, x, gate):
    b, t, d = x.shape
    tm = _row_tile(t, 2048)
    tn = 512
    nt = t // tm
    return pl.pallas_call(
        _resid_kernel,
        out_shape=jax.ShapeDtypeStruct((b, t, d), F32),
        grid=(b, nt, d // tn),
        in_specs=[pl.BlockSpec((None, tm, d), lambda i, r, j: (i, r, 0)),
                  pl.BlockSpec((d, tn), lambda i, r, j: (0, j)),
                  pl.BlockSpec((None, tm, tn), lambda i, r, j: (i, r, j)),
                  pl.BlockSpec((None, 1, tn), lambda i, r, j: (i, 0, j))],
        out_specs=pl.BlockSpec((None, tm, tn), lambda i, r, j: (i, r, j)),
        compiler_params=_cparams(("parallel", "parallel", "parallel")),
        name="out_proj",
    )(merged, w_out_l, x, gate.reshape(b, 1, d))


T_PER_STEP = 8


def _gelu_tanh(x):
    return 0.5 * x * (1.0 + jnp.tanh(0.7978845608028654 * (x + 0.044715 * x * x * x)))


def _compress_kernel(x_ref, w1_ref, b1_ref, w2_ref, o_ref, acc_ref):
    tc = pl.program_id(3)
    nblk = o_ref.shape[0]

    @pl.when(tc == 0)
    def _():
        acc_ref[...] = jnp.zeros_like(acc_ref)

    acc = acc_ref[...]
    for tl in range(T_PER_STEP):
        t = tc * T_PER_STEP + tl
        xt = x_ref[pl.ds(t, nblk, stride=BLK), :]
        acc = acc + _dot(xt, w1_ref[tl * HD_A:(tl + 1) * HD_A, :])
    acc_ref[...] = acc

    @pl.when(tc == pl.num_programs(3) - 1)
    def _():
        h = _gelu_tanh(acc + b1_ref[...])
        o_ref[...] = _dot(h, w2_ref[...])


def compress(rows, col0, n_slabs, w1b, b1, w2):
    r = rows.shape[0] // n_slabs
    nblk = r // BLK
    cb = col0 // HD_A
    return pl.pallas_call(
        _compress_kernel,
        out_shape=jax.ShapeDtypeStruct((2, N_KV, n_slabs * nblk, HD_A), F32),
        grid=(2, N_KV, n_slabs, BLK // T_PER_STEP),
        in_specs=[pl.BlockSpec((r, HD_A), lambda kv, g, s, tc: (s, cb + kv * N_KV + g)),
                  pl.BlockSpec((None, T_PER_STEP * HD_A, CMP_HID), lambda kv, g, s, tc: (kv, tc, 0)),
                  pl.BlockSpec((None, 1, CMP_HID), lambda kv, g, s, tc: (kv, 0, 0)),
                  pl.BlockSpec((None, CMP_HID, HD_A), lambda kv, g, s, tc: (kv, 0, 0))],
        out_specs=pl.BlockSpec((None, None, nblk, HD_A), lambda kv, g, s, tc: (kv, g, s, 0)),
        scratch_shapes=[pltpu.VMEM((nblk, CMP_HID), F32)],
        compiler_params=_cparams(("parallel", "parallel", "parallel", "arbitrary")),
        name="compress",
    )(rows, w1b, b1.reshape(2, 1, CMP_HID), w2)


QB = 128
KC = 512
HEAD_GROUP = 4


def _softmax_cols(s, valid):
    sm = jnp.where(valid, s, NEG)
    m = jnp.max(sm, axis=0, keepdims=True)
    e = jnp.where(valid, jnp.exp(sm - m), 0.0)
    return e, jnp.sum(e, axis=0, keepdims=True)


def _safe(den):
    return jnp.where(den > 0, den, 1.0)


def _select_mask_t(imp, jblk, cur):
    nb = imp.shape[0]
    score = jnp.where(jblk < cur, imp, jnp.where(jblk == cur, GQA_R + 1.0, -1.0))
    rank = jnp.zeros(imp.shape, I32)
    for i in range(nb):
        row = score[i:i + 1, :]
        beats = (row > score) | ((row == score) & (jblk > i))
        rank = rank + beats.astype(I32)
    return (rank < N_SEL) & (jblk <= cur)


def _nsa_prompt_kernel(slopes_ref, q_ref, kc_ref, vc_ref, ks_ref, vs_ref, kw_ref, vw_ref, gate_ref,
                       o_ref, bias_ref, acc_ref, win_ref):
    g = pl.program_id(1)
    i = pl.program_id(2)
    t_len = ks_ref.shape[0]
    nb = t_len // BLK
    q0 = i * QB
    span = WINDOW + QB

    jblk = lax.broadcasted_iota(I32, (nb, QB), 0)
    qpos_l = q0 + lax.broadcasted_iota(I32, (nb, QB), 1)
    dist_c = qpos_l - ((jblk + 1) * BLK - 1)
    valid_c = dist_c >= 0
    dist_cf = dist_c.astype(F32)
    kc = kc_ref[...]
    vc = vc_ref[...]

    qs = [q_ref[:, r * HD_A:(r + 1) * HD_A] for r in range(GQA_R)]
    slopes = [slopes_ref[g * GQA_R + r] for r in range(GQA_R)]

    start = pl.multiple_of(jnp.maximum(i - WINDOW // QB, 0) * QB, QB)
    kw = kw_ref[pl.ds(start, span), :]
    vw = vw_ref[pl.ds(start, span), :]
    dist_w = (q0 + lax.broadcasted_iota(I32, (QB, span), 0)) - (start + lax.broadcasted_iota(I32, (QB, span), 1))
    bias_w = jnp.where((dist_w >= 0) & (dist_w < WINDOW), 0.0, NEG)
    kpos_w = (start + lax.broadcasted_iota(I32, (1, span), 1)).astype(F32)
    sw = [_dot_nt(qs[r], kw) * SCALE_A + (bias_w + slopes[r] * kpos_w) for r in range(GQA_R)]
    ew = [jnp.exp(sw[r] - jnp.max(sw[r], axis=-1, keepdims=True)) for r in range(GQA_R)]
    den_w = [jnp.sum(ew[r], axis=-1, keepdims=True) for r in range(GQA_R)]
    pv_w = [_dot(ew[r], vw) for r in range(GQA_R)]
    for r in range(GQA_R):
        win_ref[r] = pv_w[r] / den_w[r]

    imp = jnp.zeros((nb, QB), F32)
    o_cmp = []
    for r in range(GQA_R):
        s = _dot_nt(kc, qs[r]) * SCALE_A - slopes[r] * dist_cf
        e, den = _softmax_cols(s, valid_c)
        p = e / _safe(den)
        imp = imp + p
        o_cmp.append(_dot_tn(p, vc))

    sel = _select_mask_t(imp, jblk, qpos_l // BLK)
    expand = (lax.broadcasted_iota(I32, (nb, t_len), 1) // BLK
              == lax.broadcasted_iota(I32, (nb, t_len), 0))
    key_sel = _dot_tn(sel.astype(F32), expand.astype(F32))
    row_q = q0 + lax.broadcasted_iota(I32, (QB, KC), 0)
    col_k = lax.broadcasted_iota(I32, (QB, KC), 1)
    for cc in range(t_len // KC):
        @pl.when(cc * KC < q0 + QB)
        def _(cc=cc):
            ok = (key_sel[:, cc * KC:(cc + 1) * KC] > 0.5) & (row_q >= cc * KC + col_k)
            bias_ref[cc] = jnp.where(ok, 0.0, NEG)

    n_chunks = (q0 + QB + KC - 1) // KC
    kpos0 = lax.broadcasted_iota(I32, (1, KC), 1).astype(F32)
    acc_ref[...] = jnp.zeros_like(acc_ref)

    def body(c, carry):
        ms, ls = carry
        k0 = pl.multiple_of(c * KC, KC)
        kk = ks_ref[pl.ds(k0, KC), :]
        vv = vs_ref[pl.ds(k0, KC), :]
        bias = bias_ref[c]
        kpos = kpos0 + k0.astype(F32)
        m_new, l_new = [None] * GQA_R, [None] * GQA_R
        for r0 in range(0, GQA_R, HEAD_GROUP):
            grp = range(r0, r0 + HEAD_GROUP)
            ss = {r: _dot_nt(qs[r], kk) * SCALE_A + (bias + slopes[r] * kpos) for r in grp}
            for r in grp:
                m_new[r] = jnp.maximum(ms[r], jnp.max(ss[r], axis=-1, keepdims=True))
            alpha = {r: jnp.exp(ms[r] - m_new[r]) for r in grp}
            es = {r: jnp.exp(ss[r] - m_new[r]) for r in grp}
            for r in grp:
                l_new[r] = alpha[r] * ls[r] + jnp.sum(es[r], axis=-1, keepdims=True)
            pv = {r: _dot(es[r], vv) for r in grp}
            for r in grp:
                acc_ref[r] = alpha[r] * acc_ref[r] + pv[r]
        return tuple(m_new), tuple(l_new)

    init = (tuple(jnp.full((QB, 1), NEG, F32) for _ in range(GQA_R)),
            tuple(jnp.zeros((QB, 1), F32) for _ in range(GQA_R)))
    _, l_sel = lax.fori_loop(0, n_chunks, body, init)

    gates = _sigmoid(gate_ref[...])
    for r in range(GQA_R):
        o = (gates[:, 3 * r:3 * r + 1] * o_cmp[r] + gates[:, 3 * r + 1:3 * r + 2] * (acc_ref[r] / l_sel[r])
             + gates[:, 3 * r + 2:3 * r + 3] * win_ref[r])
        o_ref[:, r * HD_A:(r + 1) * HD_A] = o.astype(o_ref.dtype)


def nsa_prompt(slopes, ppb, pp, cmp_kv, b, t):
    nq = t // QB
    nb = t // BLK
    kvb = A_KV // HD_A

    def kv_spec(branch, kv):
        return pl.BlockSpec((t, HD_A), lambda bi, g, i, s: (bi, kvb + branch * 4 + kv * 2 + g))

    grid_spec = pltpu.PrefetchScalarGridSpec(
        num_scalar_prefetch=1,
        grid=(b, N_KV, nq),
        in_specs=[pl.BlockSpec((QB, GQA_R * HD_A), lambda bi, g, i, s: (bi * nq + i, g)),
                  pl.BlockSpec((None, None, nb, HD_A), lambda bi, g, i, s: (0, g, bi, 0)),
                  pl.BlockSpec((None, None, nb, HD_A), lambda bi, g, i, s: (1, g, bi, 0)),
                  kv_spec(1, 0), kv_spec(1, 1), kv_spec(2, 0), kv_spec(2, 1),
                  pl.BlockSpec((QB, LANE), lambda bi, g, i, s: (bi * nq + i, A_GATE // LANE + g))],
        out_specs=pl.BlockSpec((QB, GQA_R * HD_A), lambda bi, g, i, s: (bi * nq + i, g)),
        scratch_shapes=[pltpu.VMEM((t // KC, QB, KC), F32), pltpu.VMEM((GQA_R, QB, HD_A), F32),
                        pltpu.VMEM((GQA_R, QB, HD_A), F32)],
    )
    return pl.pallas_call(
        _nsa_prompt_kernel,
        out_shape=jax.ShapeDtypeStruct((b * t, D_A), BF16),
        grid_spec=grid_spec,
        compiler_params=_cparams(("parallel", "parallel", "arbitrary")),
        name="nsa_prompt",
    )(slopes, ppb, cmp_kv, cmp_kv, ppb, ppb, ppb, ppb, pp)


PAGE_ROWS = PAGE * KV_ROWS
SLAB_PAGES = 64


def _compress_paged_kernel(pt_ref, pool_hbm, w1_ref, b1_ref, w2_ref, o_ref, slab_ref, acc_ref, sem, *, page0):
    s = pl.program_id(0)
    tc = pl.program_id(1)
    n_slabs = pl.num_programs(0)
    nblk = SLAB_PAGES * (PAGE // BLK)

    def page_copy(slab, p, slot):
        phys = pt_ref[slab * SLAB_PAGES + p]
        return pltpu.make_async_copy(pool_hbm.at[pl.ds((page0 + phys) * PAGE_ROWS, PAGE_ROWS), :],
                                     slab_ref.at[slot, pl.ds(p * PAGE_ROWS, PAGE_ROWS), :], sem.at[slot])

    def start_slab(slab, slot):
        lax.fori_loop(0, SLAB_PAGES, lambda p, c: (page_copy(slab, p, slot).start(), c)[1], 0)

    def wait_slab(slab, slot):
        lax.fori_loop(0, SLAB_PAGES, lambda p, c: (page_copy(slab, p, slot).wait(), c)[1], 0)

    slot = s % 2

    @pl.when(tc == 0)
    def _():
        @pl.when(s == 0)
        def _():
            start_slab(0, 0)

        wait_slab(s, slot)

        @pl.when(s + 1 < n_slabs)
        def _():
            start_slab(s + 1, 1 - slot)

        acc_ref[...] = jnp.zeros_like(acc_ref)

    for tl in range(T_PER_STEP):
        t = tc * T_PER_STEP + tl
        for kv in range(2):
            xt = jnp.concatenate(
                [slab_ref[slot, pl.ds(t * KV_ROWS + kv * N_KV + g, nblk, stride=BLK * KV_ROWS), :]
                 for g in range(N_KV)], axis=0)
            acc_ref[kv] += _dot(xt, w1_ref[kv, tl * HD_A:(tl + 1) * HD_A, :])

    @pl.when(tc == pl.num_programs(1) - 1)
    def _():
        for kv in range(2):
            o = _dot(_gelu_tanh(acc_ref[kv] + b1_ref[kv]), w2_ref[kv])
            for g in range(N_KV):
                o_ref[kv, g] = o[g * nblk:(g + 1) * nblk]


def compress_paged(page_table, pool2d, page0, w1b, b1, w2):
    bd, n_pages = page_table.shape
    assert n_pages % SLAB_PAGES == 0
    n_slabs = bd * n_pages // SLAB_PAGES
    nblk = SLAB_PAGES * (PAGE // BLK)
    grid_spec = pltpu.PrefetchScalarGridSpec(
        num_scalar_prefetch=1,
        grid=(n_slabs, BLK // T_PER_STEP),
        in_specs=[pl.BlockSpec(memory_space=pl.ANY),
                  pl.BlockSpec((2, T_PER_STEP * HD_A, CMP_HID), lambda s, tc, pt: (0, tc, 0)),
                  pl.BlockSpec((2, 1, CMP_HID), lambda s, tc, pt: (0, 0, 0)),
                  pl.BlockSpec((2, CMP_HID, HD_A), lambda s, tc, pt: (0, 0, 0))],
        out_specs=pl.BlockSpec((2, N_KV, nblk, HD_A), lambda s, tc, pt: (0, 0, s, 0)),
        scratch_shapes=[pltpu.VMEM((2, SLAB_PAGES * PAGE_ROWS, HD_A), F32),
                        pltpu.VMEM((2, N_KV * nblk, CMP_HID), F32),
                        pltpu.SemaphoreType.DMA((2,))],
    )
    return pl.pallas_call(
        functools.partial(_compress_paged_kernel, page0=page0),
        out_shape=jax.ShapeDtypeStruct((2, N_KV, n_slabs * nblk, HD_A), F32),
        grid_spec=grid_spec,
        compiler_params=_cparams(("arbitrary", "arbitrary")),
        name="compress_paged",
    )(page_table.reshape(-1), pool2d, w1b, b1.reshape(2, 1, CMP_HID), w2)


SEL_PAD = 128


def _heads_on_sublanes(q_ref):
    q = q_ref[...].astype(F32)
    row = lax.broadcasted_iota(I32, (SUBLANE, HD_A), 0)
    q8 = jnp.zeros((SUBLANE, HD_A), F32)
    for r in range(GQA_R):
        q8 = jnp.where(row == r, jnp.broadcast_to(q[:, r * HD_A:(r + 1) * HD_A], (SUBLANE, HD_A)), q8)
    return q8


def _slopes_on_sublanes(slopes_ref, g):
    row = lax.broadcasted_iota(I32, (SUBLANE, 1), 0)
    slope = jnp.zeros((SUBLANE, 1), F32)
    for r in range(GQA_R):
        slope = jnp.where(row == r, slopes_ref[g * GQA_R + r], slope)
    return slope


def _softmax_rows(s, valid):
    sm = jnp.where(valid, s, NEG)
    m = jnp.max(sm, axis=-1, keepdims=True)
    e = jnp.where(valid, jnp.exp(sm - m), 0.0)
    return e, jnp.sum(e, axis=-1, keepdims=True)


def _nsa_sample_cmp_kernel(slopes_ref, q_ref, kc_ref, vc_ref, o_ref, idx_ref, *, past, nb_real):
    g = pl.program_id(1)
    nbp = kc_ref.shape[0]
    kc = kc_ref[...]
    vc = vc_ref[...]
    q8 = _heads_on_sublanes(q_ref)
    slope = _slopes_on_sublanes(slopes_ref, g)
    head = lax.broadcasted_iota(I32, (SUBLANE, 1), 0) < GQA_R
    jrow = lax.broadcasted_iota(I32, (1, nbp), 1)
    dist = past - ((jrow + 1) * BLK - 1)
    valid = (dist >= 0) & (jrow < nb_real)
    s = _dot_nt(q8, kc) * SCALE_A - slope * dist.astype(F32)
    e, den = _softmax_rows(s, valid)
    p = e / _safe(den)
    o = _dot(p, vc)
    for r in range(GQA_R):
        o_ref[:, r * HD_A:(r + 1) * HD_A] = o[r:r + 1, :]

    imp = jnp.sum(jnp.where(head, p, 0.0), axis=0, keepdims=True)
    cur = past // BLK
    score_r = jnp.where(jrow < cur, imp, jnp.where(jrow == cur, GQA_R + 1.0, -1.0))
    score_r = jnp.where(jrow < nb_real, score_r, -2.0)
    ii = lax.broadcasted_iota(I32, (nbp, nbp), 0)
    jj = lax.broadcasted_iota(I32, (nbp, nbp), 1)
    sc_b = jnp.broadcast_to(score_r, (nbp, nbp))
    score_c = jnp.sum(jnp.where(ii == jj, sc_b, 0.0), axis=1, keepdims=True)
    beats = (score_c > sc_b) | ((score_c == sc_b) & (ii < jj))
    rank_r = jnp.sum(beats.astype(F32), axis=0, keepdims=True)
    nn = lax.broadcasted_iota(I32, (SEL_PAD, nbp), 0).astype(F32)
    jn = lax.broadcasted_iota(I32, (SEL_PAD, nbp), 1).astype(F32)
    hit = jnp.broadcast_to(rank_r, (SEL_PAD, nbp)) == nn
    idx_c = jnp.sum(jnp.where(hit, jn, 0.0), axis=1, keepdims=True)
    idx_ref[...] = jnp.broadcast_to(idx_c, (SEL_PAD, LANE)).astype(I32)


def nsa_sample_cmp(slopes, ppb3, cmp_kv_s, past, nb_real):
    bd = ppb3.shape[0]
    nbp = cmp_kv_s.shape[3]
    grid_spec = pltpu.PrefetchScalarGridSpec(
        num_scalar_prefetch=1,
        grid=(bd, N_KV),
        in_specs=[pl.BlockSpec((None, 1, GQA_R * HD_A), lambda bi, g, s: (bi, 0, g)),
                  pl.BlockSpec((None, None, None, nbp, HD_A), lambda bi, g, s: (0, g, bi, 0, 0)),
                  pl.BlockSpec((None, None, None, nbp, HD_A), lambda bi, g, s: (1, g, bi, 0, 0))],
        out_specs=[pl.BlockSpec((None, 1, GQA_R * HD_A), lambda bi, g, s: (bi, 0, g)),
                   pl.BlockSpec((None, None, SEL_PAD, LANE), lambda bi, g, s: (bi, g, 0, 0))],
    )
    return pl.pallas_call(
        functools.partial(_nsa_sample_cmp_kernel, past=past, nb_real=nb_real),
        out_shape=[jax.ShapeDtypeStruct((bd, 1, D_A), F32),
                   jax.ShapeDtypeStruct((bd, N_KV, SEL_PAD, LANE), I32)],
        grid_spec=grid_spec,
        compiler_params=_cparams(("parallel", "parallel")),
        name="nsa_sample_cmp",
    )(slopes, ppb3, cmp_kv_s, cmp_kv_s)


W_PAD = LANE


def _nsa_sample_sel_kernel(idx_ref, pt_ref, slopes_ref, q_ref, pool_hbm, kn_ref, vn_ref,
                           wbuf_ref, kwn_ref, vwn_ref, gate_ref, oc_ref, o_ref,
                           blk_ref, ksel_ref, vsel_ref, kwin_ref, vwin_ref, sem, *, past, page0, n_pages):
    bi = pl.program_id(0)
    g = pl.program_id(1)
    nb_past = past // BLK
    bpp = PAGE // BLK
    blk_rows = BLK * KV_ROWS
    base = (bi * N_KV + g) * N_SEL

    def block_copy(n):
        ip = jnp.clip(idx_ref[base + n], 0, nb_past - 1)
        phys = pt_ref[bi * n_pages + ip // bpp]
        row0 = ((page0 + phys) * bpp + ip % bpp) * blk_rows
        return pltpu.make_async_copy(pool_hbm.at[pl.ds(row0, blk_rows), :], blk_ref.at[n], sem)

    for n in range(N_SEL):
        block_copy(n).start()
    for n in range(N_SEL):
        block_copy(n).wait()
    first = lax.broadcasted_iota(I32, (BLK, HD_A), 0) == 0
    k_new = jnp.where(first, kn_ref[...], 0.0)
    v_new = jnp.where(first, vn_ref[...], 0.0)
    for n in range(N_SEL):
        is_past = idx_ref[base + n] < nb_past
        blk = blk_ref.at[n]
        ksel_ref[n * BLK:(n + 1) * BLK, :] = jnp.where(is_past, blk[pl.ds(g, BLK, stride=KV_ROWS), :], k_new)
        vsel_ref[n * BLK:(n + 1) * BLK, :] = jnp.where(is_past, blk[pl.ds(N_KV + g, BLK, stride=KV_ROWS), :], v_new)

    q8 = _heads_on_sublanes(q_ref)
    slope = _slopes_on_sublanes(slopes_ref, g)
    cur = past // BLK
    lane = lax.broadcasted_iota(I32, (1, N_SEL * BLK), 1)
    slot = lane // BLK
    idx_row = jnp.zeros((1, N_SEL * BLK), I32)
    for m in range(N_SEL):
        idx_row = jnp.where(slot == m, idx_ref[base + m], idx_row)
    dist = past - (idx_row * BLK + lane % BLK)
    valid = (idx_row <= cur) & (dist >= 0)
    s = _dot_nt(q8, ksel_ref[...]) * SCALE_A - slope * dist.astype(F32)
    e, den = _softmax_rows(s, valid)
    o_sel = _dot(e / _safe(den), vsel_ref[...])
    wb = wbuf_ref.shape[0] // KV_ROWS
    kwin_ref[0:wb, :] = wbuf_ref[pl.ds(g, wb, stride=KV_ROWS), :]
    vwin_ref[0:wb, :] = wbuf_ref[pl.ds(N_KV + g, wb, stride=KV_ROWS), :]
    first_w = lax.broadcasted_iota(I32, (W_PAD, HD_A), 0) == 0
    kwin_ref[wb:wb + W_PAD, :] = jnp.where(first_w, kwn_ref[...], 0.0)
    vwin_ref[wb:wb + W_PAD, :] = jnp.where(first_w, vwn_ref[...], 0.0)
    lane_w = lax.broadcasted_iota(I32, (1, wb + W_PAD), 1)
    dist_w = wb - lane_w
    valid_w = (dist_w >= 0) & (dist_w < WINDOW)
    s = _dot_nt(q8, kwin_ref[...]) * SCALE_A - slope * dist_w.astype(F32)
    e, den = _softmax_rows(s, valid_w)
    o_win = _dot(e / _safe(den), vwin_ref[...])
    gates = _sigmoid(gate_ref[...])
    for r in range(GQA_R):
        o = (gates[:, 3 * r:3 * r + 1] * oc_ref[:, r * HD_A:(r + 1) * HD_A]
             + gates[:, 3 * r + 1:3 * r + 2] * o_sel[r:r + 1, :]
             + gates[:, 3 * r + 2:3 * r + 3] * o_win[r:r + 1, :])
        o_ref[:, r * HD_A:(r + 1) * HD_A] = o.astype(o_ref.dtype)


def nsa_sample_sel(idx_flat, page_table, slopes, ppb3, pp3, pool2d, page0, win2d, seq0, wb, o_cmp, past):
    bd = ppb3.shape[0]
    n_pages = page_table.shape[1]
    kvb = A_KV // HD_A

    def new_spec(branch, kv):
        return pl.BlockSpec((None, 1, HD_A), lambda bi, g, idx, pt, s: (bi, 0, kvb + branch * 4 + kv * 2 + g))

    grid_spec = pltpu.PrefetchScalarGridSpec(
        num_scalar_prefetch=3,
        grid=(bd, N_KV),
        in_specs=[pl.BlockSpec((None, 1, GQA_R * HD_A), lambda bi, g, idx, pt, s: (bi, 0, g)),
                  pl.BlockSpec(memory_space=pl.ANY), new_spec(1, 0), new_spec(1, 1),
                  pl.BlockSpec((wb * KV_ROWS, HD_A), lambda bi, g, idx, pt, s: (seq0 + bi, 0)),
                  new_spec(2, 0), new_spec(2, 1),
                  pl.BlockSpec((None, 1, LANE), lambda bi, g, idx, pt, s: (bi, 0, A_GATE // LANE + g)),
                  pl.BlockSpec((None, 1, GQA_R * HD_A), lambda bi, g, idx, pt, s: (bi, 0, g))],
        out_specs=pl.BlockSpec((None, 1, GQA_R * HD_A), lambda bi, g, idx, pt, s: (bi, 0, g)),
        scratch_shapes=[pltpu.VMEM((N_SEL, BLK * KV_ROWS, HD_A), F32),
                        pltpu.VMEM((N_SEL * BLK, HD_A), F32), pltpu.VMEM((N_SEL * BLK, HD_A), F32),
                        pltpu.VMEM((wb + W_PAD, HD_A), F32), pltpu.VMEM((wb + W_PAD, HD_A), F32),
                        pltpu.SemaphoreType.DMA(())],
    )
    return pl.pallas_call(
        functools.partial(_nsa_sample_sel_kernel, past=past, page0=page0, n_pages=n_pages),
        out_shape=jax.ShapeDtypeStruct((bd, 1, D_A), BF16),
        grid_spec=grid_spec,
        compiler_params=_cparams(("arbitrary", "arbitrary")),
        name="nsa_sample_sel",
    )(idx_flat, page_table.reshape(-1), slopes, ppb3, pool2d, pp3, pp3, win2d, pp3, pp3, pp3, o_cmp)


def _cumsum_rows(x):
    c = x.shape[0]
    row = lax.broadcasted_iota(I32, x.shape, 0)
    sh = 1
    while sh < c:
        x = x + jnp.where(row >= sh, pltpu.roll(x, sh, 0), 0.0)
        sh *= 2
    return x


def _split2(x):
    hi = x.astype(BF16)
    return hi, (x - hi.astype(F32)).astype(BF16)


def _dot2(a, b, dims=(((1,), (0,)), ((), ()))):
    ah, al = _split2(a)
    bh, bl = _split2(b)
    d = lambda x, y: lax.dot_general(x, y, dims, preferred_element_type=F32)
    return d(ah, bh) + (d(ah, bl) + d(al, bh))


_dot_t = _dot2
NT = (((1,), (1,)), ((), ()))
TN = (((0,), (0,)), ((), ()))
CHUNK_LB = 8
STATE_GROUP = 4


def _rwkv_chunk_kernel(pr_ref, pk_ref, pv_ref, pl_ref, qr_ref, qk_ref, qv_ref, ql_ref,
                       sr_ref, sk_ref, sv_ref, sl_ref,
                       mur_ref, muk_ref, muv_ref, mul_ref, w0_ref, w2_ref, a0_ref, a2_ref, g2_ref,
                       kkp_ref, kap_ref, rkp_ref, gnb_ref,
                       y0_ref, rt_ref, bonus_ref, gate_ref, mm_ref, g0_ref, *, n_valid):
    ci = pl.program_id(2)
    c = pr_ref.shape[0]
    row = lax.broadcasted_iota(I32, (c, 1), 0)
    first_chunk = ci == 0

    def mix(p_ref, q_ref, s_ref, mu_ref):
        p = p_ref[...]
        prev = jnp.where(first_chunk, s_ref[...], q_ref[SUBLANE - 1:SUBLANE, :])
        shifted = jnp.where(row == 0, prev, pltpu.roll(p, 1, 0))
        return p + mu_ref[...] * (shifted - p)

    xr = mix(pr_ref, qr_ref, sr_ref, mur_ref)
    xk = mix(pk_ref, qk_ref, sk_ref, muk_ref)
    xv = mix(pv_ref, qv_ref, sv_ref, muv_ref)
    xl = mix(pl_ref, ql_ref, sl_ref, mul_ref)

    dw = xl[:, 0:LORA_W]
    da = xl[:, LORA_W:LORA_W + LORA_A]
    dg = xl[:, LORA_W + LORA_A:LORA_W + LORA_A + LORA_G]
    wlog = -_softplus(-(w0_ref[...] + _dot(jnp.tanh(dw), w2_ref[...]))) - 0.5
    logdec = -jnp.exp(wlog)
    a = _sigmoid(a0_ref[...] + _dot(da, a2_ref[...]))
    gate_ref[...] = _dot(_sigmoid(dg), g2_ref[...])
    kkv = xk * kkp_ref[...]
    kmod = xk * (1.0 + (a - 1.0) * kap_ref[...])
    if n_valid < c:
        live = row < n_valid
        logdec = jnp.where(live, logdec, 0.0)
        kmod = jnp.where(live, kmod, 0.0)
        a = jnp.where(live, a, 0.0)
        xv = jnp.where(live, xv, 0.0)
    cum = _cumsum_rows(logdec)
    cum_ex = cum - logdec
    cum_end = cum[c - 1:c, :]

    ti = lax.broadcasted_iota(I32, (c, c), 0)
    si = lax.broadcasted_iota(I32, (c, c), 1)
    lower_strict = ti > si
    lower_incl = ti >= si
    eye_c = (ti == si).astype(F32)
    eye_k = (lax.broadcasted_iota(I32, (HD_B, HD_B), 0) == lax.broadcasted_iota(I32, (HD_B, HD_B), 1)).astype(F32)
    rk_all = rkp_ref[...]
    gnb = gnb_ref[...]

    heads = range(pr_ref.shape[1] // HD_B)
    sls = [slice(h * HD_B, (h + 1) * HD_B) for h in heads]
    r_ = [xr[:, s] for s in sls]
    k_ = [kmod[:, s] for s in sls]
    v_ = [xv[:, s] for s in sls]
    kk_ = [kkv[:, s] for s in sls]
    kk_ = [x / jnp.maximum(jnp.sqrt(jnp.sum(x * x, axis=-1, keepdims=True)), 1e-12) for x in kk_]
    b_ = [kk_[h] * a[:, sls[h]] for h in heads]
    kap_ = [kk_[h] * jnp.exp(cum_ex[:, sls[h]]) for h in heads]
    rt_ = [r_[h] * jnp.exp(cum[:, sls[h]]) for h in heads]
    inv_ = [jnp.exp(-cum[:, sls[h]]) for h in heads]
    paired = 2 * c == LANE and len(heads) % 2 == 0
    kb_first = [paired and h % 2 == 0 for h in heads]
    amat = [_dot2(jnp.concatenate([kap_[h], rt_[h]], axis=0),
                  jnp.concatenate([b_[h] * inv_[h], k_[h] * inv_[h]] if kb_first[h]
                                  else [k_[h] * inv_[h], b_[h] * inv_[h]], axis=0), NT) for h in heads]
    kcol = [slice(c, 2 * c) if kb_first[h] else slice(0, c) for h in heads]
    bcol = [slice(0, c) if kb_first[h] else slice(c, 2 * c) for h in heads]
    a_kk = [jnp.where(lower_strict, amat[h][0:c, kcol[h]], 0.0) for h in heads]
    a_rk = [jnp.where(lower_incl, amat[h][c:2 * c, kcol[h]], 0.0) for h in heads]
    a_rb = [jnp.where(lower_incl, amat[h][c:2 * c, bcol[h]], 0.0) for h in heads]
    av = [_dot2(jnp.concatenate([a_kk[h], a_rk[h]], axis=0), v_[h]) for h in heads]

    def level_mask(rows_i, cols_i, w):
        return (rows_i // (2 * w) == cols_i // (2 * w)) & ((rows_i // w) % 2 == 1) & ((cols_i // w) % 2 == 0)

    if paired:
        tp = lax.broadcasted_iota(I32, (c, LANE), 0)
        lane_p = lax.broadcasted_iota(I32, (c, LANE), 1)
        sp = lane_p % c
        left = lane_p < c

        left_b = left.astype(BF16)
        right_b = 1.0 - left_b

        def blockdiag(x):
            return jnp.concatenate([x * left_b, x * right_b], axis=0)

        def dot3p(x_hi, x_lo, y_hi, y_lo):
            d = lambda a_, b_: lax.dot_general(a_, b_, (((1,), (0,)), ((), ())), preferred_element_type=F32)
            return d(x_hi, y_hi) + (d(x_hi, y_lo) + d(x_lo, y_hi))

        pairs = range(len(heads) // 2)
        l_pair = [jnp.where(tp > sp, jnp.where(left, amat[2 * p][0:c, :], amat[2 * p + 1][0:c, :]), 0.0)
                  for p in pairs]
        l_split = [_split2(m) for m in l_pair]
        tpair = [(tp == sp).astype(F32) - jnp.where(((tp % 2) == 1) & (sp == tp - 1), m, 0.0) for m in l_pair]
        w = 2
        while w < c:
            off_b = level_mask(tp, sp, w).astype(BF16)
            t_split = [_split2(m) for m in tpair]
            ld = [dot3p(l_split[p][0] * off_b, l_split[p][1] * off_b,
                        blockdiag(t_split[p][0]), blockdiag(t_split[p][1])) for p in pairs]
            ld_split = [_split2(m) for m in ld]
            tpair = [tpair[p] - dot3p(t_split[p][0], t_split[p][1],
                                      blockdiag(ld_split[p][0]), blockdiag(ld_split[p][1])) for p in pairs]
            w *= 2
        tinv = [tpair[h // 2][:, (h % 2) * c:(h % 2 + 1) * c] for h in heads]
    else:
        a_kb = [jnp.where(lower_strict, amat[h][0:c, bcol[h]], 0.0) for h in heads]
        tinv = [eye_c - jnp.where(((ti % 2) == 1) & (si == ti - 1), m, 0.0) for m in a_kb]
        w = 2
        while w < c:
            off = level_mask(ti, si, w)
            ld = [_dot_t(jnp.where(off, a_kb[h], 0.0), tinv[h]) for h in heads]
            tinv = [tinv[h] - _dot_t(tinv[h], ld[h]) for h in heads]
            w *= 2

    tx = [_dot2(tinv[h], jnp.concatenate([av[h][0:c], kap_[h]], axis=1)) for h in heads]
    arb_tx = [_dot2(a_rb[h], tx[h]) for h in heads]
    dec_end = [jnp.exp(cum_end[:, sls[h]] - cum[:, sls[h]]) for h in heads]
    k_end = [k_[h] * dec_end[h] for h in heads]
    b_end = [b_[h] * dec_end[h] for h in heads]
    g0 = [_dot2(jnp.concatenate([v_[h], -tx[h][:, 0:HD_B]], axis=0),
                jnp.concatenate([k_end[h], b_end[h]], axis=0), TN) for h in heads]
    ktb = [_dot2(tx[h][:, HD_B:2 * HD_B], b_end[h], TN) for h in heads]
    for h in heads:
        s = sls[h]
        y0_ref[:, s] = av[h][c:2 * c] - arb_tx[h][:, 0:HD_B]
        rt_ref[:, s] = rt_[h] - arb_tx[h][:, HD_B:2 * HD_B]
        bonus_ref[:, s] = jnp.sum(r_[h] * k_[h] * rk_all[:, s], axis=-1, keepdims=True) * v_[h] + gnb[:, s]
        mm_ref[h] = eye_k * jnp.exp(cum_end[:, s]) - ktb[h]
        g0_ref[h] = g0[h]


def _rwkv_state_kernel(y0_ref, rt_ref, bonus_ref, gate_ref, mm_ref, g0_ref, s0_ref, gng_ref,
                       o_ref, sT_ref, st_ref):
    @pl.when(pl.program_id(1) == 0)
    def _():
        st_ref[...] = s0_ref[...]

    gng = gng_ref[...]
    for h0 in range(0, N_HEADS_B, STATE_GROUP):
        heads = range(h0, h0 + STATE_GROUP)
        sls = {h: slice(h * HD_B, (h + 1) * HD_B) for h in heads}
        s0 = {h: st_ref[h] for h in heads}
        s_new = {h: _dot2(s0[h], mm_ref[h]) + g0_ref[h] for h in heads}
        y = {h: y0_ref[:, sls[h]] + _dot2(rt_ref[:, sls[h]], s0[h], NT) for h in heads}
        for h in heads:
            st_ref[h] = s_new[h]
            sT_ref[h] = s_new[h]
        mean = {h: jnp.mean(y[h], axis=-1, keepdims=True) for h in heads}
        dev = {h: y[h] - mean[h] for h in heads}
        var = {h: jnp.mean(jnp.square(dev[h]), axis=-1, keepdims=True) for h in heads}
        for h in heads:
            yn = dev[h] * lax.rsqrt(var[h] + GN_EPS) * gng[:, sls[h]]
            o_ref[:, sls[h]] = ((yn + bonus_ref[:, sls[h]]) * gate_ref[:, sls[h]]).astype(o_ref.dtype)


def rwkv7(pp, shift0, wkv0, prm, bsz, t, chunk, n_valid):
    nc = t // chunk
    lb = CHUNK_LB * LANE
    ngrp = D_B // lb
    hs = lb // HD_B
    assert A_RKV % lb == 0
    rb = A_RKV // lb
    per = D_B // lb
    lo_blk = A_LO // (2 * LANE)
    lo3 = 3 * D_B // (2 * LANE)
    sub = chunk // SUBLANE

    def rows(off):
        return pl.BlockSpec((chunk, lb), lambda b, hg, ci: (b * nc + ci, rb + off * per + hg))

    def prev_rows(off):
        return pl.BlockSpec((SUBLANE, lb),
                            lambda b, hg, ci: (jnp.maximum((b * nc + ci) * sub - 1, 0), rb + off * per + hg))

    def srow(off):
        return pl.BlockSpec((None, 1, lb), lambda b, hg, ci: (b, 0, off * per + hg))

    def prow(off):
        return pl.BlockSpec((1, lb), lambda b, hg, ci: (0, off * per + hg))

    def per_head(shape0):
        return pl.BlockSpec((shape0, lb), lambda b, hg, ci: (0, hg))

    lo_spec = lambda rws, imap: pl.BlockSpec((rws, 2 * LANE), imap)
    in_specs = [rows(0), rows(1), rows(2), lo_spec(chunk, lambda b, hg, ci: (b * nc + ci, lo_blk)),
                prev_rows(0), prev_rows(1), prev_rows(2),
                lo_spec(SUBLANE, lambda b, hg, ci: (jnp.maximum((b * nc + ci) * sub - 1, 0), lo_blk)),
                srow(0), srow(1), srow(2), pl.BlockSpec((None, 1, 2 * LANE), lambda b, hg, ci: (b, 0, lo3)),
                prow(0), prow(1), prow(2), lo_spec(1, lambda b, hg, ci: (0, lo3)),
                per_head(1), per_head(LORA_W), per_head(1), per_head(LORA_A), per_head(LORA_G),
                per_head(1), per_head(1), per_head(1), per_head(1)]
    row_out = pl.BlockSpec((chunk, lb), lambda b, hg, ci: (b * nc + ci, hg))
    mat_out = pl.BlockSpec((None, None, hs, HD_B, HD_B), lambda b, hg, ci: (b, ci, hg, 0, 0))
    n = bsz * t
    y0, rt, bonus, gate, mm, g0 = pl.pallas_call(
        functools.partial(_rwkv_chunk_kernel, n_valid=n_valid),
        out_shape=[jax.ShapeDtypeStruct((n, D_B), F32)] * 4
        + [jax.ShapeDtypeStruct((bsz, nc, N_HEADS_B, HD_B, HD_B), F32)] * 2,
        grid=(bsz, ngrp, nc),
        in_specs=in_specs,
        out_specs=[row_out] * 4 + [mat_out] * 2,
        compiler_params=_cparams(("parallel", "parallel", "parallel")),
        name="rwkv_chunk",
    )(pp, pp, pp, pp, pp, pp, pp, pp, shift0, shift0, shift0, shift0,
      prm["mu"], prm["mu"], prm["mu"], prm["mu"], prm["w0"], prm["w2"], prm["a0"], prm["a2"], prm["g2"],
      prm["kk"], prm["ka"], prm["rk"], prm["gn_b"])

    row_in = pl.BlockSpec((chunk, D_B), lambda b, ci: (b * nc + ci, 0))
    mat_in = pl.BlockSpec((None, None, N_HEADS_B, HD_B, HD_B), lambda b, ci: (b, ci, 0, 0, 0))
    state = pl.BlockSpec((None, N_HEADS_B, HD_B, HD_B), lambda b, ci: (b, 0, 0, 0))
    o_b, s_fin = pl.pallas_call(
        _rwkv_state_kernel,
        out_shape=[jax.ShapeDtypeStruct((n, D_B), BF16),
                   jax.ShapeDtypeStruct((bsz, N_HEADS_B, HD_B, HD_B), F32)],
        grid=(bsz, nc),
        in_specs=[row_in, row_in, row_in, row_in, mat_in, mat_in, state,
                  pl.BlockSpec((1, D_B), lambda b, ci: (0, 0))],
        out_specs=[row_in, state],
        scratch_shapes=[pltpu.VMEM((N_HEADS_B, HD_B, HD_B), F32)],
        compiler_params=_cparams(("parallel", "arbitrary")),
        name="rwkv_state",
    )(y0, rt, bonus, gate, mm, g0, wkv0, prm["gn_g"])
    return o_b, s_fin


def _first_lane(cond, lane):
    return jnp.min(jnp.where(cond, lane, 4 * LANE), axis=-1, keepdims=True)


def _router_kernel(h_ref, w_ref, b_ref, eid_ref, wt_ref):
    logits = _dot3(h_ref[...], w_ref[...]) + b_ref[...]
    lane = lax.broadcasted_iota(I32, logits.shape, 1)
    gmask = lane < N_GROUPS
    lg = jnp.where(gmask, logits, NEG)
    eg = jnp.where(gmask, jnp.exp(lg - jnp.max(lg, axis=-1, keepdims=True)), 0.0)
    gp = eg / jnp.sum(eg, axis=-1, keepdims=True)
    g_w = jnp.max(gp, axis=-1, keepdims=True)
    grp = _first_lane(gmask & (gp == g_w), lane)
    lo = N_GROUPS + grp * E_PER_GROUP
    emask = (lane >= lo) & (lane < lo + E_PER_GROUP)
    le = jnp.where(emask, logits, NEG)
    ee = jnp.where(emask, jnp.exp(le - jnp.max(le, axis=-1, keepdims=True)), 0.0)
    ep = ee / jnp.sum(ee, axis=-1, keepdims=True)
    p1 = jnp.max(jnp.where(emask, ep, -1.0), axis=-1, keepdims=True)
    i1 = _first_lane(emask & (ep == p1), lane)
    rest = emask & (lane != i1)
    p2 = jnp.max(jnp.where(rest, ep, -1.0), axis=-1, keepdims=True)
    i2 = _first_lane(rest & (ep == p2), lane)
    tot = p1 + p2
    eid_ref[...] = jnp.where(lane == 0, i1 - N_GROUPS, jnp.where(lane == 1, i2 - N_GROUPS, 0))
    wt_ref[...] = jnp.where(lane == 0, g_w * p1 / tot, jnp.where(lane == 1, g_w * p2 / tot, 0.0))


def router(h2, wr, br):
    n, d = h2.shape
    tm = _row_tile(n, 512)
    return pl.pallas_call(
        _router_kernel,
        out_shape=[jax.ShapeDtypeStruct((n, LANE), I32), jax.ShapeDtypeStruct((n, LANE), F32)],
        grid=(n // tm,),
        in_specs=[pl.BlockSpec((tm, d), lambda i: (i, 0)),
                  pl.BlockSpec((d, LANE), lambda i: (0, 0)),
                  pl.BlockSpec((1, LANE), lambda i: (0, 0))],
        out_specs=[pl.BlockSpec((tm, LANE), lambda i: (i, 0)), pl.BlockSpec((tm, LANE), lambda i: (i, 0))],
        compiler_params=_cparams(("parallel",)),
        name="router",
    )(h2, wr, br)


def _row_copy(src_hbm, dst_vmem, sem, src_row, dst_row):
    return pltpu.make_async_copy(src_hbm.at[pl.ds(src_row * ROW_TILES, ROW_TILES), :],
                                 dst_vmem.at[pl.ds(dst_row * ROW_TILES, ROW_TILES), :], sem)


def _experts_kernel(blk_e_ref, n_used_ref, tok_ref, h_hbm, w1_ref, w3_ref, w2_ref, y_ref,
                    x_even, x_odd, w1b_ref, w3b_ref, w2b_ref, sem):
    i = pl.program_id(0)
    bm = x_even.shape[0] // ROW_TILES
    n_used = n_used_ref[0]
    bufs = (x_even, x_odd)

    def rows(blk, slot, go):
        for r in range(bm):
            go(_row_copy(h_hbm, bufs[slot], sem.at[slot], tok_ref[blk * bm + r], r))

    def block(slot):
        @pl.when(i == 0)
        def _():
            rows(0, 0, lambda cp: cp.start())

        rows(i, slot, lambda cp: cp.wait())

        @pl.when((i == 0) | (blk_e_ref[i] != blk_e_ref[jnp.maximum(i - 1, 0)]))
        def _():
            w1b_ref[...] = w1_ref[...].astype(BF16)
            w3b_ref[...] = w3_ref[...].astype(BF16)
            w2b_ref[...] = w2_ref[...].astype(BF16)

        nxt = jnp.minimum(i + 1, n_used - 1)
        rows(nxt, 1 - slot, lambda cp: cp.start())
        x = jnp.concatenate([_from_row_tiles(bufs[slot], bm, c).astype(BF16) for c in range(ROW_TILES)], axis=1)
        h1 = _dot(x, w1b_ref[...])
        h3 = _dot(x, w3b_ref[...])
        act = h1 * _sigmoid(h1) * h3
        _to_row_tiles(y_ref, _dot(act, w2b_ref[...]))

        @pl.when(i == n_used - 1)
        def _():
            rows(nxt, 1 - slot, lambda cp: cp.wait())

    for slot in range(2):
        pl.when((i < n_used) & (i % 2 == slot))(functools.partial(block, slot))

    @pl.when(i >= n_used)
    def _():
        y_ref[...] = jnp.zeros_like(y_ref)


def experts(h_all, blk_e, n_used, row_tok, w1, w3, w2, l, bm):
    n_blocks = blk_e.shape[0]
    d = D_MODEL
    grid_spec = pltpu.PrefetchScalarGridSpec(
        num_scalar_prefetch=3,
        grid=(n_blocks,),
        in_specs=[pl.BlockSpec(memory_space=pl.ANY),
                  pl.BlockSpec((None, None, d, D_EXPERT), lambda i, be, nu, tk: (l, be[i], 0, 0)),
                  pl.BlockSpec((None, None, d, D_EXPERT), lambda i, be, nu, tk: (l, be[i], 0, 0)),
                  pl.BlockSpec((None, None, D_EXPERT, d), lambda i, be, nu, tk: (l, be[i], 0, 0))],
        out_specs=pl.BlockSpec((bm * ROW_TILES, ROW_W), lambda i, be, nu, tk: (i, 0)),
        scratch_shapes=[pltpu.VMEM((bm * ROW_TILES, ROW_W), F32), pltpu.VMEM((bm * ROW_TILES, ROW_W), F32),
                        pltpu.VMEM((d, D_EXPERT), BF16), pltpu.VMEM((d, D_EXPERT), BF16),
                        pltpu.VMEM((D_EXPERT, d), BF16), pltpu.SemaphoreType.DMA((2,))],
    )
    return pl.pallas_call(
        _experts_kernel,
        out_shape=jax.ShapeDtypeStruct((n_blocks * bm * ROW_TILES, ROW_W), F32),
        grid_spec=grid_spec,
        compiler_params=_cparams(("arbitrary",)),
        name="experts",
    )(blk_e, n_used, row_tok, h_all, w1, w3, w2)


def _combine_kernel(dest_ref, ys_hbm, wt_ref, x_ref, g_ref, o_ref, buf, sem):
    tm = x_ref.shape[0]
    step = pl.program_id(0) * pl.num_programs(1) + pl.program_id(1)
    n_steps = pl.num_programs(0) * pl.num_programs(1)

    def rows(tile, slot, go):
        for r in range(tm):
            base = (tile * tm + r) * 2
            go(_row_copy(ys_hbm, buf.at[slot, 0], sem.at[slot], dest_ref[base], r))
            go(_row_copy(ys_hbm, buf.at[slot, 1], sem.at[slot], dest_ref[base + 1], r))

    slot = step % 2

    @pl.when(step == 0)
    def _():
        rows(0, 0, lambda cp: cp.start())

    rows(step, slot, lambda cp: cp.wait())
    nxt = jnp.minimum(step + 1, n_steps - 1)
    rows(nxt, 1 - slot, lambda cp: cp.start())
    wt = wt_ref[...]
    w0, w1 = wt[:, 0:1], wt[:, 1:2]
    for c in range(ROW_TILES):
        cs = slice(c * ROW_W, (c + 1) * ROW_W)
        moe = w0 * _from_row_tiles(buf.at[slot, 0], tm, c) + w1 * _from_row_tiles(buf.at[slot, 1], tm, c)
        o_ref[:, cs] = x_ref[:, cs] + g_ref[:, cs] * moe

    @pl.when(step == n_steps - 1)
    def _():
        rows(nxt, 1 - slot, lambda cp: cp.wait())


def combine(ys, dest, wts, x, gate):
    b, t, d = x.shape
    tm = _row_tile(t, 128)
    nt = t // tm
    grid_spec = pltpu.PrefetchScalarGridSpec(
        num_scalar_prefetch=1,
        grid=(b, nt),
        in_specs=[pl.BlockSpec(memory_space=pl.ANY),
                  pl.BlockSpec((None, tm, LANE), lambda i, j, ds: (i, j, 0)),
                  pl.BlockSpec((None, tm, d), lambda i, j, ds: (i, j, 0)),
                  pl.BlockSpec((None, 1, d), lambda i, j, ds: (i, 0, 0))],
        out_specs=pl.BlockSpec((None, tm, d), lambda i, j, ds: (i, j, 0)),
        scratch_shapes=[pltpu.VMEM((2, 2, tm * ROW_TILES, ROW_W), F32), pltpu.SemaphoreType.DMA((2,))],
    )
    return pl.pallas_call(
        _combine_kernel,
        out_shape=jax.ShapeDtypeStruct((b, t, d), F32),
        grid_spec=grid_spec,
        compiler_params=_cparams(("arbitrary", "arbitrary")),
        name="moe_combine",
    )(dest.reshape(-1), ys, wts.reshape(b, t, LANE), x, gate.reshape(b, 1, d))


def _dispatch_tables(eid, bm):
    n = eid.shape[0]
    nk = n * 2
    n_blocks = (nk + N_EXPERTS * (bm - 1) + bm - 1) // bm
    flat_e = eid.reshape(-1)
    onehot = (flat_e[:, None] == jnp.arange(N_EXPERTS, dtype=I32)[None, :]).astype(I32)
    csum = jnp.cumsum(onehot, axis=0)
    counts = csum[-1]
    rank = jnp.sum(onehot * (csum - 1), axis=1)
    padded = (counts + bm - 1) // bm * bm
    pad_end = jnp.cumsum(padded)
    pad_start = pad_end - padded
    dest = (pad_start[flat_e] + rank).astype(I32)
    row_tok = jnp.zeros((n_blocks * bm,), I32).at[dest].set(jnp.arange(nk, dtype=I32) // 2)
    blk_start = jnp.arange(n_blocks, dtype=I32) * bm
    blk_e = jnp.minimum(jnp.sum((pad_end[None, :] <= blk_start[:, None]).astype(I32), axis=1), N_EXPERTS - 1)
    n_used = (pad_end[-1] // bm).astype(I32).reshape(1)
    return blk_e, n_used, row_tok, dest.reshape(n, 2)


def _final_norm_kernel(x_ref, g_ref, o_ref):
    x = x_ref[...]
    o_ref[...] = x * lax.rsqrt(jnp.mean(x * x, axis=-1, keepdims=True) + RMS_EPS) * g_ref[...]


def final_norm(x, g):
    b, t, d = x.shape
    tr = min(t, 256)
    return pl.pallas_call(
        _final_norm_kernel,
        out_shape=jax.ShapeDtypeStruct((b, t, d), F32),
        grid=(b, t // tr),
        in_specs=[pl.BlockSpec((None, tr, d), lambda i, j: (i, j, 0)), pl.BlockSpec((1, d), lambda i, j: (0, 0))],
        out_specs=pl.BlockSpec((None, tr, d), lambda i, j: (i, j, 0)),
        compiler_params=_cparams(("parallel", "parallel")),
        name="final_norm",
    )(x, g.reshape(1, d))


def _align_in_cols(w, axis):
    def take(a, b_):
        return lax.slice_in_dim(w, a, b_, axis=axis)

    def zeros(nz):
        shp = list(w.shape)
        shp[axis] = nz
        return jnp.zeros(shp, w.dtype)

    gate_parts = []
    for g in range(N_KV):
        gate_parts += [take(OFF_GATE_A + g * 3 * GQA_R, OFF_GATE_A + (g + 1) * 3 * GQA_R), zeros(LANE - 3 * GQA_R)]
    parts = [take(0, OFF_GATE_A)] + gate_parts + [take(OFF_RWKV + 3 * D_B, OFF_MERGE), zeros(RW_W - SHIFT_W),
                                                   take(OFF_RWKV, OFF_RWKV + 3 * D_B),
                                                   take(OFF_MERGE, OFF_MERGE + 2 * D_MODEL)]
    return jnp.concatenate(parts, axis=axis)


def _rwkv_cols(p):
    return jnp.concatenate([p[..., A_RKV:A_RKV + 3 * D_B], p[..., A_LO:A_LO + SHIFT_W - 3 * D_B]], axis=-1)


def _pad_lanes(v, width):
    return jnp.pad(v, [(0, 0)] * (v.ndim - 1) + [(0, width - v.shape[-1])])


def kernel(x_prompt, x_sample, cache_cmp, cache_slc, cache_win, state_shift, state_wkv, page_table, c_prompt, c_sample, ln1_g, ln2_g, ada_w, ada_b, w_in, cmp_w1, cmp_b1, cmp_w2, rwkv_mu, rwkv_w0, rwkv_w2, rwkv_a0, rwkv_a2, rwkv_g2, rwkv_kk, rwkv_ka, rwkv_rk, rwkv_gn_g, rwkv_gn_b, w_branch, w_out, router_g_w, router_g_b, router_e_w, router_e_b, exp_w1, exp_w3, exp_w2, final_g):
    depth = w_in.shape[0]
    bp, t, d = x_prompt.shape
    bd, ts, _ = x_sample.shape
    assert ts == 1 and t % KC == 0 and t >= WINDOW + QB
    n_pages = page_table.shape[1]
    past = n_pages * PAGE
    wb = cache_win.shape[2]
    n_phys = cache_cmp.shape[1]
    n_p = bp * t

    slopes = jnp.exp2(-8.0 * (jnp.arange(N_HEADS_A, dtype=F32) + 1.0) / N_HEADS_A)
    c_rows = bp + bd
    c16 = jnp.zeros(((c_rows + SUBLANE - 1) // SUBLANE * SUBLANE, d), F32).at[:bp].set(c_prompt).at[bp:c_rows].set(c_sample)
    mod = adaln(c16, ada_w, ada_b).reshape(depth, c16.shape[0], 6, d)

    xp, xs = x_prompt, x_sample
    outs = {k: [] for k in ("cmp_p", "slc_p", "win_p", "shf_p", "wkv_p", "cmp_s", "slc_s", "win_s", "shf_s", "wkv_s")}
    pool_cmp2d = cache_cmp.reshape(-1, HD_A)
    pool_slc2d = cache_slc.reshape(-1, HD_A)
    win2d = cache_win.reshape(-1, HD_A)

    for l in range(depth):
        mp, ms = mod[l, :bp], mod[l, bp:c_rows]
        w_in_al = _align_in_cols(w_in[l], 1).astype(BF16)
        w1b = cmp_w1[l].astype(BF16)
        w_out_b = w_out[l].astype(BF16)
        prm = {
            "mu": _pad_lanes(rwkv_mu[l][None, :], RW_W),
            "w0": rwkv_w0[l][None, :], "w2": rwkv_w2[l], "a0": rwkv_a0[l][None, :], "a2": rwkv_a2[l],
            "g2": rwkv_g2[l], "kk": rwkv_kk[l][None, :], "ka": rwkv_ka[l][None, :],
            "rk": rwkv_rk[l].reshape(1, D_B), "gn_g": rwkv_gn_g[l][None, :], "gn_b": rwkv_gn_b[l][None, :],
        }
        wr = _pad_lanes(jnp.concatenate([router_g_w[l], router_e_w[l]], axis=1), LANE)
        br = _pad_lanes(jnp.concatenate([router_g_b[l], router_e_b[l]])[None, :], LANE)

        hp_ = norm_mod(xp, ln1_g[l], mp[:, 0], mp[:, 1], BF16).reshape(n_p, d)
        pp, ppb, kv_p = in_proj(hp_, w_in_al)
        cmp_kv = compress(pp, A_KV, 1, w1b, cmp_b1[l], cmp_w2[l])
        o_a = nsa_prompt(slopes, ppb, pp, cmp_kv, bp, t)
        o_b, s_fin = rwkv7(pp, jnp.zeros((bp, 1, RW_W), F32), jnp.zeros((bp, N_HEADS_B, HD_B, HD_B), F32),
                           prm, bp, t, 64, 64)
        merged = branch_merge(o_a, o_b, w_branch, l, pp)
        xp = out_proj_residual(merged.reshape(bp, t, d), w_out_b, xp, mp[:, 2])
        kv_p = kv_p.reshape(3, bp, t, 2, N_KV, HD_A)
        outs["cmp_p"].append(kv_p[0])
        outs["slc_p"].append(kv_p[1])
        outs["win_p"].append(kv_p[2, :, t - min(WINDOW, t):])
        outs["shf_p"].append(_rwkv_cols(pp.reshape(bp, t, IN_AL)[:, -1]))
        outs["wkv_p"].append(s_fin)

        hs_ = norm_mod(xs, ln1_g[l], ms[:, 0], ms[:, 1], BF16).reshape(bd, d)
        ps, psb, kv_s = in_proj(hs_, w_in_al)
        cmp_past = compress_paged(page_table, pool_cmp2d, l * n_phys, w1b, cmp_b1[l], cmp_w2[l])
        new_rows = jnp.zeros((bd, BLK, 2 * KV_W), F32).at[:, 0].set(ps[:, A_KV:A_KV + 2 * KV_W])
        cmp_new = compress(new_rows.reshape(bd * BLK, 2 * KV_W), 0, 1, w1b, cmp_b1[l], cmp_w2[l])
        nb_past = past // BLK
        nb_real = nb_past + 1
        nbp = (nb_real + LANE - 1) // LANE * LANE
        cmp_s = jnp.concatenate([cmp_past.reshape(2, N_KV, bd, nb_past, HD_A), cmp_new[:, :, :, None, :],
                                 jnp.zeros((2, N_KV, bd, nbp - nb_real, HD_A), F32)], axis=3)
        ps3 = ps.reshape(bd, 1, IN_AL)
        psb3 = psb.reshape(bd, 1, A_GATE)
        o_cmp, idx_full = nsa_sample_cmp(slopes, psb3, cmp_s, past, nb_real)
        idx_flat = idx_full[:, :, :N_SEL, 0].reshape(-1)
        o_a_s = nsa_sample_sel(idx_flat, page_table, slopes, psb3, ps3, pool_slc2d, l * n_phys,
                               win2d, l * bd, wb, o_cmp, past).reshape(bd, D_A)
        ps_pad = jnp.zeros((bd, SUBLANE, IN_AL), F32).at[:, 0].set(ps).reshape(bd * SUBLANE, IN_AL)
        o_b_s, s_fin_s = rwkv7(ps_pad, _pad_lanes(state_shift[l], RW_W)[:, None, :], state_wkv[l],
                               prm, bd, SUBLANE, SUBLANE, 1)
        o_b_s = o_b_s.reshape(bd, SUBLANE, D_B)[:, 0]
        merged_s = branch_merge(o_a_s, o_b_s, w_branch, l, ps)
        xs = out_proj_residual(merged_s.reshape(bd, 1, d), w_out_b, xs, ms[:, 2])
        kv_s = kv_s.reshape(3, bd, 1, 2, N_KV, HD_A)
        outs["cmp_s"].append(kv_s[0])
        outs["slc_s"].append(kv_s[1])
        outs["win_s"].append(jnp.concatenate([cache_win[l, :, 1:], kv_s[2]], axis=1))
        outs["shf_s"].append(_rwkv_cols(ps))
        outs["wkv_s"].append(s_fin_s)

        h2p, h_all = norm_mod(xp, ln2_g[l], mp[:, 3], mp[:, 4], F32, tiles_rows=n_p + bd,
                              tiles_into=jnp.zeros(((n_p + bd) * ROW_TILES, ROW_W), F32))
        h2s, h_all = norm_mod(xs, ln2_g[l], ms[:, 3], ms[:, 4], F32, tiles_rows=n_p + bd, tiles_into=h_all,
                              tiles_row0=n_p)
        eid_p, wt_p = router(h2p.reshape(n_p, d), wr, br)
        eid_s, wt_s = router(h2s.reshape(bd, d), wr, br)
        eid = jnp.concatenate([eid_p[:, :2], eid_s[:, :2]], axis=0)
        blk_e, n_used, row_tok, dest = _dispatch_tables(eid, MOE_BLOCK)
        ys = experts(h_all, blk_e, n_used, row_tok, exp_w1, exp_w3, exp_w2, l, MOE_BLOCK)
        xp = combine(ys, dest[:n_p], wt_p, xp, mp[:, 5])
        xs = combine(ys, dest[n_p:], wt_s, xs, ms[:, 5])

    y_prompt = final_norm(xp, final_g)
    y_sample = final_norm(xs, final_g)
    st = lambda k: jnp.stack(outs[k])
    return (y_prompt, y_sample, st("cmp_p"), st("slc_p"), st("win_p"), st("shf_p"), st("wkv_p"),
            st("cmp_s"), st("slc_s"), st("win_s"), st("shf_s"), st("wkv_s"))
```

```python
import functools

import jax
import jax.numpy as jnp
from jax import lax
from jax.experimental import pallas as pl
from jax.experimental.pallas import tpu as pltpu

F32 = jnp.float32
BF16 = jnp.bfloat16
I32 = jnp.int32

LANE = 128
SUBLANE = 8
VMEM_LIMIT = 56 * 1024 * 1024

D_MODEL = 2048
HD_A = 128
N_HEADS_A = 8
N_KV = 2
GQA_R = 4
BLK = 64
N_SEL = 16
WINDOW = 512
CMP_HID = 256
SCALE_A = HD_A ** -0.5
D_A = 1024
D_B = 1024
HD_B = 64
N_HEADS_B = 16
LORA_W, LORA_A, LORA_G = 64, 64, 32
GN_EPS = HD_B * 1e-5
N_GROUPS = 4
E_PER_GROUP = 8
N_EXPERTS = 32
D_EXPERT = 512
MOE_BLOCK = 128
RMS_EPS = 1e-6
PAGE = 128

Q_W = N_HEADS_A * HD_A
KV_W = N_KV * HD_A
OFF_KV = Q_W
OFF_GATE_A = OFF_KV + 6 * KV_W
OFF_RWKV = OFF_GATE_A + 3 * N_HEADS_A
SHIFT_W = 3 * D_B + LORA_W + LORA_A + LORA_G
OFF_MERGE = OFF_RWKV + SHIFT_W

A_KV = Q_W
A_GATE = A_KV + 6 * KV_W
A_LO = A_GATE + N_KV * LANE
A_RKV = A_LO + 2 * LANE
RW_W = 3 * D_B + 2 * LANE
A_MERGE = A_RKV + 3 * D_B
IN_AL = A_MERGE + 2 * D_MODEL
NEG = -1e30


def _cparams(sem):
    return pltpu.CompilerParams(dimension_semantics=sem, vmem_limit_bytes=VMEM_LIMIT)


def _dot(a, b, dims=(((1,), (0,)), ((), ()))):
    return lax.dot_general(a.astype(BF16), b.astype(BF16), dims, preferred_element_type=F32)


def _dot_nt(a, b):
    return _dot(a, b, (((1,), (1,)), ((), ())))


TN_DIMS = (((0,), (0,)), ((), ()))


def _dot_tn(a, b):
    return _dot(a, b, TN_DIMS)


def _split3(x):
    h = x.astype(BF16)
    r1 = x - h.astype(F32)
    m = r1.astype(BF16)
    lo = (r1 - m.astype(F32)).astype(BF16)
    return h, m, lo


def _dot3(a, b, dims=(((1,), (0,)), ((), ()))):
    ah, am, al = _split3(a)
    bh, bm, bl = _split3(b)
    d = lambda x, y: lax.dot_general(x, y, dims, preferred_element_type=F32)
    return (d(ah, bh) + (d(ah, bm) + d(am, bh))) + ((d(am, bm) + d(ah, bl)) + d(al, bh))


def _sigmoid(x):
    return 1.0 / (1.0 + jnp.exp(-x))


def _softplus(x):
    return jnp.maximum(x, 0.0) + jnp.log(1.0 + jnp.exp(-jnp.abs(x)))


def _adaln_kernel(c_ref, w_ref, b_ref, o_ref):
    c = c_ref[...]
    h = c * _sigmoid(c)
    o_ref[...] = _dot(h, w_ref[...]) + b_ref[...]


def adaln(c16, ada_w, ada_b):
    depth, d, n = ada_w.shape
    tn = 1024
    return pl.pallas_call(
        _adaln_kernel,
        out_shape=jax.ShapeDtypeStruct((depth, c16.shape[0], n), F32),
        grid=(depth, n // tn),
        in_specs=[pl.BlockSpec(c16.shape, lambda l, j: (0, 0)),
                  pl.BlockSpec((None, d, tn), lambda l, j: (l, 0, j)),
                  pl.BlockSpec((None, 1, tn), lambda l, j: (l, 0, j))],
        out_specs=pl.BlockSpec((None, c16.shape[0], tn), lambda l, j: (l, 0, j)),
        compiler_params=_cparams(("parallel", "parallel")),
        name="adaln",
    )(c16, ada_w, ada_b.reshape(depth, 1, n))


ROW_W = D_MODEL
ROW_TILES = D_MODEL // ROW_W


def _to_row_tiles(ref, val):
    rows = val.shape[0]
    if ROW_TILES == 1:
        ref[...] = val
        return
    for c in range(ROW_TILES):
        ref[pl.ds(c, rows, stride=ROW_TILES), :] = val[:, c * ROW_W:(c + 1) * ROW_W]


def _from_row_tiles(ref, rows, c):
    if ROW_TILES == 1:
        return ref[...]
    return ref[pl.ds(c, rows, stride=ROW_TILES), :]


def _norm_kernel(x_ref, g_ref, sh_ref, sc_ref, o_ref):
    x = x_ref[...]
    y = x * lax.rsqrt(jnp.mean(x * x, axis=-1, keepdims=True) + RMS_EPS)
    y = y * g_ref[...]
    o_ref[...] = (y * (1.0 + sc_ref[...]) + sh_ref[...]).astype(o_ref.dtype)


def _norm_tiles_kernel(x_ref, g_ref, sh_ref, sc_ref, *refs):
    o_ref, tiles_ref = refs[-2:]
    _norm_kernel(x_ref, g_ref, sh_ref, sc_ref, o_ref)
    rows = o_ref.shape[0] * ROW_TILES
    if tiles_ref.shape[0] == rows:
        _to_row_tiles(tiles_ref, o_ref[...])
    else:
        tiles_ref[pl.ds((pl.program_id(0) % SUBLANE) * rows, rows), :] = o_ref[...]


def norm_mod(x, g, shift, scale, out_dtype, tiles_rows=0, tiles_into=None, tiles_row0=0):
    b, t, d = x.shape
    tr = min(t, 256)
    nt = t // tr
    in_specs = [pl.BlockSpec((None, tr, d), lambda i, j: (i, j, 0)),
                pl.BlockSpec((1, d), lambda i, j: (0, 0)),
                pl.BlockSpec((None, 1, d), lambda i, j: (i, 0, 0)),
                pl.BlockSpec((None, 1, d), lambda i, j: (i, 0, 0))]
    args = [x, g.reshape(1, d), shift.reshape(b, 1, d), scale.reshape(b, 1, d)]
    row_spec = pl.BlockSpec((None, tr, d), lambda i, j: (i, j, 0))
    if not tiles_rows:
        return pl.pallas_call(
            _norm_kernel, out_shape=jax.ShapeDtypeStruct((b, t, d), out_dtype), grid=(b, nt),
            in_specs=in_specs, out_specs=row_spec,
            compiler_params=_cparams(("parallel", "parallel")), name="norm_mod")(*args)
    assert out_dtype == F32
    if tr * ROW_TILES % SUBLANE == 0:
        assert tiles_row0 % tr == 0
        blk0 = tiles_row0 // tr
        tiles_spec = pl.BlockSpec((tr * ROW_TILES, ROW_W), lambda i, j: (blk0 + i * nt + j, 0))
        sem = ("parallel", "parallel")
    else:
        assert t == 1 and ROW_TILES == 1 and b % SUBLANE == 0 and tiles_row0 % SUBLANE == 0
        blk0 = tiles_row0 // SUBLANE
        tiles_spec = pl.BlockSpec((SUBLANE, ROW_W), lambda i, j: (blk0 + i // SUBLANE, 0))
        sem = ("arbitrary", "arbitrary")
    aliases = {}
    if tiles_into is not None:
        in_specs.append(pl.BlockSpec(memory_space=pl.ANY))
        args.append(tiles_into)
        aliases = {len(args) - 1: 1}
    return pl.pallas_call(
        _norm_tiles_kernel,
        out_shape=[jax.ShapeDtypeStruct((b, t, d), F32),
                   jax.ShapeDtypeStruct((tiles_rows * ROW_TILES, ROW_W), F32)],
        grid=(b, nt),
        in_specs=in_specs,
        out_specs=[row_spec, tiles_spec],
        input_output_aliases=aliases,
        compiler_params=_cparams(sem),
        name="norm_tiles",
    )(*args)


IN_TN = 2 * KV_W
N_BF16_TILES = A_GATE // IN_TN
KV_TILE0 = A_KV // IN_TN
KV_ROWS = 2 * N_KV


def _in_proj_kernel(a_ref, w_ref, o_ref, ob_ref, kv_ref):
    j = pl.program_id(1)
    acc = _dot(a_ref[...], w_ref[...])
    o_ref[...] = acc

    @pl.when(j < N_BF16_TILES)
    def _():
        ob_ref[...] = acc.astype(BF16)

    @pl.when((j >= KV_TILE0) & (j < N_BF16_TILES))
    def _():
        tm = acc.shape[0]
        for c4 in range(KV_ROWS):
            kv_ref[pl.ds(c4, tm, stride=KV_ROWS), :] = acc[:, c4 * HD_A:(c4 + 1) * HD_A]


def _row_tile(m, cap):
    return m if m <= cap else cap


def in_proj(a, w):
    m, k = a.shape
    tm = _row_tile(m, 2048)
    tn = IN_TN
    assert m % tm == 0
    return pl.pallas_call(
        _in_proj_kernel,
        out_shape=[jax.ShapeDtypeStruct((m, IN_AL), F32), jax.ShapeDtypeStruct((m, A_GATE), BF16),
                   jax.ShapeDtypeStruct((3, m * KV_ROWS, HD_A), F32)],
        grid=(m // tm, IN_AL // tn),
        in_specs=[pl.BlockSpec((tm, k), lambda i, j: (i, 0)),
                  pl.BlockSpec((k, tn), lambda i, j: (0, j))],
        out_specs=[pl.BlockSpec((tm, tn), lambda i, j: (i, j)),
                   pl.BlockSpec((tm, tn), lambda i, j: (i, jnp.minimum(j, N_BF16_TILES - 1))),
                   pl.BlockSpec((None, tm * KV_ROWS, HD_A),
                                lambda i, j: (jnp.clip(j - KV_TILE0, 0, 2), i, 0))],
        compiler_params=_cparams(("parallel", "arbitrary")),
        name="proj_in",
    )(a, w)


def _branch_kernel(oa_ref, ob_ref, wa_ref, wb_ref, ga_ref, gb_ref, o_ref):
    ya = _dot(oa_ref[...], wa_ref[...])
    yb = _dot(ob_ref[...], wb_ref[...])
    o_ref[...] = (_sigmoid(ga_ref[...]) * ya + _sigmoid(gb_ref[...]) * yb).astype(o_ref.dtype)


def branch_merge(o_a, o_b, w_branch, l, pp):
    m = o_a.shape[0]
    tm = _row_tile(m, 1024)
    tn = 512
    gcol = A_MERGE // tn
    return pl.pallas_call(
        _branch_kernel,
        out_shape=jax.ShapeDtypeStruct((m, D_MODEL), BF16),
        grid=(m // tm, D_MODEL // tn),
        in_specs=[pl.BlockSpec((tm, D_A), lambda i, j: (i, 0)),
                  pl.BlockSpec((tm, D_B), lambda i, j: (i, 0)),
                  pl.BlockSpec((None, D_A, tn), lambda i, j: (l, 0, j)),
                  pl.BlockSpec((None, D_B, tn), lambda i, j: (l, D_A // D_B, j)),
                  pl.BlockSpec((tm, tn), lambda i, j: (i, gcol + j)),
                  pl.BlockSpec((tm, tn), lambda i, j: (i, gcol + D_MODEL // tn + j))],
        out_specs=pl.BlockSpec((tm, tn), lambda i, j: (i, j)),
        compiler_params=_cparams(("parallel", "parallel")),
        name="branch_merge",
    )(o_a, o_b, w_branch, w_branch, pp, pp)


def _resid_kernel(m_ref, w_ref, x_ref, g_ref, o_ref):
    o_ref[...] = x_ref[...] + g_ref[...] * _dot(m_ref[...], w_ref[...])


def out_proj_residual(merged, w_out_l, x, gate):
    b, t, d = x.shape
    tm = _row_tile(t, 2048)
    tn = 512
    nt = t // tm
    return pl.pallas_call(
        _resid_kernel,
        out_shape=jax.ShapeDtypeStruct((b, t, d), F32),
        grid=(b, nt, d // tn),
        in_specs=[pl.BlockSpec((None, tm, d), lambda i, r, j: (i, r, 0)),
                  pl.BlockSpec((d, tn), lambda i, r, j: (0, j)),
                  pl.BlockSpec((None, tm, tn), lambda i, r, j: (i, r, j)),
                  pl.BlockSpec((None, 1, tn), lambda i, r, j: (i, 0, j))],
        out_specs=pl.BlockSpec((None, tm, tn), lambda i, r, j: (i, r, j)),
        compiler_params=_cparams(("parallel", "parallel", "parallel")),
        name="out_proj",
    )(merged, w_out_l, x, gate.reshape(b, 1, d))


T_PER_STEP = 8


def _gelu_tanh(x):
    return 0.5 * x * (1.0 + jnp.tanh(0.7978845608028654 * (x + 0.044715 * x * x * x)))


def _compress_kernel(x_ref, w1_ref, b1_ref, w2_ref, o_ref, acc_ref):
    tc = pl.program_id(3)
    nblk = o_ref.shape[0]

    @pl.when(tc == 0)
    def _():
        acc_ref[...] = jnp.zeros_like(acc_ref)

    acc = acc_ref[...]
    for tl in range(T_PER_STEP):
        t = tc * T_PER_STEP + tl
        xt = x_ref[pl.ds(t, nblk, stride=BLK), :]
        acc = acc + _dot(xt, w1_ref[tl * HD_A:(tl + 1) * HD_A, :])
    acc_ref[...] = acc

    @pl.when(tc == pl.num_programs(3) - 1)
    def _():
        h = _gelu_tanh(acc + b1_ref[...])
        o_ref[...] = _dot(h, w2_ref[...])


def compress(rows, col0, n_slabs, w1b, b1, w2):
    r = rows.shape[0] // n_slabs
    nblk = r // BLK
    cb = col0 // HD_A
    return pl.pallas_call(
        _compress_kernel,
        out_shape=jax.ShapeDtypeStruct((2, N_KV, n_slabs * nblk, HD_A), F32),
        grid=(2, N_KV, n_slabs, BLK // T_PER_STEP),
        in_specs=[pl.BlockSpec((r, HD_A), lambda kv, g, s, tc: (s, cb + kv * N_KV + g)),
                  pl.BlockSpec((None, T_PER_STEP * HD_A, CMP_HID), lambda kv, g, s, tc: (kv, tc, 0)),
                  pl.BlockSpec((None, 1, CMP_HID), lambda kv, g, s, tc: (kv, 0, 0)),
                  pl.BlockSpec((None, CMP_HID, HD_A), lambda kv, g, s, tc: (kv, 0, 0))],
        out_specs=pl.BlockSpec((None, None, nblk, HD_A), lambda kv, g, s, tc: (kv, g, s, 0)),
        scratch_shapes=[pltpu.VMEM((nblk, CMP_HID), F32)],
        compiler_params=_cparams(("parallel", "parallel", "parallel", "arbitrary")),
        name="compress",
    )(rows, w1b, b1.reshape(2, 1, CMP_HID), w2)


QB = 128
KC = 512
HEAD_GROUP = 4


def _softmax_cols(s, valid):
    sm = jnp.where(valid, s, NEG)
    m = jnp.max(sm, axis=0, keepdims=True)
    e = jnp.where(valid, jnp.exp(sm - m), 0.0)
    return e, jnp.sum(e, axis=0, keepdims=True)


def _safe(den):
    return jnp.where(den > 0, den, 1.0)


def _select_mask_t(imp, jblk, cur):
    nb = imp.shape[0]
    score = jnp.where(jblk < cur, imp, jnp.where(jblk == cur, GQA_R + 1.0, -1.0))
    rank = jnp.zeros(imp.shape, I32)
    for i in range(nb):
        row = score[i:i + 1, :]
        beats = (row > score) | ((row == score) & (jblk > i))
        rank = rank + beats.astype(I32)
    return (rank < N_SEL) & (jblk <= cur)


def _nsa_prompt_kernel(slopes_ref, q_ref, kc_ref, vc_ref, ks_ref, vs_ref, kw_ref, vw_ref, gate_ref,
                       o_ref, bias_ref, acc_ref, win_ref):
    g = pl.program_id(1)
    i = pl.program_id(2)
    t_len = ks_ref.shape[0]
    nb = t_len // BLK
    q0 = i * QB
    span = WINDOW + QB

    jblk = lax.broadcasted_iota(I32, (nb, QB), 0)
    qpos_l = q0 + lax.broadcasted_iota(I32, (nb, QB), 1)
    dist_c = qpos_l - ((jblk + 1) * BLK - 1)
    valid_c = dist_c >= 0
    dist_cf = dist_c.astype(F32)
    kc = kc_ref[...]
    vc = vc_ref[...]

    qs = [q_ref[:, r * HD_A:(r + 1) * HD_A] for r in range(GQA_R)]
    slopes = [slopes_ref[g * GQA_R + r] for r in range(GQA_R)]

    start = pl.multiple_of(jnp.maximum(i - WINDOW // QB, 0) * QB, QB)
    kw = kw_ref[pl.ds(start, span), :]
    vw = vw_ref[pl.ds(start, span), :]
    dist_w = (q0 + lax.broadcasted_iota(I32, (QB, span), 0)) - (start + lax.broadcasted_iota(I32, (QB, span), 1))
    bias_w = jnp.where((dist_w >= 0) & (dist_w < WINDOW), 0.0, NEG)
    kpos_w = (start + lax.broadcasted_iota(I32, (1, span), 1)).astype(F32)
    sw = [_dot_nt(qs[r], kw) * SCALE_A + (bias_w + slopes[r] * kpos_w) for r in range(GQA_R)]
    ew = [jnp.exp(sw[r] - jnp.max(sw[r], axis=-1, keepdims=True)) for r in range(GQA_R)]
    den_w = [jnp.sum(ew[r], axis=-1, keepdims=True) for r in range(GQA_R)]
    pv_w = [_dot(ew[r], vw) for r in range(GQA_R)]
    for r in range(GQA_R):
        win_ref[r] = pv_w[r] / den_w[r]

    imp = jnp.zeros((nb, QB), F32)
    o_cmp = []
    for r in range(GQA_R):
        s = _dot_nt(kc, qs[r]) * SCALE_A - slopes[r] * dist_cf
        e, den = _softmax_cols(s, valid_c)
        p = e / _safe(den)
        imp = imp + p
        o_cmp.append(_dot_tn(p, vc))

    sel = _select_mask_t(imp, jblk, qpos_l // BLK)
    expand = (lax.broadcasted_iota(I32, (nb, t_len), 1) // BLK
              == lax.broadcasted_iota(I32, (nb, t_len), 0))
    key_sel = _dot_tn(sel.astype(F32), expand.astype(F32))
    row_q = q0 + lax.broadcasted_iota(I32, (QB, KC), 0)
    col_k = lax.broadcasted_iota(I32, (QB, KC), 1)
    for cc in range(t_len // KC):
        @pl.when(cc * KC < q0 + QB)
        def _(cc=cc):
            ok = (key_sel[:, cc * KC:(cc + 1) * KC] > 0.5) & (row_q >= cc * KC + col_k)
            bias_ref[cc] = jnp.where(ok, 0.0, NEG)

    n_chunks = (q0 + QB + KC - 1) // KC
    kpos0 = lax.broadcasted_iota(I32, (1, KC), 1).astype(F32)
    acc_ref[...] = jnp.zeros_like(acc_ref)

    def body(c, carry):
        ms, ls = carry
        k0 = pl.multiple_of(c * KC, KC)
        kk = ks_ref[pl.ds(k0, KC), :]
        vv = vs_ref[pl.ds(k0, KC), :]
        bias = bias_ref[c]
        kpos = kpos0 + k0.astype(F32)
        m_new, l_new = [None] * GQA_R, [None] * GQA_R
        for r0 in range(0, GQA_R, HEAD_GROUP):
            grp = range(r0, r0 + HEAD_GROUP)
            ss = {r: _dot_nt(qs[r], kk) * SCALE_A + (bias + slopes[r] * kpos) for r in grp}
            for r in grp:
                m_new[r] = jnp.maximum(ms[r], jnp.max(ss[r], axis=-1, keepdims=True))
            alpha = {r: jnp.exp(ms[r] - m_new[r]) for r in grp}
            es = {r: jnp.exp(ss[r] - m_new[r]) for r in grp}
            for r in grp:
                l_new[r] = alpha[r] * ls[r] + jnp.sum(es[r], axis=-1, keepdims=True)
            pv = {r: _dot(es[r], vv) for r in grp}
            for r in grp:
                acc_ref[r] = alpha[r] * acc_ref[r] + pv[r]
        return tuple(m_new), tuple(l_new)

    init = (tuple(jnp.full((QB, 1), NEG, F32) for _ in range(GQA_R)),
            tuple(jnp.zeros((QB, 1), F32) for _ in range(GQA_R)))
    _, l_sel = lax.fori_loop(0, n_chunks, body, init)

    gates = _sigmoid(gate_ref[...])
    for r in range(GQA_R):
        o = (gates[:, 3 * r:3 * r + 1] * o_cmp[r] + gates[:, 3 * r + 1:3 * r + 2] * (acc_ref[r] / l_sel[r])
             + gates[:, 3 * r + 2:3 * r + 3] * win_ref[r])
        o_ref[:, r * HD_A:(r + 1) * HD_A] = o.astype(o_ref.dtype)


def nsa_prompt(slopes, ppb, pp, cmp_kv, b, t):
    nq = t // QB
    nb = t // BLK
    kvb = A_KV // HD_A

    def kv_spec(branch, kv):
        return pl.BlockSpec((t, HD_A), lambda bi, g, i, s: (bi, kvb + branch * 4 + kv * 2 + g))

    grid_spec = pltpu.PrefetchScalarGridSpec(
        num_scalar_prefetch=1,
        grid=(b, N_KV, nq),
        in_specs=[pl.BlockSpec((QB, GQA_R * HD_A), lambda bi, g, i, s: (bi * nq + i, g)),
                  pl.BlockSpec((None, None, nb, HD_A), lambda bi, g, i, s: (0, g, bi, 0)),
                  pl.BlockSpec((None, None, nb, HD_A), lambda bi, g, i, s: (1, g, bi, 0)),
                  kv_spec(1, 0), kv_spec(1, 1), kv_spec(2, 0), kv_spec(2, 1),
                  pl.BlockSpec((QB, LANE), lambda bi, g, i, s: (bi * nq + i, A_GATE // LANE + g))],
        out_specs=pl.BlockSpec((QB, GQA_R * HD_A), lambda bi, g, i, s: (bi * nq + i, g)),
        scratch_shapes=[pltpu.VMEM((t // KC, QB, KC), F32), pltpu.VMEM((GQA_R, QB, HD_A), F32),
                        pltpu.VMEM((GQA_R, QB, HD_A), F32)],
    )
    return pl.pallas_call(
        _nsa_prompt_kernel,
        out_shape=jax.ShapeDtypeStruct((b * t, D_A), BF16),
        grid_spec=grid_spec,
        compiler_params=_cparams(("parallel", "parallel", "arbitrary")),
        name="nsa_prompt",
    )(slopes, ppb, cmp_kv, cmp_kv, ppb, ppb, ppb, ppb, pp)


PAGE_ROWS = PAGE * KV_ROWS
SLAB_PAGES = 64


def _compress_paged_kernel(pt_ref, pool_hbm, w1_ref, b1_ref, w2_ref, o_ref, slab_ref, acc_ref, sem, *, page0):
    s = pl.program_id(0)
    tc = pl.program_id(1)
    n_slabs = pl.num_programs(0)
    nblk = SLAB_PAGES * (PAGE // BLK)

    def page_copy(slab, p, slot):
        phys = pt_ref[slab * SLAB_PAGES + p]
        return pltpu.make_async_copy(pool_hbm.at[pl.ds((page0 + phys) * PAGE_ROWS, PAGE_ROWS), :],
                                     slab_ref.at[slot, pl.ds(p * PAGE_ROWS, PAGE_ROWS), :], sem.at[slot])

    def start_slab(slab, slot):
        lax.fori_loop(0, SLAB_PAGES, lambda p, c: (page_copy(slab, p, slot).start(), c)[1], 0)

    def wait_slab(slab, slot):
        lax.fori_loop(0, SLAB_PAGES, lambda p, c: (page_copy(slab, p, slot).wait(), c)[1], 0)

    slot = s % 2

    @pl.when(tc == 0)
    def _():
        @pl.when(s == 0)
        def _():
            start_slab(0, 0)

        wait_slab(s, slot)

        @pl.when(s + 1 < n_slabs)
        def _():
            start_slab(s + 1, 1 - slot)

        acc_ref[...] = jnp.zeros_like(acc_ref)

    for tl in range(T_PER_STEP):
        t = tc * T_PER_STEP + tl
        for kv in range(2):
            xt = jnp.concatenate(
                [slab_ref[slot, pl.ds(t * KV_ROWS + kv * N_KV + g, nblk, stride=BLK * KV_ROWS), :]
                 for g in range(N_KV)], axis=0)
            acc_ref[kv] += _dot(xt, w1_ref[kv, tl * HD_A:(tl + 1) * HD_A, :])

    @pl.when(tc == pl.num_programs(1) - 1)
    def _():
        for kv in range(2):
            o = _dot(_gelu_tanh(acc_ref[kv] + b1_ref[kv]), w2_ref[kv])
            for g in range(N_KV):
                o_ref[kv, g] = o[g * nblk:(g + 1) * nblk]


def compress_paged(page_table, pool2d, page0, w1b, b1, w2):
    bd, n_pages = page_table.shape
    assert n_pages % SLAB_PAGES == 0
    n_slabs = bd * n_pages // SLAB_PAGES
    nblk = SLAB_PAGES * (PAGE // BLK)
    grid_spec = pltpu.PrefetchScalarGridSpec(
        num_scalar_prefetch=1,
        grid=(n_slabs, BLK // T_PER_STEP),
        in_specs=[pl.BlockSpec(memory_space=pl.ANY),
                  pl.BlockSpec((2, T_PER_STEP * HD_A, CMP_HID), lambda s, tc, pt: (0, tc, 0)),
                  pl.BlockSpec((2, 1, CMP_HID), lambda s, tc, pt: (0, 0, 0)),
                  pl.BlockSpec((2, CMP_HID, HD_A), lambda s, tc, pt: (0, 0, 0))],
        out_specs=pl.BlockSpec((2, N_KV, nblk, HD_A), lambda s, tc, pt: (0, 0, s, 0)),
        scratch_shapes=[pltpu.VMEM((2, SLAB_PAGES * PAGE_ROWS, HD_A), F32),
                        pltpu.VMEM((2, N_KV * nblk, CMP_HID), F32),
                        pltpu.SemaphoreType.DMA((2,))],
    )
    return pl.pallas_call(
        functools.partial(_compress_paged_kernel, page0=page0),
        out_shape=jax.ShapeDtypeStruct((2, N_KV, n_slabs * nblk, HD_A), F32),
        grid_spec=grid_spec,
        compiler_params=_cparams(("arbitrary", "arbitrary")),
        name="compress_paged",
    )(page_table.reshape(-1), pool2d, w1b, b1.reshape(2, 1, CMP_HID), w2)


SEL_PAD = 128


def _heads_on_sublanes(q_ref):
    q = q_ref[...].astype(F32)
    row = lax.broadcasted_iota(I32, (SUBLANE, HD_A), 0)
    q8 = jnp.zeros((SUBLANE, HD_A), F32)
    for r in range(GQA_R):
        q8 = jnp.where(row == r, jnp.broadcast_to(q[:, r * HD_A:(r + 1) * HD_A], (SUBLANE, HD_A)), q8)
    return q8


def _slopes_on_sublanes(slopes_ref, g):
    row = lax.broadcasted_iota(I32, (SUBLANE, 1), 0)
    slope = jnp.zeros((SUBLANE, 1), F32)
    for r in range(GQA_R):
        slope = jnp.where(row == r, slopes_ref[g * GQA_R + r], slope)
    return slope


def _softmax_rows(s, valid):
    sm = jnp.where(valid, s, NEG)
    m = jnp.max(sm, axis=-1, keepdims=True)
    e = jnp.where(valid, jnp.exp(sm - m), 0.0)
    return e, jnp.sum(e, axis=-1, keepdims=True)


def _nsa_sample_cmp_kernel(slopes_ref, q_ref, kc_ref, vc_ref, o_ref, idx_ref, *, past, nb_real):
    g = pl.program_id(1)
    nbp = kc_ref.shape[0]
    kc = kc_ref[...]
    vc = vc_ref[...]
    q8 = _heads_on_sublanes(q_ref)
    slope = _slopes_on_sublanes(slopes_ref, g)
    head = lax.broadcasted_iota(I32, (SUBLANE, 1), 0) < GQA_R
    jrow = lax.broadcasted_iota(I32, (1, nbp), 1)
    dist = past - ((jrow + 1) * BLK - 1)
    valid = (dist >= 0) & (jrow < nb_real)
    s = _dot_nt(q8, kc) * SCALE_A - slope * dist.astype(F32)
    e, den = _softmax_rows(s, valid)
    p = e / _safe(den)
    o = _dot(p, vc)
    for r in range(GQA_R):
        o_ref[:, r * HD_A:(r + 1) * HD_A] = o[r:r + 1, :]

    imp = jnp.sum(jnp.where(head, p, 0.0), axis=0, keepdims=True)
    cur = past // BLK
    score_r = jnp.where(jrow < cur, imp, jnp.where(jrow == cur, GQA_R + 1.0, -1.0))
    score_r = jnp.where(jrow < nb_real, score_r, -2.0)
    ii = lax.broadcasted_iota(I32, (nbp, nbp), 0)
    jj = lax.broadcasted_iota(I32, (nbp, nbp), 1)
    sc_b = jnp.broadcast_to(score_r, (nbp, nbp))
    score_c = jnp.sum(jnp.where(ii == jj, sc_b, 0.0), axis=1, keepdims=True)
    beats = (score_c > sc_b) | ((score_c == sc_b) & (ii < jj))
    rank_r = jnp.sum(beats.astype(F32), axis=0, keepdims=True)
    nn = lax.broadcasted_iota(I32, (SEL_PAD, nbp), 0).astype(F32)
    jn = lax.broadcasted_iota(I32, (SEL_PAD, nbp), 1).astype(F32)
    hit = jnp.broadcast_to(rank_r, (SEL_PAD, nbp)) == nn
    idx_c = jnp.sum(jnp.where(hit, jn, 0.0), axis=1, keepdims=True)
    idx_ref[...] = jnp.broadcast_to(idx_c, (SEL_PAD, LANE)).astype(I32)


def nsa_sample_cmp(slopes, ppb3, cmp_kv_s, past, nb_real):
    bd = ppb3.shape[0]
    nbp = cmp_kv_s.shape[3]
    grid_spec = pltpu.PrefetchScalarGridSpec(
        num_scalar_prefetch=1,
        grid=(bd, N_KV),
        in_specs=[pl.BlockSpec((None, 1, GQA_R * HD_A), lambda bi, g, s: (bi, 0, g)),
                  pl.BlockSpec((None, None, None, nbp, HD_A), lambda bi, g, s: (0, g, bi, 0, 0)),
                  pl.BlockSpec((None, None, None, nbp, HD_A), lambda bi, g, s: (1, g, bi, 0, 0))],
        out_specs=[pl.BlockSpec((None, 1, GQA_R * HD_A), lambda bi, g, s: (bi, 0, g)),
                   pl.BlockSpec((None, None, SEL_PAD, LANE), lambda bi, g, s: (bi, g, 0, 0))],
    )
    return pl.pallas_call(
        functools.partial(_nsa_sample_cmp_kernel, past=past, nb_real=nb_real),
        out_shape=[jax.ShapeDtypeStruct((bd, 1, D_A), F32),
                   jax.ShapeDtypeStruct((bd, N_KV, SEL_PAD, LANE), I32)],
        grid_spec=grid_spec,
        compiler_params=_cparams(("parallel", "parallel")),
        name="nsa_sample_cmp",
    )(slopes, ppb3, cmp_kv_s, cmp_kv_s)


W_PAD = LANE


def _nsa_sample_sel_kernel(idx_ref, pt_ref, slopes_ref, q_ref, pool_hbm, kn_ref, vn_ref,
                           wbuf_ref, kwn_ref, vwn_ref, gate_ref, oc_ref, o_ref,
                           blk_ref, ksel_ref, vsel_ref, kwin_ref, vwin_ref, sem, *, past, page0, n_pages):
    bi = pl.program_id(0)
    g = pl.program_id(1)
    nb_past = past // BLK
    bpp = PAGE // BLK
    blk_rows = BLK * KV_ROWS
    base = (bi * N_KV + g) * N_SEL

    def block_copy(n):
        ip = jnp.clip(idx_ref[base + n], 0, nb_past - 1)
        phys = pt_ref[bi * n_pages + ip // bpp]
        row0 = ((page0 + phys) * bpp + ip % bpp) * blk_rows
        return pltpu.make_async_copy(pool_hbm.at[pl.ds(row0, blk_rows), :], blk_ref.at[n], sem)

    for n in range(N_SEL):
        block_copy(n).start()
    for n in range(N_SEL):
        block_copy(n).wait()
    first = lax.broadcasted_iota(I32, (BLK, HD_A), 0) == 0
    k_new = jnp.where(first, kn_ref[...], 0.0)
    v_new = jnp.where(first, vn_ref[...], 0.0)
    for n in range(N_SEL):
        is_past = idx_ref[base + n] < nb_past
        blk = blk_ref.at[n]
        ksel_ref[n * BLK:(n + 1) * BLK, :] = jnp.where(is_past, blk[pl.ds(g, BLK, stride=KV_ROWS), :], k_new)
        vsel_ref[n * BLK:(n + 1) * BLK, :] = jnp.where(is_past, blk[pl.ds(N_KV + g, BLK, stride=KV_ROWS), :], v_new)

    q8 = _heads_on_sublanes(q_ref)
    slope = _slopes_on_sublanes(slopes_ref, g)
    cur = past // BLK
    lane = lax.broadcasted_iota(I32, (1, N_SEL * BLK), 1)
    slot = lane // BLK
    idx_row = jnp.zeros((1, N_SEL * BLK), I32)
    for m in range(N_SEL):
        idx_row = jnp.where(slot == m, idx_ref[base + m], idx_row)
    dist = past - (idx_row * BLK + lane % BLK)
    valid = (idx_row <= cur) & (dist >= 0)
    s = _dot_nt(q8, ksel_ref[...]) * SCALE_A - slope * dist.astype(F32)
    e, den = _softmax_rows(s, valid)
    o_sel = _dot(e / _safe(den), vsel_ref[...])
    wb = wbuf_ref.shape[0] // KV_ROWS
    kwin_ref[0:wb, :] = wbuf_ref[pl.ds(g, wb, stride=KV_ROWS), :]
    vwin_ref[0:wb, :] = wbuf_ref[pl.ds(N_KV + g, wb, stride=KV_ROWS), :]
    first_w = lax.broadcasted_iota(I32, (W_PAD, HD_A), 0) == 0
    kwin_ref[wb:wb + W_PAD, :] = jnp.where(first_w, kwn_ref[...], 0.0)
    vwin_ref[wb:wb + W_PAD, :] = jnp.where(first_w, vwn_ref[...], 0.0)
    lane_w = lax.broadcasted_iota(I32, (1, wb + W_PAD), 1)
    dist_w = wb - lane_w
    valid_w = (dist_w >= 0) & (dist_w < WINDOW)
    s = _dot_nt(q8, kwin_ref[...]) * SCALE_A - slope * dist_w.astype(F32)
    e, den = _softmax_rows(s, valid_w)
    o_win = _dot(e / _safe(den), vwin_ref[...])
    gates = _sigmoid(gate_ref[...])
    for r in range(GQA_R):
        o = (gates[:, 3 * r:3 * r + 1] * oc_ref[:, r * HD_A:(r + 1) * HD_A]
             + gates[:, 3 * r + 1:3 * r + 2] * o_sel[r:r + 1, :]
             + gates[:, 3 * r + 2:3 * r + 3] * o_win[r:r + 1, :])
        o_ref[:, r * HD_A:(r + 1) * HD_A] = o.astype(o_ref.dtype)


def nsa_sample_sel(idx_flat, page_table, slopes, ppb3, pp3, pool2d, page0, win2d, seq0, wb, o_cmp, past):
    bd = ppb3.shape[0]
    n_pages = page_table.shape[1]
    kvb = A_KV // HD_A

    def new_spec(branch, kv):
        return pl.BlockSpec((None, 1, HD_A), lambda bi, g, idx, pt, s: (bi, 0, kvb + branch * 4 + kv * 2 + g))

    grid_spec = pltpu.PrefetchScalarGridSpec(
        num_scalar_prefetch=3,
        grid=(bd, N_KV),
        in_specs=[pl.BlockSpec((None, 1, GQA_R * HD_A), lambda bi, g, idx, pt, s: (bi, 0, g)),
                  pl.BlockSpec(memory_space=pl.ANY), new_spec(1, 0), new_spec(1, 1),
                  pl.BlockSpec((wb * KV_ROWS, HD_A), lambda bi, g, idx, pt, s: (seq0 + bi, 0)),
                  new_spec(2, 0), new_spec(2, 1),
                  pl.BlockSpec((None, 1, LANE), lambda bi, g, idx, pt, s: (bi, 0, A_GATE // LANE + g)),
                  pl.BlockSpec((None, 1, GQA_R * HD_A), lambda bi, g, idx, pt, s: (bi, 0, g))],
        out_specs=pl.BlockSpec((None, 1, GQA_R * HD_A), lambda bi, g, idx, pt, s: (bi, 0, g)),
        scratch_shapes=[pltpu.VMEM((N_SEL, BLK * KV_ROWS, HD_A), F32),
                        pltpu.VMEM((N_SEL * BLK, HD_A), F32), pltpu.VMEM((N_SEL * BLK, HD_A), F32),
                        pltpu.VMEM((wb + W_PAD, HD_A), F32), pltpu.VMEM((wb + W_PAD, HD_A), F32),
                        pltpu.SemaphoreType.DMA(())],
    )
    return pl.pallas_call(
        functools.partial(_nsa_sample_sel_kernel, past=past, page0=page0, n_pages=n_pages),
        out_shape=jax.ShapeDtypeStruct((bd, 1, D_A), BF16),
        grid_spec=grid_spec,
        compiler_params=_cparams(("arbitrary", "arbitrary")),
        name="nsa_sample_sel",
    )(idx_flat, page_table.reshape(-1), slopes, ppb3, pool2d, pp3, pp3, win2d, pp3, pp3, pp3, o_cmp)


def _cumsum_rows(x):
    c = x.shape[0]
    row = lax.broadcasted_iota(I32, x.shape, 0)
    sh = 1
    while sh < c:
        x = x + jnp.where(row >= sh, pltpu.roll(x, sh, 0), 0.0)
        sh *= 2
    return x


def _split2(x):
    hi = x.astype(BF16)
    return hi, (x - hi.astype(F32)).astype(BF16)


def _dot2(a, b, dims=(((1,), (0,)), ((), ()))):
    ah, al = _split2(a)
    bh, bl = _split2(b)
    d = lambda x, y: lax.dot_general(x, y, dims, preferred_element_type=F32)
    return d(ah, bh) + (d(ah, bl) + d(al, bh))


_dot_t = _dot2
NT = (((1,), (1,)), ((), ()))
TN = (((0,), (0,)), ((), ()))
CHUNK_LB = 8
STATE_GROUP = 4


def _rwkv_chunk_kernel(pr_ref, pk_ref, pv_ref, pl_ref, qr_ref, qk_ref, qv_ref, ql_ref,
                       sr_ref, sk_ref, sv_ref, sl_ref,
                       mur_ref, muk_ref, muv_ref, mul_ref, w0_ref, w2_ref, a0_ref, a2_ref, g2_ref,
                       kkp_ref, kap_ref, rkp_ref, gnb_ref,
                       y0_ref, rt_ref, bonus_ref, gate_ref, mm_ref, g0_ref, *, n_valid):
    ci = pl.program_id(2)
    c = pr_ref.shape[0]
    row = lax.broadcasted_iota(I32, (c, 1), 0)
    first_chunk = ci == 0

    def mix(p_ref, q_ref, s_ref, mu_ref):
        p = p_ref[...]
        prev = jnp.where(first_chunk, s_ref[...], q_ref[SUBLANE - 1:SUBLANE, :])
        shifted = jnp.where(row == 0, prev, pltpu.roll(p, 1, 0))
        return p + mu_ref[...] * (shifted - p)

    xr = mix(pr_ref, qr_ref, sr_ref, mur_ref)
    xk = mix(pk_ref, qk_ref, sk_ref, muk_ref)
    xv = mix(pv_ref, qv_ref, sv_ref, muv_ref)
    xl = mix(pl_ref, ql_ref, sl_ref, mul_ref)

    dw = xl[:, 0:LORA_W]
    da = xl[:, LORA_W:LORA_W + LORA_A]
    dg = xl[:, LORA_W + LORA_A:LORA_W + LORA_A + LORA_G]
    wlog = -_softplus(-(w0_ref[...] + _dot(jnp.tanh(dw), w2_ref[...]))) - 0.5
    logdec = -jnp.exp(wlog)
    a = _sigmoid(a0_ref[...] + _dot(da, a2_ref[...]))
    gate_ref[...] = _dot(_sigmoid(dg), g2_ref[...])
    kkv = xk * kkp_ref[...]
    kmod = xk * (1.0 + (a - 1.0) * kap_ref[...])
    if n_valid < c:
        live = row < n_valid
        logdec = jnp.where(live, logdec, 0.0)
        kmod = jnp.where(live, kmod, 0.0)
        a = jnp.where(live, a, 0.0)
        xv = jnp.where(live, xv, 0.0)
    cum = _cumsum_rows(logdec)
    cum_ex = cum - logdec
    cum_end = cum[c - 1:c, :]

    ti = lax.broadcasted_iota(I32, (c, c), 0)
    si = lax.broadcasted_iota(I32, (c, c), 1)
    lower_strict = ti > si
    lower_incl = ti >= si
    eye_c = (ti == si).astype(F32)
    eye_k = (lax.broadcasted_iota(I32, (HD_B, HD_B), 0) == lax.broadcasted_iota(I32, (HD_B, HD_B), 1)).astype(F32)
    rk_all = rkp_ref[...]
    gnb = gnb_ref[...]

    heads = range(pr_ref.shape[1] // HD_B)
    sls = [slice(h * HD_B, (h + 1) * HD_B) for h in heads]
    r_ = [xr[:, s] for s in sls]
    k_ = [kmod[:, s] for s in sls]
    v_ = [xv[:, s] for s in sls]
    kk_ = [kkv[:, s] for s in sls]
    kk_ = [x / jnp.maximum(jnp.sqrt(jnp.sum(x * x, axis=-1, keepdims=True)), 1e-12) for x in kk_]
    b_ = [kk_[h] * a[:, sls[h]] for h in heads]
    kap_ = [kk_[h] * jnp.exp(cum_ex[:, sls[h]]) for h in heads]
    rt_ = [r_[h] * jnp.exp(cum[:, sls[h]]) for h in heads]
    inv_ = [jnp.exp(-cum[:, sls[h]]) for h in heads]
    paired = 2 * c == LANE and len(heads) % 2 == 0
    kb_first = [paired and h % 2 == 0 for h in heads]
    amat = [_dot2(jnp.concatenate([kap_[h], rt_[h]], axis=0),
                  jnp.concatenate([b_[h] * inv_[h], k_[h] * inv_[h]] if kb_first[h]
                                  else [k_[h] * inv_[h], b_[h] * inv_[h]], axis=0), NT) for h in heads]
    kcol = [slice(c, 2 * c) if kb_first[h] else slice(0, c) for h in heads]
    bcol = [slice(0, c) if kb_first[h] else slice(c, 2 * c) for h in heads]
    a_kk = [jnp.where(lower_strict, amat[h][0:c, kcol[h]], 0.0) for h in heads]
    a_rk = [jnp.where(lower_incl, amat[h][c:2 * c, kcol[h]], 0.0) for h in heads]
    a_rb = [jnp.where(lower_incl, amat[h][c:2 * c, bcol[h]], 0.0) for h in heads]
    av = [_dot2(jnp.concatenate([a_kk[h], a_rk[h]], axis=0), v_[h]) for h in heads]

    def level_mask(rows_i, cols_i, w):
        return (rows_i // (2 * w) == cols_i // (2 * w)) & ((rows_i // w) % 2 == 1) & ((cols_i // w) % 2 == 0)

    if paired:
        tp = lax.broadcasted_iota(I32, (c, LANE), 0)
        lane_p = lax.broadcasted_iota(I32, (c, LANE), 1)
        sp = lane_p % c
        left = lane_p < c

        left_b = left.astype(BF16)
        right_b = 1.0 - left_b

        def blockdiag(x):
            return jnp.concatenate([x * left_b, x * right_b], axis=0)

        def dot3p(x_hi, x_lo, y_hi, y_lo):
            d = lambda a_, b_: lax.dot_general(a_, b_, (((1,), (0,)), ((), ())), preferred_element_type=F32)
            return d(x_hi, y_hi) + (d(x_hi, y_lo) + d(x_lo, y_hi))

        pairs = range(len(heads) // 2)
        l_pair = [jnp.where(tp > sp, jnp.where(left, amat[2 * p][0:c, :], amat[2 * p + 1][0:c, :]), 0.0)
                  for p in pairs]
        l_split = [_split2(m) for m in l_pair]
        tpair = [(tp == sp).astype(F32) - jnp.where(((tp % 2) == 1) & (sp == tp - 1), m, 0.0) for m in l_pair]
        w = 2
        while w < c:
            off_b = level_mask(tp, sp, w).astype(BF16)
            t_split = [_split2(m) for m in tpair]
            ld = [dot3p(l_split[p][0] * off_b, l_split[p][1] * off_b,
                        blockdiag(t_split[p][0]), blockdiag(t_split[p][1])) for p in pairs]
            ld_split = [_split2(m) for m in ld]
            tpair = [tpair[p] - dot3p(t_split[p][0], t_split[p][1],
                                      blockdiag(ld_split[p][0]), blockdiag(ld_split[p][1])) for p in pairs]
            w *= 2
        tinv = [tpair[h // 2][:, (h % 2) * c:(h % 2 + 1) * c] for h in heads]
    else:
        a_kb = [jnp.where(lower_strict, amat[h][0:c, bcol[h]], 0.0) for h in heads]
        tinv = [eye_c - jnp.where(((ti % 2) == 1) & (si == ti - 1), m, 0.0) for m in a_kb]
        w = 2
        while w < c:
            off = level_mask(ti, si, w)
            ld = [_dot_t(jnp.where(off, a_kb[h], 0.0), tinv[h]) for h in heads]
            tinv = [tinv[h] - _dot_t(tinv[h], ld[h]) for h in heads]
            w *= 2

    tx = [_dot2(tinv[h], jnp.concatenate([av[h][0:c], kap_[h]], axis=1)) for h in heads]
    arb_tx = [_dot2(a_rb[h], tx[h]) for h in heads]
    dec_end = [jnp.exp(cum_end[:, sls[h]] - cum[:, sls[h]]) for h in heads]
    k_end = [k_[h] * dec_end[h] for h in heads]
    b_end = [b_[h] * dec_end[h] for h in heads]
    g0 = [_dot2(jnp.concatenate([v_[h], -tx[h][:, 0:HD_B]], axis=0),
                jnp.concatenate([k_end[h], b_end[h]], axis=0), TN) for h in heads]
    ktb = [_dot2(tx[h][:, HD_B:2 * HD_B], b_end[h], TN) for h in heads]
    for h in heads:
        s = sls[h]
        y0_ref[:, s] = av[h][c:2 * c] - arb_tx[h][:, 0:HD_B]
        rt_ref[:, s] = rt_[h] - arb_tx[h][:, HD_B:2 * HD_B]
        bonus_ref[:, s] = jnp.sum(r_[h] * k_[h] * rk_all[:, s], axis=-1, keepdims=True) * v_[h] + gnb[:, s]
        mm_ref[h] = eye_k * jnp.exp(cum_end[:, s]) - ktb[h]
        g0_ref[h] = g0[h]


def _rwkv_state_kernel(y0_ref, rt_ref, bonus_ref, gate_ref, mm_ref, g0_ref, s0_ref, gng_ref,
                       o_ref, sT_ref, st_ref):
    @pl.when(pl.program_id(1) == 0)
    def _():
        st_ref[...] = s0_ref[...]

    gng = gng_ref[...]
    for h0 in range(0, N_HEADS_B, STATE_GROUP):
        heads = range(h0, h0 + STATE_GROUP)
        sls = {h: slice(h * HD_B, (h + 1) * HD_B) for h in heads}
        s0 = {h: st_ref[h] for h in heads}
        s_new = {h: _dot2(s0[h], mm_ref[h]) + g0_ref[h] for h in heads}
        y = {h: y0_ref[:, sls[h]] + _dot2(rt_ref[:, sls[h]], s0[h], NT) for h in heads}
        for h in heads:
            st_ref[h] = s_new[h]
            sT_ref[h] = s_new[h]
        mean = {h: jnp.mean(y[h], axis=-1, keepdims=True) for h in heads}
        dev = {h: y[h] - mean[h] for h in heads}
        var = {h: jnp.mean(jnp.square(dev[h]), axis=-1, keepdims=True) for h in heads}
        for h in heads:
            yn = dev[h] * lax.rsqrt(var[h] + GN_EPS) * gng[:, sls[h]]
            o_ref[:, sls[h]] = ((yn + bonus_ref[:, sls[h]]) * gate_ref[:, sls[h]]).astype(o_ref.dtype)


def rwkv7(pp, shift0, wkv0, prm, bsz, t, chunk, n_valid):
    nc = t // chunk
    lb = CHUNK_LB * LANE
    ngrp = D_B // lb
    hs = lb // HD_B
    assert A_RKV % lb == 0
    rb = A_RKV // lb
    per = D_B // lb
    lo_blk = A_LO // (2 * LANE)
    lo3 = 3 * D_B // (2 * LANE)
    sub = chunk // SUBLANE

    def rows(off):
        return pl.BlockSpec((chunk, lb), lambda b, hg, ci: (b * nc + ci, rb + off * per + hg))

    def prev_rows(off):
        return pl.BlockSpec((SUBLANE, lb),
                            lambda b, hg, ci: (jnp.maximum((b * nc + ci) * sub - 1, 0), rb + off * per + hg))

    def srow(off):
        return pl.BlockSpec((None, 1, lb), lambda b, hg, ci: (b, 0, off * per + hg))

    def prow(off):
        return pl.BlockSpec((1, lb), lambda b, hg, ci: (0, off * per + hg))

    def per_head(shape0):
        return pl.BlockSpec((shape0, lb), lambda b, hg, ci: (0, hg))

    lo_spec = lambda rws, imap: pl.BlockSpec((rws, 2 * LANE), imap)
    in_specs = [rows(0), rows(1), rows(2), lo_spec(chunk, lambda b, hg, ci: (b * nc + ci, lo_blk)),
                prev_rows(0), prev_rows(1), prev_rows(2),
                lo_spec(SUBLANE, lambda b, hg, ci: (jnp.maximum((b * nc + ci) * sub - 1, 0), lo_blk)),
                srow(0), srow(1), srow(2), pl.BlockSpec((None, 1, 2 * LANE), lambda b, hg, ci: (b, 0, lo3)),
                prow(0), prow(1), prow(2), lo_spec(1, lambda b, hg, ci: (0, lo3)),
                per_head(1), per_head(LORA_W), per_head(1), per_head(LORA_A), per_head(LORA_G),
                per_head(1), per_head(1), per_head(1), per_head(1)]
    row_out = pl.BlockSpec((chunk, lb), lambda b, hg, ci: (b * nc + ci, hg))
    mat_out = pl.BlockSpec((None, None, hs, HD_B, HD_B), lambda b, hg, ci: (b, ci, hg, 0, 0))
    n = bsz * t
    y0, rt, bonus, gate, mm, g0 = pl.pallas_call(
        functools.partial(_rwkv_chunk_kernel, n_valid=n_valid),
        out_shape=[jax.ShapeDtypeStruct((n, D_B), F32)] * 4
        + [jax.ShapeDtypeStruct((bsz, nc, N_HEADS_B, HD_B, HD_B), F32)] * 2,
        grid=(bsz, ngrp, nc),
        in_specs=in_specs,
        out_specs=[row_out] * 4 + [mat_out] * 2,
        compiler_params=_cparams(("parallel", "parallel", "parallel")),
        name="rwkv_chunk",
    )(pp, pp, pp, pp, pp, pp, pp, pp, shift0, shift0, shift0, shift0,
      prm["mu"], prm["mu"], prm["mu"], prm["mu"], prm["w0"], prm["w2"], prm["a0"], prm["a2"], prm["g2"],
      prm["kk"], prm["ka"], prm["rk"], prm["gn_b"])

    row_in = pl.BlockSpec((chunk, D_B), lambda b, ci: (b * nc + ci, 0))
    mat_in = pl.BlockSpec((None, None, N_HEADS_B, HD_B, HD_B), lambda b, ci: (b, ci, 0, 0, 0))
    state = pl.BlockSpec((None, N_HEADS_B, HD_B, HD_B), lambda b, ci: (b, 0, 0, 0))
    o_b, s_fin = pl.pallas_call(
        _rwkv_state_kernel,
        out_shape=[jax.ShapeDtypeStruct((n, D_B), BF16),
                   jax.ShapeDtypeStruct((bsz, N_HEADS_B, HD_B, HD_B), F32)],
        grid=(bsz, nc),
        in_specs=[row_in, row_in, row_in, row_in, mat_in, mat_in, state,
                  pl.BlockSpec((1, D_B), lambda b, ci: (0, 0))],
        out_specs=[row_in, state],
        scratch_shapes=[pltpu.VMEM((N_HEADS_B, HD_B, HD_B), F32)],
        compiler_params=_cparams(("parallel", "arbitrary")),
        name="rwkv_state",
    )(y0, rt, bonus, gate, mm, g0, wkv0, prm["gn_g"])
    return o_b, s_fin


def _first_lane(cond, lane):
    return jnp.min(jnp.where(cond, lane, 4 * LANE), axis=-1, keepdims=True)


def _router_kernel(h_ref, w_ref, b_ref, eid_ref, wt_ref):
    logits = _dot3(h_ref[...], w_ref[...]) + b_ref[...]
    lane = lax.broadcasted_iota(I32, logits.shape, 1)
    gmask = lane < N_GROUPS
    lg = jnp.where(gmask, logits, NEG)
    eg = jnp.where(gmask, jnp.exp(lg - jnp.max(lg, axis=-1, keepdims=True)), 0.0)
    gp = eg / jnp.sum(eg, axis=-1, keepdims=True)
    g_w = jnp.max(gp, axis=-1, keepdims=True)
    grp = _first_lane(gmask & (gp == g_w), lane)
    lo = N_GROUPS + grp * E_PER_GROUP
    emask = (lane >= lo) & (lane < lo + E_PER_GROUP)
    le = jnp.where(emask, logits, NEG)
    ee = jnp.where(emask, jnp.exp(le - jnp.max(le, axis=-1, keepdims=True)), 0.0)
    ep = ee / jnp.sum(ee, axis=-1, keepdims=True)
    p1 = jnp.max(jnp.where(emask, ep, -1.0), axis=-1, keepdims=True)
    i1 = _first_lane(emask & (ep == p1), lane)
    rest = emask & (lane != i1)
    p2 = jnp.max(jnp.where(rest, ep, -1.0), axis=-1, keepdims=True)
    i2 = _first_lane(rest & (ep == p2), lane)
    tot = p1 + p2
    eid_ref[...] = jnp.where(lane == 0, i1 - N_GROUPS, jnp.where(lane == 1, i2 - N_GROUPS, 0))
    wt_ref[...] = jnp.where(lane == 0, g_w * p1 / tot, jnp.where(lane == 1, g_w * p2 / tot, 0.0))


def router(h2, wr, br):
    n, d = h2.shape
    tm = _row_tile(n, 512)
    return pl.pallas_call(
        _router_kernel,
        out_shape=[jax.ShapeDtypeStruct((n, LANE), I32), jax.ShapeDtypeStruct((n, LANE), F32)],
        grid=(n // tm,),
        in_specs=[pl.BlockSpec((tm, d), lambda i: (i, 0)),
                  pl.BlockSpec((d, LANE), lambda i: (0, 0)),
                  pl.BlockSpec((1, LANE), lambda i: (0, 0))],
        out_specs=[pl.BlockSpec((tm, LANE), lambda i: (i, 0)), pl.BlockSpec((tm, LANE), lambda i: (i, 0))],
        compiler_params=_cparams(("parallel",)),
        name="router",
    )(h2, wr, br)


def _row_copy(src_hbm, dst_vmem, sem, src_row, dst_row):
    return pltpu.make_async_copy(src_hbm.at[pl.ds(src_row * ROW_TILES, ROW_TILES), :],
                                 dst_vmem.at[pl.ds(dst_row * ROW_TILES, ROW_TILES), :], sem)


def _experts_kernel(blk_e_ref, n_used_ref, dest_ref, h_hbm, w1_ref, w3_ref, w2_ref, y_ref,
                    x_even, x_odd, w1b_ref, w3b_ref, w2b_ref, sem, tok_ref):
    i = pl.program_id(0)
    bm = x_even.shape[0] // ROW_TILES
    n_used = n_used_ref[0]
    bufs = (x_even, x_odd)

    @pl.when(i == 0)
    def _():
        def clear(j, carry):
            tok_ref[j] = 0
            return carry

        def put(p, carry):
            tok_ref[dest_ref[p]] = p // 2
            return carry

        lax.fori_loop(0, tok_ref.shape[0], clear, 0)
        lax.fori_loop(0, dest_ref.shape[0], put, 0)

    def rows(blk, slot, go):
        for r in range(bm):
            go(_row_copy(h_hbm, bufs[slot], sem.at[slot], tok_ref[blk * bm + r], r))

    def block(slot):
        @pl.when(i == 0)
        def _():
            rows(0, 0, lambda cp: cp.start())

        rows(i, slot, lambda cp: cp.wait())

        @pl.when((i == 0) | (blk_e_ref[i] != blk_e_ref[jnp.maximum(i - 1, 0)]))
        def _():
            w1b_ref[...] = w1_ref[...].astype(BF16)
            w3b_ref[...] = w3_ref[...].astype(BF16)
            w2b_ref[...] = w2_ref[...].astype(BF16)

        nxt = jnp.minimum(i + 1, n_used - 1)
        rows(nxt, 1 - slot, lambda cp: cp.start())
        x = jnp.concatenate([_from_row_tiles(bufs[slot], bm, c).astype(BF16) for c in range(ROW_TILES)], axis=1)
        h1 = _dot(x, w1b_ref[...])
        h3 = _dot(x, w3b_ref[...])
        act = h1 * _sigmoid(h1) * h3
        _to_row_tiles(y_ref, _dot(act, w2b_ref[...]))

        @pl.when(i == n_used - 1)
        def _():
            rows(nxt, 1 - slot, lambda cp: cp.wait())

    for slot in range(2):
        pl.when((i < n_used) & (i % 2 == slot))(functools.partial(block, slot))

    @pl.when(i >= n_used)
    def _():
        y_ref[...] = jnp.zeros_like(y_ref)


def experts(h_all, blk_e, n_used, dest, w1, w3, w2, l, bm):
    n_blocks = blk_e.shape[0]
    d = D_MODEL
    grid_spec = pltpu.PrefetchScalarGridSpec(
        num_scalar_prefetch=3,
        grid=(n_blocks,),
        in_specs=[pl.BlockSpec(memory_space=pl.ANY),
                  pl.BlockSpec((None, None, d, D_EXPERT), lambda i, be, nu, tk: (l, be[i], 0, 0)),
                  pl.BlockSpec((None, None, d, D_EXPERT), lambda i, be, nu, tk: (l, be[i], 0, 0)),
                  pl.BlockSpec((None, None, D_EXPERT, d), lambda i, be, nu, tk: (l, be[i], 0, 0))],
        out_specs=pl.BlockSpec((bm * ROW_TILES, ROW_W), lambda i, be, nu, tk: (i, 0)),
        scratch_shapes=[pltpu.VMEM((bm * ROW_TILES, ROW_W), F32), pltpu.VMEM((bm * ROW_TILES, ROW_W), F32),
                        pltpu.VMEM((d, D_EXPERT), BF16), pltpu.VMEM((d, D_EXPERT), BF16),
                        pltpu.VMEM((D_EXPERT, d), BF16), pltpu.SemaphoreType.DMA((2,)),
                        pltpu.SMEM((n_blocks * bm,), I32)],
    )
    return pl.pallas_call(
        _experts_kernel,
        out_shape=jax.ShapeDtypeStruct((n_blocks * bm * ROW_TILES, ROW_W), F32),
        grid_spec=grid_spec,
        compiler_params=_cparams(("arbitrary",)),
        name="experts",
    )(blk_e, n_used, dest, h_all, w1, w3, w2)


def _combine_kernel(dest_ref, ys_hbm, wt_ref, x_ref, g_ref, o_ref, buf, sem):
    tm = x_ref.shape[0]
    step = pl.program_id(0) * pl.num_programs(1) + pl.program_id(1)
    n_steps = pl.num_programs(0) * pl.num_programs(1)

    def rows(tile, slot, go):
        for r in range(tm):
            base = (tile * tm + r) * 2
            go(_row_copy(ys_hbm, buf.at[slot, 0], sem.at[slot], dest_ref[base], r))
            go(_row_copy(ys_hbm, buf.at[slot, 1], sem.at[slot], dest_ref[base + 1], r))

    slot = step % 2

    @pl.when(step == 0)
    def _():
        rows(0, 0, lambda cp: cp.start())

    rows(step, slot, lambda cp: cp.wait())
    nxt = jnp.minimum(step + 1, n_steps - 1)
    rows(nxt, 1 - slot, lambda cp: cp.start())
    wt = wt_ref[...]
    w0, w1 = wt[:, 0:1], wt[:, 1:2]
    for c in range(ROW_TILES):
        cs = slice(c * ROW_W, (c + 1) * ROW_W)
        moe = w0 * _from_row_tiles(buf.at[slot, 0], tm, c) + w1 * _from_row_tiles(buf.at[slot, 1], tm, c)
        o_ref[:, cs] = x_ref[:, cs] + g_ref[:, cs] * moe

    @pl.when(step == n_steps - 1)
    def _():
        rows(nxt, 1 - slot, lambda cp: cp.wait())


def combine(ys, dest, wts, x, gate):
    b, t, d = x.shape
    tm = _row_tile(t, 128)
    nt = t // tm
    grid_spec = pltpu.PrefetchScalarGridSpec(
        num_scalar_prefetch=1,
        grid=(b, nt),
        in_specs=[pl.BlockSpec(memory_space=pl.ANY),
                  pl.BlockSpec((None, tm, LANE), lambda i, j, ds: (i, j, 0)),
                  pl.BlockSpec((None, tm, d), lambda i, j, ds: (i, j, 0)),
                  pl.BlockSpec((None, 1, d), lambda i, j, ds: (i, 0, 0))],
        out_specs=pl.BlockSpec((None, tm, d), lambda i, j, ds: (i, j, 0)),
        scratch_shapes=[pltpu.VMEM((2, 2, tm * ROW_TILES, ROW_W), F32), pltpu.SemaphoreType.DMA((2,))],
    )
    return pl.pallas_call(
        _combine_kernel,
        out_shape=jax.ShapeDtypeStruct((b, t, d), F32),
        grid_spec=grid_spec,
        compiler_params=_cparams(("arbitrary", "arbitrary")),
        name="moe_combine",
    )(dest.reshape(-1), ys, wts.reshape(b, t, LANE), x, gate.reshape(b, 1, d))


def _dispatch_tables(eid, bm):
    n = eid.shape[0]
    nk = n * 2
    n_blocks = (nk + N_EXPERTS * (bm - 1) + bm - 1) // bm
    flat_e = eid.reshape(-1)
    onehot = (flat_e[:, None] == jnp.arange(N_EXPERTS, dtype=I32)[None, :]).astype(I32)
    csum = jnp.cumsum(onehot, axis=0)
    counts = csum[-1]
    rank = jnp.sum(onehot * (csum - 1), axis=1)
    padded = (counts + bm - 1) // bm * bm
    pad_end = jnp.cumsum(padded)
    pad_start = pad_end - padded
    dest = (pad_start[flat_e] + rank).astype(I32)
    blk_start = jnp.arange(n_blocks, dtype=I32) * bm
    blk_e = jnp.minimum(jnp.sum((pad_end[None, :] <= blk_start[:, None]).astype(I32), axis=1), N_EXPERTS - 1)
    n_used = (pad_end[-1] // bm).astype(I32).reshape(1)
    return blk_e, n_used, dest.reshape(n, 2)


def _final_norm_kernel(x_ref, g_ref, o_ref):
    x = x_ref[...]
    o_ref[...] = x * lax.rsqrt(jnp.mean(x * x, axis=-1, keepdims=True) + RMS_EPS) * g_ref[...]


def final_norm(x, g):
    b, t, d = x.shape
    tr = min(t, 256)
    return pl.pallas_call(
        _final_norm_kernel,
        out_shape=jax.ShapeDtypeStruct((b, t, d), F32),
        grid=(b, t // tr),
        in_specs=[pl.BlockSpec((None, tr, d), lambda i, j: (i, j, 0)), pl.BlockSpec((1, d), lambda i, j: (0, 0))],
        out_specs=pl.BlockSpec((None, tr, d), lambda i, j: (i, j, 0)),
        compiler_params=_cparams(("parallel", "parallel")),
        name="final_norm",
    )(x, g.reshape(1, d))


def _align_in_cols(w, axis):
    def take(a, b_):
        return lax.slice_in_dim(w, a, b_, axis=axis)

    def zeros(nz):
        shp = list(w.shape)
        shp[axis] = nz
        return jnp.zeros(shp, w.dtype)

    gate_parts = []
    for g in range(N_KV):
        gate_parts += [take(OFF_GATE_A + g * 3 * GQA_R, OFF_GATE_A + (g + 1) * 3 * GQA_R), zeros(LANE - 3 * GQA_R)]
    parts = [take(0, OFF_GATE_A)] + gate_parts + [take(OFF_RWKV + 3 * D_B, OFF_MERGE), zeros(RW_W - SHIFT_W),
                                                   take(OFF_RWKV, OFF_RWKV + 3 * D_B),
                                                   take(OFF_MERGE, OFF_MERGE + 2 * D_MODEL)]
    return jnp.concatenate(parts, axis=axis)


def _rwkv_cols(p):
    return jnp.concatenate([p[..., A_RKV:A_RKV + 3 * D_B], p[..., A_LO:A_LO + SHIFT_W - 3 * D_B]], axis=-1)


def _pad_lanes(v, width):
    return jnp.pad(v, [(0, 0)] * (v.ndim - 1) + [(0, width - v.shape[-1])])


def kernel(x_prompt, x_sample, cache_cmp, cache_slc, cache_win, state_shift, state_wkv, page_table, c_prompt, c_sample, ln1_g, ln2_g, ada_w, ada_b, w_in, cmp_w1, cmp_b1, cmp_w2, rwkv_mu, rwkv_w0, rwkv_w2, rwkv_a0, rwkv_a2, rwkv_g2, rwkv_kk, rwkv_ka, rwkv_rk, rwkv_gn_g, rwkv_gn_b, w_branch, w_out, router_g_w, router_g_b, router_e_w, router_e_b, exp_w1, exp_w3, exp_w2, final_g):
    depth = w_in.shape[0]
    bp, t, d = x_prompt.shape
    bd, ts, _ = x_sample.shape
    assert ts == 1 and t % KC == 0 and t >= WINDOW + QB
    n_pages = page_table.shape[1]
    past = n_pages * PAGE
    wb = cache_win.shape[2]
    n_phys = cache_cmp.shape[1]
    n_p = bp * t

    slopes = jnp.exp2(-8.0 * (jnp.arange(N_HEADS_A, dtype=F32) + 1.0) / N_HEADS_A)
    c_rows = bp + bd
    c16 = jnp.zeros(((c_rows + SUBLANE - 1) // SUBLANE * SUBLANE, d), F32).at[:bp].set(c_prompt).at[bp:c_rows].set(c_sample)
    mod = adaln(c16, ada_w, ada_b).reshape(depth, c16.shape[0], 6, d)

    xp, xs = x_prompt, x_sample
    outs = {k: [] for k in ("cmp_p", "slc_p", "win_p", "shf_p", "wkv_p", "cmp_s", "slc_s", "win_s", "shf_s", "wkv_s")}
    pool_cmp2d = cache_cmp.reshape(-1, HD_A)
    pool_slc2d = cache_slc.reshape(-1, HD_A)
    win2d = cache_win.reshape(-1, HD_A)

    for l in range(depth):
        mp, ms = mod[l, :bp], mod[l, bp:c_rows]
        w_in_al = _align_in_cols(w_in[l], 1).astype(BF16)
        w1b = cmp_w1[l].astype(BF16)
        w_out_b = w_out[l].astype(BF16)
        prm = {
            "mu": _pad_lanes(rwkv_mu[l][None, :], RW_W),
            "w0": rwkv_w0[l][None, :], "w2": rwkv_w2[l], "a0": rwkv_a0[l][None, :], "a2": rwkv_a2[l],
            "g2": rwkv_g2[l], "kk": rwkv_kk[l][None, :], "ka": rwkv_ka[l][None, :],
            "rk": rwkv_rk[l].reshape(1, D_B), "gn_g": rwkv_gn_g[l][None, :], "gn_b": rwkv_gn_b[l][None, :],
        }
        wr = _pad_lanes(jnp.concatenate([router_g_w[l], router_e_w[l]], axis=1), LANE)
        br = _pad_lanes(jnp.concatenate([router_g_b[l], router_e_b[l]])[None, :], LANE)

        hp_ = norm_mod(xp, ln1_g[l], mp[:, 0], mp[:, 1], BF16).reshape(n_p, d)
        pp, ppb, kv_p = in_proj(hp_, w_in_al)
        cmp_kv = compress(pp, A_KV, 1, w1b, cmp_b1[l], cmp_w2[l])
        o_a = nsa_prompt(slopes, ppb, pp, cmp_kv, bp, t)
        o_b, s_fin = rwkv7(pp, jnp.zeros((bp, 1, RW_W), F32), jnp.zeros((bp, N_HEADS_B, HD_B, HD_B), F32),
                           prm, bp, t, 64, 64)
        merged = branch_merge(o_a, o_b, w_branch, l, pp)
        xp = out_proj_residual(merged.reshape(bp, t, d), w_out_b, xp, mp[:, 2])
        kv_p = kv_p.reshape(3, bp, t, 2, N_KV, HD_A)
        outs["cmp_p"].append(kv_p[0])
        outs["slc_p"].append(kv_p[1])
        outs["win_p"].append(kv_p[2, :, t - min(WINDOW, t):])
        outs["shf_p"].append(_rwkv_cols(pp.reshape(bp, t, IN_AL)[:, -1]))
        outs["wkv_p"].append(s_fin)

        hs_ = norm_mod(xs, ln1_g[l], ms[:, 0], ms[:, 1], BF16).reshape(bd, d)
        ps, psb, kv_s = in_proj(hs_, w_in_al)
        cmp_past = compress_paged(page_table, pool_cmp2d, l * n_phys, w1b, cmp_b1[l], cmp_w2[l])
        new_rows = jnp.zeros((bd, BLK, 2 * KV_W), F32).at[:, 0].set(ps[:, A_KV:A_KV + 2 * KV_W])
        cmp_new = compress(new_rows.reshape(bd * BLK, 2 * KV_W), 0, 1, w1b, cmp_b1[l], cmp_w2[l])
        nb_past = past // BLK
        nb_real = nb_past + 1
        nbp = (nb_real + LANE - 1) // LANE * LANE
        cmp_s = jnp.concatenate([cmp_past.reshape(2, N_KV, bd, nb_past, HD_A), cmp_new[:, :, :, None, :],
                                 jnp.zeros((2, N_KV, bd, nbp - nb_real, HD_A), F32)], axis=3)
        ps3 = ps.reshape(bd, 1, IN_AL)
        psb3 = psb.reshape(bd, 1, A_GATE)
        o_cmp, idx_full = nsa_sample_cmp(slopes, psb3, cmp_s, past, nb_real)
        idx_flat = idx_full[:, :, :N_SEL, 0].reshape(-1)
        o_a_s = nsa_sample_sel(idx_flat, page_table, slopes, psb3, ps3, pool_slc2d, l * n_phys,
                               win2d, l * bd, wb, o_cmp, past).reshape(bd, D_A)
        ps_pad = jnp.zeros((bd, SUBLANE, IN_AL), F32).at[:, 0].set(ps).reshape(bd * SUBLANE, IN_AL)
        o_b_s, s_fin_s = rwkv7(ps_pad, _pad_lanes(state_shift[l], RW_W)[:, None, :], state_wkv[l],
                               prm, bd, SUBLANE, SUBLANE, 1)
        o_b_s = o_b_s.reshape(bd, SUBLANE, D_B)[:, 0]
        merged_s = branch_merge(o_a_s, o_b_s, w_branch, l, ps)
        xs = out_proj_residual(merged_s.reshape(bd, 1, d), w_out_b, xs, ms[:, 2])
        kv_s = kv_s.reshape(3, bd, 1, 2, N_KV, HD_A)
        outs["cmp_s"].append(kv_s[0])
        outs["slc_s"].append(kv_s[1])
        outs["win_s"].append(jnp.concatenate([cache_win[l, :, 1:], kv_s[2]], axis=1))
        outs["shf_s"].append(_rwkv_cols(ps))
        outs["wkv_s"].append(s_fin_s)

        h2p, h_all = norm_mod(xp, ln2_g[l], mp[:, 3], mp[:, 4], F32, tiles_rows=n_p + bd,
                              tiles_into=jnp.zeros(((n_p + bd) * ROW_TILES, ROW_W), F32))
        h2s, h_all = norm_mod(xs, ln2_g[l], ms[:, 3], ms[:, 4], F32, tiles_rows=n_p + bd, tiles_into=h_all,
                              tiles_row0=n_p)
        eid_p, wt_p = router(h2p.reshape(n_p, d), wr, br)
        eid_s, wt_s = router(h2s.reshape(bd, d), wr, br)
        eid = jnp.concatenate([eid_p[:, :2], eid_s[:, :2]], axis=0)
        blk_e, n_used, dest = _dispatch_tables(eid, MOE_BLOCK)
        ys = experts(h_all, blk_e, n_used, dest.reshape(-1), exp_w1, exp_w3, exp_w2, l, MOE_BLOCK)
        xp = combine(ys, dest[:n_p], wt_p, xp, mp[:, 5])
        xs = combine(ys, dest[n_p:], wt_s, xs, ms[:, 5])

    y_prompt = final_norm(xp, final_g)
    y_sample = final_norm(xs, final_g)
    st = lambda k: jnp.stack(outs[k])
    return (y_prompt, y_sample, st("cmp_p"), st("slc_p"), st("win_p"), st("shf_p"), st("wkv_p"),
            st("cmp_s"), st("slc_s"), st("win_s"), st("shf_s"), st("wkv_s"))
```
